```python
import math
import jax, jax.numpy as jnp
from jax import lax
import numpy as np

D_MODEL = 1024
BATCH = 2
SEQ = 8192
DEPTH = 1

D_CONV = D_MODEL
CONV_WIDTH = 3
HEAD_DIM = 64
N_Q_HEADS = D_MODEL // HEAD_DIM
N_KV_HEADS = 4
GQA_GROUP = N_Q_HEADS // N_KV_HEADS
WINDOW = 128
BLOCK = 128
N_EXPERTS = 32
TOP_K = 4
D_EXPERT = D_MODEL
SWIGLU_LIMIT = 7.0
SWIGLU_ALPHA = 1.702
EXPERT_CHUNK = 256
D_PLE = 256
DN_ALPHA = (2.0 * DEPTH) ** 0.25
DN_BETA = (8.0 * DEPTH) ** -0.25
LN_EPS = 1e-5

IN_SIZES = [D_CONV, D_CONV, D_CONV,
            N_Q_HEADS * HEAD_DIM, N_KV_HEADS * HEAD_DIM, N_KV_HEADS * HEAD_DIM,
            D_MODEL, D_MODEL]
IN_TOTAL = sum(IN_SIZES)
IN_SPLITS = [int(s) for s in np.cumsum(IN_SIZES)[:-1]]
V_START = 3 * D_CONV + N_Q_HEADS * HEAD_DIM + N_KV_HEADS * HEAD_DIM
V_END = V_START + N_KV_HEADS * HEAD_DIM

kernel_name = "hybrid_conv_swa_moe_deepnorm"


def layer_norm(x, g, b):
    xf = x.astype(jnp.float32)
    mu = jnp.mean(xf, axis=-1, keepdims=True)
    var = jnp.mean(jnp.square(xf - mu), axis=-1, keepdims=True)
    y = (xf - mu) * lax.rsqrt(var + LN_EPS) * g.astype(jnp.float32) + b.astype(jnp.float32)
    return y.astype(x.dtype)


def short_conv_mixer(b_gate, c_gate, h, conv_w):
    u = c_gate * h
    S = u.shape[1]
    up = jnp.pad(u, ((0, 0), (CONV_WIDTH - 1, 0), (0, 0)))
    y = sum(conv_w[k] * up[:, k:k + S] for k in range(CONV_WIDTH))
    return b_gate * y


def sliding_window_attention(q, k, v, sinks):
    B, S, _ = q.shape
    nb = S // BLOCK
    qb = q.reshape(B, nb, BLOCK, N_KV_HEADS, GQA_GROUP, HEAD_DIM)
    k4 = k.reshape(B, S, N_KV_HEADS, HEAD_DIM)
    v4 = v.reshape(B, S, N_KV_HEADS, HEAD_DIM)
    pad = ((0, 0), (BLOCK, 0), (0, 0), (0, 0))
    kp = jnp.pad(k4, pad)[:, :S].reshape(B, nb, BLOCK, N_KV_HEADS, HEAD_DIM)
    vp = jnp.pad(v4, pad)[:, :S].reshape(B, nb, BLOCK, N_KV_HEADS, HEAD_DIM)
    kb = jnp.concatenate([kp, k4.reshape(B, nb, BLOCK, N_KV_HEADS, HEAD_DIM)], axis=2)
    vb = jnp.concatenate([vp, v4.reshape(B, nb, BLOCK, N_KV_HEADS, HEAD_DIM)], axis=2)
    scale = 1.0 / math.sqrt(HEAD_DIM)
    s = jnp.einsum('bnqhgd,bnkhd->bnhgqk', qb, kb).astype(jnp.float32) * scale
    qi = jnp.arange(BLOCK)[:, None] + BLOCK
    kj = jnp.arange(2 * BLOCK)[None, :]
    rel = qi - kj
    band = (rel >= 0) & (rel < WINDOW)
    valid_key = (jnp.arange(nb)[:, None] * BLOCK - BLOCK + kj) >= 0
    mask = band[None] & valid_key[:, None, :]
    s = jnp.where(mask[None, :, None, None], s, -jnp.inf)
    sink = sinks.astype(jnp.float32).reshape(1, 1, N_KV_HEADS, GQA_GROUP, 1, 1)
    m = jnp.maximum(jnp.max(s, axis=-1, keepdims=True), sink)
    e = jnp.exp(s - m)
    denom = jnp.sum(e, axis=-1, keepdims=True) + jnp.exp(sink - m)
    probs = (e / denom).astype(q.dtype)
    o = jnp.einsum('bnhgqk,bnkhd->bnqhgd', probs, vb)
    return o.reshape(B, S, N_Q_HEADS * HEAD_DIM)


def routed_experts(x, w_router, b_router, w_gu, b_gu, w_down, b_down):
    B, S, D = x.shape
    N = B * S
    t = x.reshape(N, D)
    logits = (t @ w_router + b_router).astype(jnp.float32)
    top_vals, top_idx = lax.top_k(logits, TOP_K)
    top_w = jax.nn.softmax(top_vals, axis=-1)
    A = N * TOP_K
    flat_e = top_idx.reshape(A)
    flat_tok = jnp.repeat(jnp.arange(N, dtype=jnp.int32), TOP_K)
    flat_w = top_w.reshape(A)
    order = jnp.argsort(flat_e, stable=True)
    sorted_e = flat_e[order]
    counts = jnp.bincount(flat_e, length=N_EXPERTS)
    padded = ((counts + EXPERT_CHUNK - 1) // EXPERT_CHUNK) * EXPERT_CHUNK
    start_sorted = jnp.cumsum(counts) - counts
    end_padded = jnp.cumsum(padded)
    start_padded = end_padded - padded
    rank = jnp.arange(A) - start_sorted[sorted_e]
    dest = start_padded[sorted_e] + rank
    P = A + N_EXPERTS * EXPERT_CHUNK
    P = ((P + EXPERT_CHUNK - 1) // EXPERT_CHUNK) * EXPERT_CHUNK
    n_chunks = P // EXPERT_CHUNK
    tok_buf = jnp.full((P,), N, dtype=jnp.int32).at[dest].set(flat_tok[order])
    w_buf = jnp.zeros((P,), jnp.float32).at[dest].set(flat_w[order])
    chunk_start = jnp.arange(n_chunks) * EXPERT_CHUNK
    chunk_e = jnp.clip(jnp.searchsorted(end_padded, chunk_start, side='right'), 0, N_EXPERTS - 1)
    t_pad = jnp.concatenate([t, jnp.zeros((1, D), t.dtype)], axis=0)
    xs = t_pad[tok_buf].reshape(n_chunks, EXPERT_CHUNK, D)

    def expert_chunk(args):
        xc, e = args
        gu = xc @ w_gu[e] + b_gu[e]
        gate, up = gu[:, :D_EXPERT], gu[:, D_EXPERT:]
        gate = jnp.minimum(gate, SWIGLU_LIMIT)
        up = jnp.clip(up, -SWIGLU_LIMIT, SWIGLU_LIMIT)
        h = (up + 1.0) * gate * jax.nn.sigmoid(SWIGLU_ALPHA * gate)
        return h @ w_down[e] + b_down[e]

    ys = lax.map(expert_chunk, (xs, chunk_e)).reshape(P, D)
    ys = ys * w_buf[:, None].astype(ys.dtype)
    out = jnp.zeros((N + 1, D), ys.dtype).at[tok_buf].add(ys)[:N]
    return out.reshape(B, S, D).astype(x.dtype)


def setup_inputs(seed: int = 0) -> dict:
    key = jax.random.key(seed)
    ks = jax.random.split(key, 24)
    f32 = jnp.float32
    nrm = lambda k, shape, s: jax.random.normal(k, shape, f32) * s
    L = DEPTH
    w_in = nrm(ks[2], (L, D_MODEL, IN_TOTAL), D_MODEL ** -0.5)
    w_in = w_in.at[:, :, V_START:V_END].multiply(DN_BETA)
    return {
        "x": nrm(ks[0], (BATCH, SEQ, D_MODEL), 1.0),
        "p": nrm(ks[1], (DEPTH, BATCH, SEQ, D_PLE), 1.0),
        "w_in": w_in,
        "conv_w": nrm(ks[3], (L, CONV_WIDTH, D_CONV), CONV_WIDTH ** -0.5),
        "w_br_conv": nrm(ks[4], (L, D_CONV, D_MODEL), D_CONV ** -0.5 * DN_BETA),
        "w_br_attn": nrm(ks[5], (L, N_Q_HEADS * HEAD_DIM, D_MODEL), (N_Q_HEADS * HEAD_DIM) ** -0.5 * DN_BETA),
        "attn_sinks": nrm(ks[6], (L, N_Q_HEADS), 0.5),
        "w_out": nrm(ks[7], (L, D_MODEL, D_MODEL), D_MODEL ** -0.5 * DN_BETA),
        "ln1_g": 1.0 + nrm(ks[8], (L, D_MODEL), 0.02),
        "ln1_b": nrm(ks[9], (L, D_MODEL), 0.02),
        "w_router": nrm(ks[10], (L, D_MODEL, N_EXPERTS), D_MODEL ** -0.5),
        "b_router": nrm(ks[11], (L, N_EXPERTS), 0.01),
        "w_gu": nrm(ks[12], (L, N_EXPERTS, D_MODEL, 2 * D_EXPERT), D_MODEL ** -0.5 * DN_BETA),
        "b_gu": nrm(ks[13], (L, N_EXPERTS, 2 * D_EXPERT), 0.01),
        "w_down": nrm(ks[14], (L, N_EXPERTS, D_EXPERT, D_MODEL), D_EXPERT ** -0.5 * DN_BETA),
        "b_down": nrm(ks[15], (L, N_EXPERTS, D_MODEL), 0.01),
        "w_ple_proj": nrm(ks[16], (L, D_PLE, D_MODEL), D_PLE ** -0.5 * DN_BETA),
        "w_ple_gate": nrm(ks[17], (L, D_MODEL, D_MODEL), D_MODEL ** -0.5),
        "ln2_g": 1.0 + nrm(ks[18], (L, D_MODEL), 0.02),
        "ln2_b": nrm(ks[19], (L, D_MODEL), 0.02),
    }


def reference(x, p, w_in, conv_w, w_br_conv, w_br_attn, attn_sinks, w_out, ln1_g, ln1_b,
              w_router, b_router, w_gu, b_gu, w_down, b_down, w_ple_proj, w_ple_gate,
              ln2_g, ln2_b):
    B, S, D = x.shape
    for i in range(DEPTH):
        proj = x @ w_in[i]
        b_gate, c_gate, h, q, k, v, g_conv, g_attn = jnp.split(proj, IN_SPLITS, axis=-1)
        y_conv = short_conv_mixer(b_gate, c_gate, h, conv_w[i]) @ w_br_conv[i]
        y_attn = sliding_window_attention(q, k, v, attn_sinks[i]) @ w_br_attn[i]
        merged = jax.nn.sigmoid(g_conv) * y_conv + jax.nn.sigmoid(g_attn) * y_attn
        x1 = layer_norm(DN_ALPHA * x + merged @ w_out[i], ln1_g[i], ln1_b[i])
        moe_out = routed_experts(x1, w_router[i], b_router[i], w_gu[i], b_gu[i], w_down[i], b_down[i])
        ple = jax.nn.sigmoid(x1 @ w_ple_gate[i]) * (p[i] @ w_ple_proj[i])
        x = layer_norm(DN_ALPHA * x1 + moe_out + ple, ln2_g[i], ln2_b[i])
    return x
```

```python
import functools
import math

import jax
import jax.numpy as jnp
from jax import lax
from jax.experimental import pallas as pl
from jax.experimental.pallas import tpu as pltpu

F32 = jnp.float32
BF16 = jnp.bfloat16
I32 = jnp.int32

SUBLANES = 8
LANES = 128
VMEM_LIMIT_BYTES = 56 * 1024 * 1024

D_MODEL = 1024
HEAD_DIM = 64
N_Q_HEADS = 16
N_KV_HEADS = 4
GQA_GROUP = N_Q_HEADS // N_KV_HEADS
KV_DIM = N_KV_HEADS * HEAD_DIM
WINDOW = 128
N_EXPERTS = 32
TOP_K = 4
D_EXPERT = D_MODEL
SWIGLU_LIMIT = 7.0
SWIGLU_ALPHA = 1.702
D_PLE = 256
DEPTH = 1
DN_ALPHA = (2.0 * DEPTH) ** 0.25
LN_EPS = 1e-5
ROW_TILES = D_MODEL // LANES

OFF_B, OFF_C, OFF_H, OFF_Q, OFF_GC, OFF_GA = (i * D_MODEL for i in range(6))
OFF_K = 6 * D_MODEL
OFF_V = OFF_K + KV_DIM
IN_TOTAL = OFF_V + KV_DIM

TS_MIX = 256
TS_ROUTE = 512
TN_DEST = 2048
TD_DISP = 1024
CHUNK = 256
TS_FIN = 256


def _layer_norm(z, g, b):
    mu = jnp.mean(z, axis=-1, keepdims=True)
    zc = z - mu
    var = jnp.mean(zc * zc, axis=-1, keepdims=True)
    return zc * lax.rsqrt(var + LN_EPS) * g + b


def _dot(a, b):
    return jnp.dot(a, b, preferred_element_type=F32)


def _dot_nt(a, b):
    return lax.dot_general(a, b, (((1,), (1,)), ((), ())), preferred_element_type=F32)


def _mix_kernel(sinks_ref, x_ref, win_ref, rep_ref, convw_ref, wbrc_ref, wbra_ref, wout_ref,
                g1_ref, b1_ref, x1_ref, x1t_ref, kext, vext, uext):
    ts = TS_MIX
    j = pl.program_id(1)

    @pl.when(j == 0)
    def _():
        kext[0:WINDOW, :] = jnp.zeros((WINDOW, D_MODEL), BF16)
        vext[0:WINDOW, :] = jnp.zeros((WINDOW, D_MODEL), BF16)
        uext[0:SUBLANES, :] = jnp.zeros((SUBLANES, D_MODEL), F32)

    x = x_ref[0]
    xb = x.astype(BF16)

    def proj(off, width):
        return _dot(xb, win_ref[:, off:off + width])

    u = proj(OFF_C, D_MODEL) * proj(OFF_H, D_MODEL)
    uext[SUBLANES:SUBLANES + ts, :] = u
    y = (convw_ref[2:3, :] * u
         + convw_ref[1:2, :] * uext[SUBLANES - 1:SUBLANES - 1 + ts, :]
         + convw_ref[0:1, :] * uext[SUBLANES - 2:SUBLANES - 2 + ts, :])
    uext[0:SUBLANES, :] = u[ts - SUBLANES:ts, :]
    yc_in = proj(OFF_B, D_MODEL) * y
    y_conv = _dot(yc_in.astype(BF16), wbrc_ref[...])
    acc = jax.nn.sigmoid(proj(OFF_GC, D_MODEL)) * y_conv

    q = proj(OFF_Q, D_MODEL).astype(BF16)
    kb = proj(OFF_K, KV_DIM).astype(BF16)
    vb = proj(OFF_V, KV_DIM).astype(BF16)
    kext[WINDOW:WINDOW + ts, :] = _dot(kb, rep_ref[...]).astype(BF16)
    vext[WINDOW:WINDOW + ts, :] = _dot(vb, rep_ref[...]).astype(BF16)

    grp = GQA_GROUP * HEAD_DIM
    row = lax.broadcasted_iota(I32, (WINDOW, 2 * WINDOW), 0)
    col = lax.broadcasted_iota(I32, (WINDOW, 2 * WINDOW), 1)
    band = (col > row) & (col <= row + WINDOW)
    lane_grp = lax.broadcasted_iota(I32, (2 * WINDOW, grp), 1) // HEAD_DIM
    q_lane_grp = lax.broadcasted_iota(I32, (WINDOW, grp), 1) // HEAD_DIM
    scale = 1.0 / math.sqrt(HEAD_DIM)
    o_blocks = []
    for i in range(ts // WINDOW):
        if i == 0:
            mask = band & ((col >= WINDOW) | (j > 0))
        else:
            mask = band
        o_heads = []
        for h in range(N_KV_HEADS):
            qh = q[i * WINDOW:(i + 1) * WINDOW, h * grp:(h + 1) * grp]
            kh = kext[i * WINDOW:i * WINDOW + 2 * WINDOW, h * grp:(h + 1) * grp]
            vh = vext[i * WINDOW:i * WINDOW + 2 * WINDOW, h * grp:(h + 1) * grp]
            oh = jnp.zeros((WINDOW, grp), F32)
            for g in range(GQA_GROUP):
                qm = jnp.where(q_lane_grp == g, qh, jnp.zeros_like(qh))
                s = _dot_nt(qm, kh) * scale
                s = jnp.where(mask, s, -jnp.inf)
                sink = sinks_ref[h * GQA_GROUP + g]
                m = jnp.maximum(jnp.max(s, axis=-1, keepdims=True), sink)
                e = jnp.exp(s - m)
                den = jnp.sum(e, axis=-1, keepdims=True) + jnp.exp(sink - m)
                p = (e / den).astype(BF16)
                vm = jnp.where(lane_grp == g, vh, jnp.zeros_like(vh))
                oh = oh + _dot(p, vm)
            o_heads.append(oh)
        o_blocks.append(jnp.concatenate(o_heads, axis=-1))
    o = jnp.concatenate(o_blocks, axis=0)
    kext[0:WINDOW, :] = kext[ts:ts + WINDOW, :]
    vext[0:WINDOW, :] = vext[ts:ts + WINDOW, :]

    y_attn = _dot(o.astype(BF16), wbra_ref[...])
    acc = acc + jax.nn.sigmoid(proj(OFF_GA, D_MODEL)) * y_attn

    z = DN_ALPHA * x + _dot(acc.astype(BF16), wout_ref[...])
    x1 = _layer_norm(z, g1_ref[...], b1_ref[...])
    x1_ref[0] = x1
    for s in range(ROW_TILES):
        x1t_ref[pl.ds(s, ts, stride=ROW_TILES), :] = x1[:, s * LANES:(s + 1) * LANES]


def _mix(x, w_in_p, rep, conv_w, wbrc, wbra, sinks, wout, g1, b1):
    bsz, seq, d = x.shape
    n = bsz * seq
    nj = seq // TS_MIX
    const = lambda shape: pl.BlockSpec(shape, lambda b, j, s: (0,) * len(shape))
    grid_spec = pltpu.PrefetchScalarGridSpec(
        num_scalar_prefetch=1,
        grid=(bsz, nj),
        in_specs=[
            pl.BlockSpec((1, TS_MIX, d), lambda b, j, s: (b, j, 0)),
            const((d, IN_TOTAL)),
            const((KV_DIM, d)),
            const((3, d)),
            const((d, d)),
            const((d, d)),
            const((d, d)),
            const((1, d)),
            const((1, d)),
        ],
        out_specs=[
            pl.BlockSpec((1, TS_MIX, d), lambda b, j, s: (b, j, 0)),
            pl.BlockSpec((TS_MIX * ROW_TILES, LANES), lambda b, j, s: (b * nj + j, 0)),
        ],
        scratch_shapes=[
            pltpu.VMEM((WINDOW + TS_MIX, d), BF16),
            pltpu.VMEM((WINDOW + TS_MIX, d), BF16),
            pltpu.VMEM((SUBLANES + TS_MIX, d), F32),
        ],
    )
    return pl.pallas_call(
        _mix_kernel,
        grid_spec=grid_spec,
        out_shape=[jax.ShapeDtypeStruct((bsz, seq, d), F32),
                   jax.ShapeDtypeStruct((n * ROW_TILES, LANES), F32)],
        compiler_params=pltpu.CompilerParams(
            dimension_semantics=("arbitrary", "arbitrary"),
            vmem_limit_bytes=VMEM_LIMIT_BYTES),
        name="mix",
    )(sinks, x, w_in_p, rep, conv_w, wbrc, wbra, wout, g1, b1)


def _route_kernel(x1_ref, whi_ref, wlo_ref, br_ref, idx_ref, w_ref, rank_ref, cnt_ref, carry):
    ts = TS_ROUTE
    i = pl.program_id(0)

    @pl.when(i == 0)
    def _():
        carry[...] = jnp.zeros_like(carry)

    x = x1_ref[...]
    xh = x.astype(BF16)
    xl = (x - xh.astype(F32)).astype(BF16)
    whi = whi_ref[...]
    logits = _dot_nt(whi, xh) + _dot_nt(whi, xl) + _dot_nt(wlo_ref[...], xh) + br_ref[...]

    eid = lax.broadcasted_iota(I32, (N_EXPERTS, ts), 0)
    rest = logits
    sels, vals = [], []
    for k in range(TOP_K):
        m = jnp.max(rest, axis=0, keepdims=True)
        idx = jnp.min(jnp.where(rest == m, eid, N_EXPERTS), axis=0, keepdims=True)
        sel = eid == idx
        rest = jnp.where(sel, -jnp.inf, rest)
        sels.append(sel)
        vals.append(m)
        idx_ref[k:k + 1, :] = idx
    exps = [jnp.exp(v - vals[0]) for v in vals]
    den = exps[0] + exps[1] + exps[2] + exps[3]
    for k in range(TOP_K):
        w_ref[k:k + 1, :] = exps[k] / den

    member = jnp.zeros((N_EXPERTS, ts), F32)
    for sel in sels:
        member = member + sel.astype(F32)
    r = lax.broadcasted_iota(I32, (ts, ts), 0)
    c = lax.broadcasted_iota(I32, (ts, ts), 1)
    upper = (r < c).astype(BF16)
    cum = _dot(member.astype(BF16), upper) + carry[:, 0:1]
    for k in range(TOP_K):
        rk = jnp.sum(jnp.where(sels[k], cum, 0.0), axis=0, keepdims=True)
        rank_ref[k:k + 1, :] = rk.astype(I32)
    carry[...] = carry[...] + jnp.sum(member, axis=1, keepdims=True)
    cnt_ref[...] = carry[...].astype(I32)


def _route(x1, whi, wlo, br):
    n, d = x1.shape
    const = lambda shape: pl.BlockSpec(shape, lambda i: (0,) * len(shape))
    tok = pl.BlockSpec((TOP_K, TS_ROUTE), lambda i: (0, i))
    return pl.pallas_call(
        _route_kernel,
        grid=(n // TS_ROUTE,),
        in_specs=[pl.BlockSpec((TS_ROUTE, d), lambda i: (i, 0)),
                  const((N_EXPERTS, d)), const((N_EXPERTS, d)), const((N_EXPERTS, 1))],
        out_specs=[tok, tok, tok, const((N_EXPERTS, LANES))],
        out_shape=[jax.ShapeDtypeStruct((TOP_K, n), I32),
                   jax.ShapeDtypeStruct((TOP_K, n), F32),
                   jax.ShapeDtypeStruct((TOP_K, n), I32),
                   jax.ShapeDtypeStruct((N_EXPERTS, LANES), I32)],
        scratch_shapes=[pltpu.VMEM((N_EXPERTS, LANES), F32)],
        compiler_params=pltpu.CompilerParams(dimension_semantics=("arbitrary",)),
        name="route",
    )(x1, whi, wlo, br)


def _dest_kernel(start_ref, idx_ref, rank_ref, dest_ref):
    idx = idx_ref[...]
    acc = rank_ref[...]
    for e in range(N_EXPERTS):
        acc = acc + jnp.where(idx == e, start_ref[e], 0)
    dest_ref[...] = acc


def _dest(start_padded, idx, rank):
    k, n = idx.shape
    tok = pl.BlockSpec((k, TN_DEST), lambda i, s: (0, i))
    return pl.pallas_call(
        _dest_kernel,
        grid_spec=pltpu.PrefetchScalarGridSpec(
            num_scalar_prefetch=1, grid=(n // TN_DEST,), in_specs=[tok, tok], out_specs=tok),
        out_shape=jax.ShapeDtypeStruct((k, n), I32),
        name="dest",
    )(start_padded, idx, rank)


def _row_copy(src, src_row, dst, dst_row, rows, sem):
    s0 = pl.multiple_of(src_row * ROW_TILES, ROW_TILES)
    d0 = pl.multiple_of(dst_row * ROW_TILES, ROW_TILES)
    return pltpu.make_async_copy(src.at[pl.ds(s0, rows * ROW_TILES)],
                                 dst.at[pl.ds(d0, rows * ROW_TILES)], sem)


def _pad_pieces():
    return [1 << b for b in reversed(range(CHUNK.bit_length() - 1))]


def _disp_kernel(dest_ref, padpos_ref, padlen_ref, nu_ref, x1t_ref, xs_ref, zbuf, sem, zsem):
    n = dest_ref.shape[0] // TOP_K
    half = CHUNK // 2
    n_half = xs_ref.shape[0] // (half * ROW_TILES)
    i = pl.program_id(0)
    t0 = i * TD_DISP

    def issue(t, carry):
        for k in range(TOP_K):
            _row_copy(x1t_ref, t0 + t, xs_ref, dest_ref[k * n + t0 + t], 1, sem).start()
        return carry

    lax.fori_loop(0, TD_DISP, issue, 0)

    @pl.when(i == 0)
    def _():
        zbuf[...] = jnp.zeros_like(zbuf)
        for wait in (False, True):
            for e in range(N_EXPERTS):
                npad = padlen_ref[e]
                for piece in _pad_pieces():
                    pos = padpos_ref[e] + (npad & ~(2 * piece - 1))
                    cp = _row_copy(zbuf, 0, xs_ref, pos, piece, zsem)

                    @pl.when((npad & piece) != 0)
                    def _():
                        cp.wait() if wait else cp.start()

            def tail(hc, carry):
                cp = _row_copy(zbuf, 0, xs_ref, hc * half, half, zsem)
                cp.wait() if wait else cp.start()
                return carry

            lax.fori_loop(2 * nu_ref[0], n_half, tail, 0)

    for k in range(TOP_K):
        _row_copy(x1t_ref, 0, xs_ref, 0, TD_DISP, sem).wait()


def _disp(dest_flat, padpos, padlen, n_used, x1t, p_rows):
    n = dest_flat.shape[0] // TOP_K
    anyspec = pl.BlockSpec(memory_space=pl.ANY)
    return pl.pallas_call(
        _disp_kernel,
        grid_spec=pltpu.PrefetchScalarGridSpec(
            num_scalar_prefetch=4, grid=(n // TD_DISP,),
            in_specs=[anyspec], out_specs=anyspec,
            scratch_shapes=[pltpu.VMEM((CHUNK // 2 * ROW_TILES, LANES), F32),
                            pltpu.SemaphoreType.DMA, pltpu.SemaphoreType.DMA]),
        out_shape=jax.ShapeDtypeStruct((p_rows * ROW_TILES, LANES), F32),
        compiler_params=pltpu.CompilerParams(dimension_semantics=("arbitrary",),
                                             has_side_effects=True),
        name="disp",
    )(dest_flat, padpos, padlen, n_used, x1t)


def _moe_kernel(ce_ref, nu_ref, xs_ref, wgu_ref, bgu_ref, wd_ref, bd_ref, ys_ref):
    c = pl.program_id(0)

    @pl.when(c < nu_ref[0])
    def _():
        x = jnp.concatenate(
            [xs_ref[pl.ds(s, CHUNK, stride=ROW_TILES), :] for s in range(ROW_TILES)], axis=-1)
        gu = _dot(x.astype(BF16), wgu_ref[0]) + bgu_ref[0]
        gate = jnp.minimum(gu[:, :D_EXPERT], SWIGLU_LIMIT)
        up = jnp.clip(gu[:, D_EXPERT:], -SWIGLU_LIMIT, SWIGLU_LIMIT)
        h = (up + 1.0) * gate * jax.nn.sigmoid(SWIGLU_ALPHA * gate)
        y = _dot(h.astype(BF16), wd_ref[0]) + bd_ref[0]
        for s in range(ROW_TILES):
            ys_ref[pl.ds(s, CHUNK, stride=ROW_TILES), :] = y[:, s * LANES:(s + 1) * LANES]

    @pl.when(c >= nu_ref[0])
    def _():
        ys_ref[...] = jnp.zeros_like(ys_ref)


def _moe(chunk_e, n_used, xs, wgu, bgu, wd, bd):
    n_chunks = xs.shape[0] // (CHUNK * ROW_TILES)
    d = D_MODEL

    def cc(c, ce, nu):
        return jnp.minimum(c, nu[0] - 1)

    rows = pl.BlockSpec((CHUNK * ROW_TILES, LANES), lambda c, ce, nu: (cc(c, ce, nu), 0))
    wspec = lambda shape: pl.BlockSpec(shape, lambda c, ce, nu: (ce[cc(c, ce, nu)], 0, 0))
    return pl.pallas_call(
        _moe_kernel,
        grid_spec=pltpu.PrefetchScalarGridSpec(
            num_scalar_prefetch=2, grid=(n_chunks,),
            in_specs=[rows, wspec((1, d, 2 * D_EXPERT)), wspec((1, 1, 2 * D_EXPERT)),
                      wspec((1, D_EXPERT, d)), wspec((1, 1, d))],
            out_specs=pl.BlockSpec((CHUNK * ROW_TILES, LANES), lambda c, ce, nu: (c, 0))),
        out_shape=jax.ShapeDtypeStruct(xs.shape, F32),
        compiler_params=pltpu.CompilerParams(dimension_semantics=("arbitrary",),
                                             vmem_limit_bytes=VMEM_LIMIT_BYTES),
        name="moe",
    )(chunk_e, n_used, xs, wgu, bgu, wd, bd)


def _fin_kernel(dest_ref, ys_ref, x1_ref, w_ref, p_ref, wpg_ref, wpp_ref, g2_ref, b2_ref,
                out_ref, rows, sem):
    ts = TS_FIN
    n = dest_ref.shape[0] // TOP_K
    t0 = pl.program_id(0) * ts

    def issue(t, carry):
        for k in range(TOP_K):
            _row_copy(ys_ref, dest_ref[k * n + t0 + t], rows.at[k], t, 1, sem).start()
        return carry

    lax.fori_loop(0, ts, issue, 0)

    x1 = x1_ref[...]
    ple = jax.nn.sigmoid(_dot(x1.astype(BF16), wpg_ref[...])) * _dot(p_ref[...].astype(BF16), wpp_ref[...])
    z = DN_ALPHA * x1 + ple

    for k in range(TOP_K):
        _row_copy(ys_ref, 0, rows.at[k], 0, ts, sem).wait()
    w = w_ref[...]
    for k in range(TOP_K):
        yk = jnp.concatenate(
            [rows[k, pl.ds(s, ts, stride=ROW_TILES), :] for s in range(ROW_TILES)], axis=-1)
        z = z + w[:, k:k + 1] * yk
    out_ref[...] = _layer_norm(z, g2_ref[...], b2_ref[...])


def _fin(dest_flat, ys, x1, w_rows, p2, wpg, wpp, g2, b2):
    n, d = x1.shape
    const = lambda shape: pl.BlockSpec(shape, lambda i, s: (0,) * len(shape))
    tile = lambda width: pl.BlockSpec((TS_FIN, width), lambda i, s: (i, 0))
    return pl.pallas_call(
        _fin_kernel,
        grid_spec=pltpu.PrefetchScalarGridSpec(
            num_scalar_prefetch=1, grid=(n // TS_FIN,),
            in_specs=[pl.BlockSpec(memory_space=pl.ANY), tile(d), tile(TOP_K), tile(D_PLE),
                      const((d, d)), const((D_PLE, d)), const((1, d)), const((1, d))],
            out_specs=tile(d),
            scratch_shapes=[pltpu.VMEM((TOP_K, TS_FIN * ROW_TILES, LANES), F32),
                            pltpu.SemaphoreType.DMA]),
        out_shape=jax.ShapeDtypeStruct((n, d), F32),
        compiler_params=pltpu.CompilerParams(dimension_semantics=("arbitrary",),
                                             vmem_limit_bytes=VMEM_LIMIT_BYTES),
        name="fin",
    )(dest_flat, ys, x1, w_rows, p2, wpg, wpp, g2, b2)


def _permute_w_in(w_in):
    d = D_MODEL
    b_, c_, h_, q_ = (w_in[:, i * d:(i + 1) * d] for i in range(4))
    k_ = w_in[:, 4 * d:4 * d + KV_DIM]
    v_ = w_in[:, 4 * d + KV_DIM:4 * d + 2 * KV_DIM]
    gc_ = w_in[:, 4 * d + 2 * KV_DIM:5 * d + 2 * KV_DIM]
    ga_ = w_in[:, 5 * d + 2 * KV_DIM:]
    return jnp.concatenate([b_, c_, h_, q_, gc_, ga_, k_, v_], axis=1).astype(BF16)


def _replication_matrix():
    src = jnp.arange(KV_DIM)[:, None]
    dst = jnp.arange(D_MODEL)[None, :]
    same_head = (dst // (GQA_GROUP * HEAD_DIM)) == (src // HEAD_DIM)
    same_dim = (dst % HEAD_DIM) == (src % HEAD_DIM)
    return (same_head & same_dim).astype(BF16)


def kernel(x, p, w_in, conv_w, w_br_conv, w_br_attn, attn_sinks, w_out, ln1_g, ln1_b, w_router,
           b_router, w_gu, b_gu, w_down, b_down, w_ple_proj, w_ple_gate, ln2_g, ln2_b):
    bsz, seq, d = x.shape
    n = bsz * seq
    for i in range(DEPTH):
        x1, x1t = _mix(x, _permute_w_in(w_in[i]), _replication_matrix(), conv_w[i],
                       w_br_conv[i].astype(BF16), w_br_attn[i].astype(BF16),
                       attn_sinks[i], w_out[i].astype(BF16),
                       ln1_g[i][None, :], ln1_b[i][None, :])
        x1 = x1.reshape(n, d)

        wr_t = w_router[i].T
        wr_hi = wr_t.astype(BF16)
        wr_lo = (wr_t - wr_hi.astype(F32)).astype(BF16)
        idx, w_top, rank, cnt = _route(x1, wr_hi, wr_lo, b_router[i][:, None])

        counts = cnt[:, 0]
        padded = ((counts + CHUNK - 1) // CHUNK) * CHUNK
        end_padded = jnp.cumsum(padded)
        start_padded = end_padded - padded
        p_rows = n * TOP_K + N_EXPERTS * CHUNK
        n_chunks = p_rows // CHUNK
        chunk_start = jnp.arange(n_chunks, dtype=I32) * CHUNK
        chunk_e = jnp.clip(jnp.searchsorted(end_padded, chunk_start, side='right'),
                           0, N_EXPERTS - 1).astype(I32)
        n_used = (end_padded[-1:] // CHUNK).astype(I32)

        dest = _dest(start_padded.astype(I32), idx, rank).reshape(TOP_K * n)
        xs = _disp(dest, (start_padded + counts).astype(I32), (padded - counts).astype(I32),
                   n_used, x1t, p_rows)
        ys = _moe(chunk_e, n_used, xs, w_gu[i].astype(BF16), b_gu[i][:, None, :],
                  w_down[i].astype(BF16), b_down[i][:, None, :])
        out = _fin(dest, ys, x1, w_top.T, p[i].reshape(n, D_PLE),
                   w_ple_gate[i].astype(BF16), w_ple_proj[i].astype(BF16),
                   ln2_g[i][None, :], ln2_b[i][None, :])
        x = out.reshape(bsz, seq, d)
    return x
```

```python
import functools
import math

import jax
import jax.numpy as jnp
from jax import lax
from jax.experimental import pallas as pl
from jax.experimental.pallas import tpu as pltpu

F32 = jnp.float32
BF16 = jnp.bfloat16
I32 = jnp.int32

SUBLANES = 8
LANES = 128
VMEM_LIMIT_BYTES = 56 * 1024 * 1024

D_MODEL = 1024
HEAD_DIM = 64
N_Q_HEADS = 16
N_KV_HEADS = 4
GQA_GROUP = N_Q_HEADS // N_KV_HEADS
KV_DIM = N_KV_HEADS * HEAD_DIM
WINDOW = 128
N_EXPERTS = 32
TOP_K = 4
D_EXPERT = D_MODEL
SWIGLU_LIMIT = 7.0
SWIGLU_ALPHA = 1.702
D_PLE = 256
DEPTH = 1
DN_ALPHA = (2.0 * DEPTH) ** 0.25
LN_EPS = 1e-5
ROW_TILES = D_MODEL // LANES

OFF_B, OFF_C, OFF_H, OFF_Q, OFF_GC, OFF_GA = (i * D_MODEL for i in range(6))
OFF_K = 6 * D_MODEL
OFF_V = OFF_K + KV_DIM
IN_TOTAL = OFF_V + KV_DIM

TS_MIX = 256
TS_ROUTE = 512
TN_DEST = 2048
TD_DISP = 1024
CHUNK = 256
TS_FIN = 256
ISSUE_UNROLL = 4


def _layer_norm(z, g, b):
    mu = jnp.mean(z, axis=-1, keepdims=True)
    zc = z - mu
    var = jnp.mean(zc * zc, axis=-1, keepdims=True)
    return zc * lax.rsqrt(var + LN_EPS) * g + b


def _dot(a, b):
    return jnp.dot(a, b, preferred_element_type=F32)


def _dot_nt(a, b):
    return lax.dot_general(a, b, (((1,), (1,)), ((), ())), preferred_element_type=F32)


def _mix_kernel(sinks_ref, x_ref, win_ref, rep_ref, convw_ref, wbrc_ref, wbra_ref, wout_ref,
                g1_ref, b1_ref, x1_ref, x1t_ref, kext, vext, uext):
    ts = TS_MIX
    j = pl.program_id(1)

    @pl.when(j == 0)
    def _():
        kext[0:WINDOW, :] = jnp.zeros((WINDOW, D_MODEL), BF16)
        vext[0:WINDOW, :] = jnp.zeros((WINDOW, D_MODEL), BF16)
        uext[0:SUBLANES, :] = jnp.zeros((SUBLANES, D_MODEL), F32)

    x = x_ref[0]
    xb = x.astype(BF16)

    def proj(off, width):
        return _dot(xb, win_ref[:, off:off + width])

    u = proj(OFF_C, D_MODEL) * proj(OFF_H, D_MODEL)
    uext[SUBLANES:SUBLANES + ts, :] = u
    y = (convw_ref[2:3, :] * u
         + convw_ref[1:2, :] * uext[SUBLANES - 1:SUBLANES - 1 + ts, :]
         + convw_ref[0:1, :] * uext[SUBLANES - 2:SUBLANES - 2 + ts, :])
    uext[0:SUBLANES, :] = u[ts - SUBLANES:ts, :]
    yc_in = proj(OFF_B, D_MODEL) * y
    y_conv = _dot(yc_in.astype(BF16), wbrc_ref[...])
    acc = jax.nn.sigmoid(proj(OFF_GC, D_MODEL)) * y_conv

    q = proj(OFF_Q, D_MODEL).astype(BF16)
    kb = proj(OFF_K, KV_DIM).astype(BF16)
    vb = proj(OFF_V, KV_DIM).astype(BF16)
    kext[WINDOW:WINDOW + ts, :] = _dot(kb, rep_ref[...]).astype(BF16)
    vext[WINDOW:WINDOW + ts, :] = _dot(vb, rep_ref[...]).astype(BF16)

    grp = GQA_GROUP * HEAD_DIM
    row = lax.broadcasted_iota(I32, (WINDOW, 2 * WINDOW), 0)
    col = lax.broadcasted_iota(I32, (WINDOW, 2 * WINDOW), 1)
    band = (col > row) & (col <= row + WINDOW)
    lane_grp = lax.broadcasted_iota(I32, (2 * WINDOW, grp), 1) // HEAD_DIM
    q_lane_grp = lax.broadcasted_iota(I32, (WINDOW, grp), 1) // HEAD_DIM
    scale = 1.0 / math.sqrt(HEAD_DIM)
    o_blocks = []
    for i in range(ts // WINDOW):
        if i == 0:
            mask = band & ((col >= WINDOW) | (j > 0))
        else:
            mask = band
        o_heads = []
        for h in range(N_KV_HEADS):
            qh = q[i * WINDOW:(i + 1) * WINDOW, h * grp:(h + 1) * grp]
            kh = kext[i * WINDOW:i * WINDOW + 2 * WINDOW, h * grp:(h + 1) * grp]
            vh = vext[i * WINDOW:i * WINDOW + 2 * WINDOW, h * grp:(h + 1) * grp]
            oh = jnp.zeros((WINDOW, grp), F32)
            for g in range(GQA_GROUP):
                qm = jnp.where(q_lane_grp == g, qh, jnp.zeros_like(qh))
                s = _dot_nt(qm, kh) * scale
                s = jnp.where(mask, s, -jnp.inf)
                sink = sinks_ref[h * GQA_GROUP + g]
                m = jnp.maximum(jnp.max(s, axis=-1, keepdims=True), sink)
                e = jnp.exp(s - m)
                den = jnp.sum(e, axis=-1, keepdims=True) + jnp.exp(sink - m)
                p = (e / den).astype(BF16)
                vm = jnp.where(lane_grp == g, vh, jnp.zeros_like(vh))
                oh = oh + _dot(p, vm)
            o_heads.append(oh)
        o_blocks.append(jnp.concatenate(o_heads, axis=-1))
    o = jnp.concatenate(o_blocks, axis=0)
    kext[0:WINDOW, :] = kext[ts:ts + WINDOW, :]
    vext[0:WINDOW, :] = vext[ts:ts + WINDOW, :]

    y_attn = _dot(o.astype(BF16), wbra_ref[...])
    acc = acc + jax.nn.sigmoid(proj(OFF_GA, D_MODEL)) * y_attn

    z = DN_ALPHA * x + _dot(acc.astype(BF16), wout_ref[...])
    x1 = _layer_norm(z, g1_ref[...], b1_ref[...])
    x1_ref[0] = x1
    for s in range(ROW_TILES):
        x1t_ref[pl.ds(s, ts, stride=ROW_TILES), :] = x1[:, s * LANES:(s + 1) * LANES]


def _mix(x, w_in_p, rep, conv_w, wbrc, wbra, sinks, wout, g1, b1):
    bsz, seq, d = x.shape
    n = bsz * seq
    nj = seq // TS_MIX
    const = lambda shape: pl.BlockSpec(shape, lambda b, j, s: (0,) * len(shape))
    grid_spec = pltpu.PrefetchScalarGridSpec(
        num_scalar_prefetch=1,
        grid=(bsz, nj),
        in_specs=[
            pl.BlockSpec((1, TS_MIX, d), lambda b, j, s: (b, j, 0)),
            const((d, IN_TOTAL)),
            const((KV_DIM, d)),
            const((3, d)),
            const((d, d)),
            const((d, d)),
            const((d, d)),
            const((1, d)),
            const((1, d)),
        ],
        out_specs=[
            pl.BlockSpec((1, TS_MIX, d), lambda b, j, s: (b, j, 0)),
            pl.BlockSpec((TS_MIX * ROW_TILES, LANES), lambda b, j, s: (b * nj + j, 0)),
        ],
        scratch_shapes=[
            pltpu.VMEM((WINDOW + TS_MIX, d), BF16),
            pltpu.VMEM((WINDOW + TS_MIX, d), BF16),
            pltpu.VMEM((SUBLANES + TS_MIX, d), F32),
        ],
    )
    return pl.pallas_call(
        _mix_kernel,
        grid_spec=grid_spec,
        out_shape=[jax.ShapeDtypeStruct((bsz, seq, d), F32),
                   jax.ShapeDtypeStruct((n * ROW_TILES, LANES), F32)],
        compiler_params=pltpu.CompilerParams(
            dimension_semantics=("arbitrary", "arbitrary"),
            vmem_limit_bytes=VMEM_LIMIT_BYTES),
        name="mix",
    )(sinks, x, w_in_p, rep, conv_w, wbrc, wbra, wout, g1, b1)


def _route_kernel(x1_ref, whi_ref, wlo_ref, br_ref, idx_ref, w_ref, rank_ref, cnt_ref, carry):
    ts = TS_ROUTE
    i = pl.program_id(0)

    @pl.when(i == 0)
    def _():
        carry[...] = jnp.zeros_like(carry)

    x = x1_ref[...]
    xh = x.astype(BF16)
    xl = (x - xh.astype(F32)).astype(BF16)
    whi = whi_ref[...]
    logits = _dot_nt(whi, xh) + _dot_nt(whi, xl) + _dot_nt(wlo_ref[...], xh) + br_ref[...]

    eid = lax.broadcasted_iota(I32, (N_EXPERTS, ts), 0)
    rest = logits
    sels, vals = [], []
    for k in range(TOP_K):
        m = jnp.max(rest, axis=0, keepdims=True)
        idx = jnp.min(jnp.where(rest == m, eid, N_EXPERTS), axis=0, keepdims=True)
        sel = eid == idx
        rest = jnp.where(sel, -jnp.inf, rest)
        sels.append(sel)
        vals.append(m)
        idx_ref[k:k + 1, :] = idx
    exps = [jnp.exp(v - vals[0]) for v in vals]
    den = exps[0] + exps[1] + exps[2] + exps[3]
    for k in range(TOP_K):
        w_ref[k:k + 1, :] = exps[k] / den

    member = jnp.zeros((N_EXPERTS, ts), F32)
    for sel in sels:
        member = member + sel.astype(F32)
    r = lax.broadcasted_iota(I32, (ts, ts), 0)
    c = lax.broadcasted_iota(I32, (ts, ts), 1)
    upper = (r < c).astype(BF16)
    cum = _dot(member.astype(BF16), upper) + carry[:, 0:1]
    for k in range(TOP_K):
        rk = jnp.sum(jnp.where(sels[k], cum, 0.0), axis=0, keepdims=True)
        rank_ref[k:k + 1, :] = rk.astype(I32)
    carry[...] = carry[...] + jnp.sum(member, axis=1, keepdims=True)
    cnt_ref[...] = carry[...].astype(I32)


def _route(x1, whi, wlo, br):
    n, d = x1.shape
    const = lambda shape: pl.BlockSpec(shape, lambda i: (0,) * len(shape))
    tok = pl.BlockSpec((TOP_K, TS_ROUTE), lambda i: (0, i))
    return pl.pallas_call(
        _route_kernel,
        grid=(n // TS_ROUTE,),
        in_specs=[pl.BlockSpec((TS_ROUTE, d), lambda i: (i, 0)),
                  const((N_EXPERTS, d)), const((N_EXPERTS, d)), const((N_EXPERTS, 1))],
        out_specs=[tok, tok, tok, const((N_EXPERTS, LANES))],
        out_shape=[jax.ShapeDtypeStruct((TOP_K, n), I32),
                   jax.ShapeDtypeStruct((TOP_K, n), F32),
                   jax.ShapeDtypeStruct((TOP_K, n), I32),
                   jax.ShapeDtypeStruct((N_EXPERTS, LANES), I32)],
        scratch_shapes=[pltpu.VMEM((N_EXPERTS, LANES), F32)],
        compiler_params=pltpu.CompilerParams(dimension_semantics=("arbitrary",)),
        name="route",
    )(x1, whi, wlo, br)


def _dest_kernel(start_ref, idx_ref, rank_ref, dest_ref):
    idx = idx_ref[...]
    acc = rank_ref[...]
    for e in range(N_EXPERTS):
        acc = acc + jnp.where(idx == e, start_ref[e], 0)
    dest_ref[...] = acc


def _dest(start_padded, idx, rank):
    k, n = idx.shape
    tok = pl.BlockSpec((k, TN_DEST), lambda i, s: (0, i))
    return pl.pallas_call(
        _dest_kernel,
        grid_spec=pltpu.PrefetchScalarGridSpec(
            num_scalar_prefetch=1, grid=(n // TN_DEST,), in_specs=[tok, tok], out_specs=tok),
        out_shape=jax.ShapeDtypeStruct((k, n), I32),
        name="dest",
    )(start_padded, idx, rank)


def _row_copy(src, src_row, dst, dst_row, rows, sem):
    s0 = pl.multiple_of(src_row * ROW_TILES, ROW_TILES)
    d0 = pl.multiple_of(dst_row * ROW_TILES, ROW_TILES)
    return pltpu.make_async_copy(src.at[pl.ds(s0, rows * ROW_TILES)],
                                 dst.at[pl.ds(d0, rows * ROW_TILES)], sem)


def _pad_pieces():
    return [1 << b for b in reversed(range(CHUNK.bit_length() - 1))]


def _disp_kernel(dest_ref, padpos_ref, padlen_ref, nu_ref, x1t_ref, xs_ref, zbuf, sem, zsem):
    n = dest_ref.shape[0] // TOP_K
    half = CHUNK // 2
    n_half = xs_ref.shape[0] // (half * ROW_TILES)
    i = pl.program_id(0)
    t0 = i * TD_DISP

    def issue(tb, carry):
        for u in range(ISSUE_UNROLL):
            t = tb * ISSUE_UNROLL + u
            for k in range(TOP_K):
                _row_copy(x1t_ref, t, xs_ref, dest_ref[k * n + t0 + t], 1, sem).start()
        return carry

    lax.fori_loop(0, TD_DISP // ISSUE_UNROLL, issue, 0)

    @pl.when(i == 0)
    def _():
        zbuf[...] = jnp.zeros_like(zbuf)
        for wait in (False, True):
            for e in range(N_EXPERTS):
                npad = padlen_ref[e]
                for piece in _pad_pieces():
                    pos = padpos_ref[e] + (npad & ~(2 * piece - 1))
                    cp = _row_copy(zbuf, 0, xs_ref, pos, piece, zsem)

                    @pl.when((npad & piece) != 0)
                    def _():
                        cp.wait() if wait else cp.start()

            def tail(hc, carry):
                cp = _row_copy(zbuf, 0, xs_ref, hc * half, half, zsem)
                cp.wait() if wait else cp.start()
                return carry

            lax.fori_loop(2 * nu_ref[0], n_half, tail, 0)

    for k in range(TOP_K):
        _row_copy(x1t_ref, 0, xs_ref, 0, TD_DISP, sem).wait()


def _disp(dest_flat, padpos, padlen, n_used, x1t, p_rows):
    n = dest_flat.shape[0] // TOP_K
    anyspec = pl.BlockSpec(memory_space=pl.ANY)
    return pl.pallas_call(
        _disp_kernel,
        grid_spec=pltpu.PrefetchScalarGridSpec(
            num_scalar_prefetch=4, grid=(n // TD_DISP,),
            in_specs=[pl.BlockSpec((TD_DISP * ROW_TILES, LANES), lambda i, *_: (i, 0))],
            out_specs=anyspec,
            scratch_shapes=[pltpu.VMEM((CHUNK // 2 * ROW_TILES, LANES), F32),
                            pltpu.SemaphoreType.DMA, pltpu.SemaphoreType.DMA]),
        out_shape=jax.ShapeDtypeStruct((p_rows * ROW_TILES, LANES), F32),
        compiler_params=pltpu.CompilerParams(dimension_semantics=("arbitrary",),
                                             has_side_effects=True),
        name="disp",
    )(dest_flat, padpos, padlen, n_used, x1t)


def _moe_kernel(ce_ref, nu_ref, xs_ref, wgu_ref, bgu_ref, wd_ref, bd_ref, ys_ref):
    c = pl.program_id(0)

    @pl.when(c < nu_ref[0])
    def _():
        x = jnp.concatenate(
            [xs_ref[pl.ds(s, CHUNK, stride=ROW_TILES), :] for s in range(ROW_TILES)], axis=-1)
        gu = _dot(x.astype(BF16), wgu_ref[0]) + bgu_ref[0]
        gate = jnp.minimum(gu[:, :D_EXPERT], SWIGLU_LIMIT)
        up = jnp.clip(gu[:, D_EXPERT:], -SWIGLU_LIMIT, SWIGLU_LIMIT)
        h = (up + 1.0) * gate * jax.nn.sigmoid(SWIGLU_ALPHA * gate)
        y = _dot(h.astype(BF16), wd_ref[0]) + bd_ref[0]
        for s in range(ROW_TILES):
            ys_ref[pl.ds(s, CHUNK, stride=ROW_TILES), :] = y[:, s * LANES:(s + 1) * LANES]

    @pl.when(c >= nu_ref[0])
    def _():
        ys_ref[...] = jnp.zeros_like(ys_ref)


def _moe(chunk_e, n_used, xs, wgu, bgu, wd, bd):
    n_chunks = xs.shape[0] // (CHUNK * ROW_TILES)
    d = D_MODEL

    def cc(c, ce, nu):
        return jnp.minimum(c, nu[0] - 1)

    rows = pl.BlockSpec((CHUNK * ROW_TILES, LANES), lambda c, ce, nu: (cc(c, ce, nu), 0))
    wspec = lambda shape: pl.BlockSpec(shape, lambda c, ce, nu: (ce[cc(c, ce, nu)], 0, 0))
    return pl.pallas_call(
        _moe_kernel,
        grid_spec=pltpu.PrefetchScalarGridSpec(
            num_scalar_prefetch=2, grid=(n_chunks,),
            in_specs=[rows, wspec((1, d, 2 * D_EXPERT)), wspec((1, 1, 2 * D_EXPERT)),
                      wspec((1, D_EXPERT, d)), wspec((1, 1, d))],
            out_specs=pl.BlockSpec((CHUNK * ROW_TILES, LANES), lambda c, ce, nu: (c, 0))),
        out_shape=jax.ShapeDtypeStruct(xs.shape, F32),
        compiler_params=pltpu.CompilerParams(dimension_semantics=("arbitrary",),
                                             vmem_limit_bytes=VMEM_LIMIT_BYTES),
        name="moe",
    )(chunk_e, n_used, xs, wgu, bgu, wd, bd)


def _fin_kernel(dest_ref, ys_ref, x1_ref, w_ref, p_ref, wpg_ref, wpp_ref, g2_ref, b2_ref,
                out_ref, rows, sem):
    ts = TS_FIN
    n = dest_ref.shape[0] // TOP_K
    t0 = pl.program_id(0) * ts

    def issue(tb, carry):
        for u in range(ISSUE_UNROLL):
            t = tb * ISSUE_UNROLL + u
            for k in range(TOP_K):
                _row_copy(ys_ref, dest_ref[k * n + t0 + t], rows.at[k], t, 1, sem).start()
        return carry

    lax.fori_loop(0, ts // ISSUE_UNROLL, issue, 0)

    x1 = x1_ref[...]
    ple = jax.nn.sigmoid(_dot(x1.astype(BF16), wpg_ref[...])) * _dot(p_ref[...].astype(BF16), wpp_ref[...])
    z = DN_ALPHA * x1 + ple

    for k in range(TOP_K):
        _row_copy(ys_ref, 0, rows.at[k], 0, ts, sem).wait()
    w = w_ref[...]
    for k in range(TOP_K):
        yk = jnp.concatenate(
            [rows[k, pl.ds(s, ts, stride=ROW_TILES), :] for s in range(ROW_TILES)], axis=-1)
        z = z + w[:, k:k + 1] * yk
    out_ref[...] = _layer_norm(z, g2_ref[...], b2_ref[...])


def _fin(dest_flat, ys, x1, w_rows, p2, wpg, wpp, g2, b2):
    n, d = x1.shape
    const = lambda shape: pl.BlockSpec(shape, lambda i, s: (0,) * len(shape))
    tile = lambda width: pl.BlockSpec((TS_FIN, width), lambda i, s: (i, 0))
    return pl.pallas_call(
        _fin_kernel,
        grid_spec=pltpu.PrefetchScalarGridSpec(
            num_scalar_prefetch=1, grid=(n // TS_FIN,),
            in_specs=[pl.BlockSpec(memory_space=pl.ANY), tile(d), tile(TOP_K), tile(D_PLE),
                      const((d, d)), const((D_PLE, d)), const((1, d)), const((1, d))],
            out_specs=tile(d),
            scratch_shapes=[pltpu.VMEM((TOP_K, TS_FIN * ROW_TILES, LANES), F32),
                            pltpu.SemaphoreType.DMA]),
        out_shape=jax.ShapeDtypeStruct((n, d), F32),
        compiler_params=pltpu.CompilerParams(dimension_semantics=("arbitrary",),
                                             vmem_limit_bytes=VMEM_LIMIT_BYTES),
        name="fin",
    )(dest_flat, ys, x1, w_rows, p2, wpg, wpp, g2, b2)


def _permute_w_in(w_in):
    d = D_MODEL
    b_, c_, h_, q_ = (w_in[:, i * d:(i + 1) * d] for i in range(4))
    k_ = w_in[:, 4 * d:4 * d + KV_DIM]
    v_ = w_in[:, 4 * d + KV_DIM:4 * d + 2 * KV_DIM]
    gc_ = w_in[:, 4 * d + 2 * KV_DIM:5 * d + 2 * KV_DIM]
    ga_ = w_in[:, 5 * d + 2 * KV_DIM:]
    return jnp.concatenate([b_, c_, h_, q_, gc_, ga_, k_, v_], axis=1).astype(BF16)


def _replication_matrix():
    src = jnp.arange(KV_DIM)[:, None]
    dst = jnp.arange(D_MODEL)[None, :]
    same_head = (dst // (GQA_GROUP * HEAD_DIM)) == (src // HEAD_DIM)
    same_dim = (dst % HEAD_DIM) == (src % HEAD_DIM)
    return (same_head & same_dim).astype(BF16)


def kernel(x, p, w_in, conv_w, w_br_conv, w_br_attn, attn_sinks, w_out, ln1_g, ln1_b, w_router,
           b_router, w_gu, b_gu, w_down, b_down, w_ple_proj, w_ple_gate, ln2_g, ln2_b):
    bsz, seq, d = x.shape
    n = bsz * seq
    for i in range(DEPTH):
        x1, x1t = _mix(x, _permute_w_in(w_in[i]), _replication_matrix(), conv_w[i],
                       w_br_conv[i].astype(BF16), w_br_attn[i].astype(BF16),
                       attn_sinks[i], w_out[i].astype(BF16),
                       ln1_g[i][None, :], ln1_b[i][None, :])
        x1 = x1.reshape(n, d)

        wr_t = w_router[i].T
        wr_hi = wr_t.astype(BF16)
        wr_lo = (wr_t - wr_hi.astype(F32)).astype(BF16)
        idx, w_top, rank, cnt = _route(x1, wr_hi, wr_lo, b_router[i][:, None])

        counts = cnt[:, 0]
        padded = ((counts + CHUNK - 1) // CHUNK) * CHUNK
        end_padded = jnp.cumsum(padded)
        start_padded = end_padded - padded
        p_rows = n * TOP_K + N_EXPERTS * CHUNK
        n_chunks = p_rows // CHUNK
        chunk_start = jnp.arange(n_chunks, dtype=I32) * CHUNK
        chunk_e = jnp.minimum(
            jnp.sum((end_padded[None, :] <= chunk_start[:, None]).astype(I32), axis=1),
            N_EXPERTS - 1)
        n_used = (end_padded[-1:] // CHUNK).astype(I32)

        dest = _dest(start_padded.astype(I32), idx, rank).reshape(TOP_K * n)
        xs = _disp(dest, (start_padded + counts).astype(I32), (padded - counts).astype(I32),
                   n_used, x1t, p_rows)
        ys = _moe(chunk_e, n_used, xs, w_gu[i].astype(BF16), b_gu[i][:, None, :],
                  w_down[i].astype(BF16), b_down[i][:, None, :])
        out = _fin(dest, ys, x1, w_top.T, p[i].reshape(n, D_PLE),
                   w_ple_gate[i].astype(BF16), w_ple_proj[i].astype(BF16),
                   ln2_g[i][None, :], ln2_b[i][None, :])
        x = out.reshape(bsz, seq, d)
    return x
```

```python
import functools
import math

import jax
import jax.numpy as jnp
from jax import lax
from jax.experimental import pallas as pl
from jax.experimental.pallas import tpu as pltpu

F32 = jnp.float32
BF16 = jnp.bfloat16
I32 = jnp.int32

SUBLANES = 8
LANES = 128
VMEM_LIMIT_BYTES = 56 * 1024 * 1024

D_MODEL = 1024
HEAD_DIM = 64
N_Q_HEADS = 16
N_KV_HEADS = 4
GQA_GROUP = N_Q_HEADS // N_KV_HEADS
KV_DIM = N_KV_HEADS * HEAD_DIM
WINDOW = 128
N_EXPERTS = 32
TOP_K = 4
D_EXPERT = D_MODEL
SWIGLU_LIMIT = 7.0
SWIGLU_ALPHA = 1.702
D_PLE = 256
DEPTH = 1
DN_ALPHA = (2.0 * DEPTH) ** 0.25
LN_EPS = 1e-5
ROW_TILES = D_MODEL // LANES

OFF_B, OFF_C, OFF_H, OFF_Q, OFF_GC, OFF_GA = (i * D_MODEL for i in range(6))
OFF_K = 6 * D_MODEL
OFF_V = OFF_K + KV_DIM
IN_TOTAL = OFF_V + KV_DIM

TS_MIX = 512
TS_ROUTE = 512
TN_DEST = 2048
TD_DISP = 1024
CHUNK = 512
TS_FIN = 256
ISSUE_UNROLL = 4


def _layer_norm(z, g, b):
    mu = jnp.mean(z, axis=-1, keepdims=True)
    zc = z - mu
    var = jnp.mean(zc * zc, axis=-1, keepdims=True)
    return zc * lax.rsqrt(var + LN_EPS) * g + b


def _dot(a, b):
    return jnp.dot(a, b, preferred_element_type=F32)


def _dot_nt(a, b):
    return lax.dot_general(a, b, (((1,), (1,)), ((), ())), preferred_element_type=F32)


def _mix_kernel(sinks_ref, x_ref, win_ref, rep_ref, convw_ref, wbrc_ref, wbra_ref, wout_ref,
                g1_ref, b1_ref, x1_ref, x1t_ref, kext, vext, uext):
    ts = TS_MIX
    j = pl.program_id(1)

    @pl.when(j == 0)
    def _():
        kext[0:WINDOW, :] = jnp.zeros((WINDOW, D_MODEL), BF16)
        vext[0:WINDOW, :] = jnp.zeros((WINDOW, D_MODEL), BF16)
        uext[0:SUBLANES, :] = jnp.zeros((SUBLANES, D_MODEL), F32)

    x = x_ref[0]
    xb = x.astype(BF16)

    def proj(off, width):
        return _dot(xb, win_ref[:, off:off + width])

    u = proj(OFF_C, D_MODEL) * proj(OFF_H, D_MODEL)
    uext[SUBLANES:SUBLANES + ts, :] = u
    y = (convw_ref[2:3, :] * u
         + convw_ref[1:2, :] * uext[SUBLANES - 1:SUBLANES - 1 + ts, :]
         + convw_ref[0:1, :] * uext[SUBLANES - 2:SUBLANES - 2 + ts, :])
    uext[0:SUBLANES, :] = u[ts - SUBLANES:ts, :]
    yc_in = proj(OFF_B, D_MODEL) * y
    y_conv = _dot(yc_in.astype(BF16), wbrc_ref[...])
    acc = jax.nn.sigmoid(proj(OFF_GC, D_MODEL)) * y_conv

    q = proj(OFF_Q, D_MODEL).astype(BF16)
    kb = proj(OFF_K, KV_DIM).astype(BF16)
    vb = proj(OFF_V, KV_DIM).astype(BF16)
    kext[WINDOW:WINDOW + ts, :] = _dot(kb, rep_ref[...]).astype(BF16)
    vext[WINDOW:WINDOW + ts, :] = _dot(vb, rep_ref[...]).astype(BF16)

    grp = GQA_GROUP * HEAD_DIM
    row = lax.broadcasted_iota(I32, (WINDOW, 2 * WINDOW), 0)
    col = lax.broadcasted_iota(I32, (WINDOW, 2 * WINDOW), 1)
    band = (col > row) & (col <= row + WINDOW)
    q_lane_grp = lax.broadcasted_iota(I32, (WINDOW, grp), 1) // HEAD_DIM
    scale = 1.0 / math.sqrt(HEAD_DIM)
    o_blocks = []
    for i in range(ts // WINDOW):
        if i == 0:
            mask = band & ((col >= WINDOW) | (j > 0))
        else:
            mask = band
        o_heads = []
        for h in range(N_KV_HEADS):
            qh = q[i * WINDOW:(i + 1) * WINDOW, h * grp:(h + 1) * grp]
            kh = kext[i * WINDOW:i * WINDOW + 2 * WINDOW, h * grp:(h + 1) * grp]
            vh = vext[i * WINDOW:i * WINDOW + 2 * WINDOW, h * grp:(h + 1) * grp]
            qs = jnp.concatenate(
                [jnp.where(q_lane_grp == g, qh, jnp.zeros_like(qh)) for g in range(GQA_GROUP)], axis=0)
            s_all = _dot_nt(qs, kh) * scale
            ps = []
            for g in range(GQA_GROUP):
                s = jnp.where(mask, s_all[g * WINDOW:(g + 1) * WINDOW], -jnp.inf)
                sink = sinks_ref[h * GQA_GROUP + g]
                m = jnp.maximum(jnp.max(s, axis=-1, keepdims=True), sink)
                e = jnp.exp(s - m)
                den = jnp.sum(e, axis=-1, keepdims=True) + jnp.exp(sink - m)
                ps.append((e / den).astype(BF16))
            pv = _dot(jnp.concatenate(ps, axis=0), vh)
            oh = jnp.zeros((WINDOW, grp), F32)
            for g in range(GQA_GROUP):
                oh = jnp.where(q_lane_grp == g, pv[g * WINDOW:(g + 1) * WINDOW], oh)
            o_heads.append(oh)
        o_blocks.append(jnp.concatenate(o_heads, axis=-1))
    o = jnp.concatenate(o_blocks, axis=0)
    kext[0:WINDOW, :] = kext[ts:ts + WINDOW, :]
    vext[0:WINDOW, :] = vext[ts:ts + WINDOW, :]

    y_attn = _dot(o.astype(BF16), wbra_ref[...])
    acc = acc + jax.nn.sigmoid(proj(OFF_GA, D_MODEL)) * y_attn

    z = DN_ALPHA * x + _dot(acc.astype(BF16), wout_ref[...])
    x1 = _layer_norm(z, g1_ref[...], b1_ref[...])
    x1_ref[0] = x1
    for s in range(ROW_TILES):
        x1t_ref[pl.ds(s, ts, stride=ROW_TILES), :] = x1[:, s * LANES:(s + 1) * LANES]


def _mix(x, w_in_p, rep, conv_w, wbrc, wbra, sinks, wout, g1, b1):
    bsz, seq, d = x.shape
    n = bsz * seq
    nj = seq // TS_MIX
    const = lambda shape: pl.BlockSpec(shape, lambda b, j, s: (0,) * len(shape),
                                       pipeline_mode=pl.Buffered(1))
    grid_spec = pltpu.PrefetchScalarGridSpec(
        num_scalar_prefetch=1,
        grid=(bsz, nj),
        in_specs=[
            pl.BlockSpec((1, TS_MIX, d), lambda b, j, s: (b, j, 0)),
            const((d, IN_TOTAL)),
            const((KV_DIM, d)),
            const((3, d)),
            const((d, d)),
            const((d, d)),
            const((d, d)),
            const((1, d)),
            const((1, d)),
        ],
        out_specs=[
            pl.BlockSpec((1, TS_MIX, d), lambda b, j, s: (b, j, 0)),
            pl.BlockSpec((TS_MIX * ROW_TILES, LANES), lambda b, j, s: (b * nj + j, 0)),
        ],
        scratch_shapes=[
            pltpu.VMEM((WINDOW + TS_MIX, d), BF16),
            pltpu.VMEM((WINDOW + TS_MIX, d), BF16),
            pltpu.VMEM((SUBLANES + TS_MIX, d), F32),
        ],
    )
    return pl.pallas_call(
        _mix_kernel,
        grid_spec=grid_spec,
        out_shape=[jax.ShapeDtypeStruct((bsz, seq, d), F32),
                   jax.ShapeDtypeStruct((n * ROW_TILES, LANES), F32)],
        compiler_params=pltpu.CompilerParams(
            dimension_semantics=("arbitrary", "arbitrary"),
            vmem_limit_bytes=VMEM_LIMIT_BYTES),
        name="mix",
    )(sinks, x, w_in_p, rep, conv_w, wbrc, wbra, wout, g1, b1)


def _route_kernel(x1_ref, whi_ref, wlo_ref, br_ref, idx_ref, w_ref, rank_ref, cnt_ref, carry):
    ts = TS_ROUTE
    i = pl.program_id(0)

    @pl.when(i == 0)
    def _():
        carry[...] = jnp.zeros_like(carry)

    x = x1_ref[...]
    xh = x.astype(BF16)
    xl = (x - xh.astype(F32)).astype(BF16)
    whi = whi_ref[...]
    logits = _dot_nt(whi, xh) + _dot_nt(whi, xl) + _dot_nt(wlo_ref[...], xh) + br_ref[...]

    eid = lax.broadcasted_iota(I32, (N_EXPERTS, ts), 0)
    rest = logits
    sels, vals = [], []
    for k in range(TOP_K):
        m = jnp.max(rest, axis=0, keepdims=True)
        idx = jnp.min(jnp.where(rest == m, eid, N_EXPERTS), axis=0, keepdims=True)
        sel = eid == idx
        rest = jnp.where(sel, -jnp.inf, rest)
        sels.append(sel)
        vals.append(m)
        idx_ref[k:k + 1, :] = idx
    exps = [jnp.exp(v - vals[0]) for v in vals]
    den = exps[0] + exps[1] + exps[2] + exps[3]
    for k in range(TOP_K):
        w_ref[k:k + 1, :] = exps[k] / den

    member = jnp.zeros((N_EXPERTS, ts), F32)
    for sel in sels:
        member = member + sel.astype(F32)
    r = lax.broadcasted_iota(I32, (ts, ts), 0)
    c = lax.broadcasted_iota(I32, (ts, ts), 1)
    upper = (r < c).astype(BF16)
    cum = _dot(member.astype(BF16), upper) + carry[:, 0:1]
    for k in range(TOP_K):
        rk = jnp.sum(jnp.where(sels[k], cum, 0.0), axis=0, keepdims=True)
        rank_ref[k:k + 1, :] = rk.astype(I32)
    carry[...] = carry[...] + jnp.sum(member, axis=1, keepdims=True)
    cnt_ref[...] = carry[...].astype(I32)


def _route(x1, whi, wlo, br):
    n, d = x1.shape
    const = lambda shape: pl.BlockSpec(shape, lambda i: (0,) * len(shape))
    tok = pl.BlockSpec((TOP_K, TS_ROUTE), lambda i: (0, i))
    return pl.pallas_call(
        _route_kernel,
        grid=(n // TS_ROUTE,),
        in_specs=[pl.BlockSpec((TS_ROUTE, d), lambda i: (i, 0)),
                  const((N_EXPERTS, d)), const((N_EXPERTS, d)), const((N_EXPERTS, 1))],
        out_specs=[tok, tok, tok, const((N_EXPERTS, LANES))],
        out_shape=[jax.ShapeDtypeStruct((TOP_K, n), I32),
                   jax.ShapeDtypeStruct((TOP_K, n), F32),
                   jax.ShapeDtypeStruct((TOP_K, n), I32),
                   jax.ShapeDtypeStruct((N_EXPERTS, LANES), I32)],
        scratch_shapes=[pltpu.VMEM((N_EXPERTS, LANES), F32)],
        compiler_params=pltpu.CompilerParams(dimension_semantics=("arbitrary",)),
        name="route",
    )(x1, whi, wlo, br)


def _dest_kernel(start_ref, idx_ref, rank_ref, dest_ref):
    idx = idx_ref[...]
    acc = rank_ref[...]
    for e in range(N_EXPERTS):
        acc = acc + jnp.where(idx == e, start_ref[e], 0)
    dest_ref[...] = acc


def _dest(start_padded, idx, rank):
    k, n = idx.shape
    tok = pl.BlockSpec((k, TN_DEST), lambda i, s: (0, i))
    return pl.pallas_call(
        _dest_kernel,
        grid_spec=pltpu.PrefetchScalarGridSpec(
            num_scalar_prefetch=1, grid=(n // TN_DEST,), in_specs=[tok, tok], out_specs=tok),
        out_shape=jax.ShapeDtypeStruct((k, n), I32),
        name="dest",
    )(start_padded, idx, rank)


def _row_copy(src, src_row, dst, dst_row, rows, sem):
    s0 = pl.multiple_of(src_row * ROW_TILES, ROW_TILES)
    d0 = pl.multiple_of(dst_row * ROW_TILES, ROW_TILES)
    return pltpu.make_async_copy(src.at[pl.ds(s0, rows * ROW_TILES)],
                                 dst.at[pl.ds(d0, rows * ROW_TILES)], sem)


def _pad_pieces():
    return [1 << b for b in reversed(range(CHUNK.bit_length() - 1))]


def _disp_kernel(dest_ref, padpos_ref, padlen_ref, nu_ref, x1t_ref, xs_ref, zbuf, sem, zsem):
    n = dest_ref.shape[0] // TOP_K
    half = CHUNK // 2
    n_half = xs_ref.shape[0] // (half * ROW_TILES)
    i = pl.program_id(0)
    t0 = i * TD_DISP

    def issue(tb, carry):
        for u in range(ISSUE_UNROLL):
            t = tb * ISSUE_UNROLL + u
            for k in range(TOP_K):
                _row_copy(x1t_ref, t, xs_ref, dest_ref[k * n + t0 + t], 1, sem).start(priority=k % 2)
        return carry

    lax.fori_loop(0, TD_DISP // ISSUE_UNROLL, issue, 0)

    @pl.when(i == 0)
    def _():
        zbuf[...] = jnp.zeros_like(zbuf)
        for wait in (False, True):
            for e in range(N_EXPERTS):
                npad = padlen_ref[e]
                for piece in _pad_pieces():
                    pos = padpos_ref[e] + (npad & ~(2 * piece - 1))
                    cp = _row_copy(zbuf, 0, xs_ref, pos, piece, zsem)

                    @pl.when((npad & piece) != 0)
                    def _():
                        cp.wait() if wait else cp.start()

            def tail(hc, carry):
                cp = _row_copy(zbuf, 0, xs_ref, hc * half, half, zsem)
                cp.wait() if wait else cp.start()
                return carry

            lax.fori_loop(2 * nu_ref[0], n_half, tail, 0)

    for k in range(TOP_K):
        _row_copy(x1t_ref, 0, xs_ref, 0, TD_DISP, sem).wait()


def _disp(dest_flat, padpos, padlen, n_used, x1t, p_rows):
    n = dest_flat.shape[0] // TOP_K
    anyspec = pl.BlockSpec(memory_space=pl.ANY)
    return pl.pallas_call(
        _disp_kernel,
        grid_spec=pltpu.PrefetchScalarGridSpec(
            num_scalar_prefetch=4, grid=(n // TD_DISP,),
            in_specs=[pl.BlockSpec((TD_DISP * ROW_TILES, LANES), lambda i, *_: (i, 0))],
            out_specs=anyspec,
            scratch_shapes=[pltpu.VMEM((CHUNK // 2 * ROW_TILES, LANES), F32),
                            pltpu.SemaphoreType.DMA, pltpu.SemaphoreType.DMA]),
        out_shape=jax.ShapeDtypeStruct((p_rows * ROW_TILES, LANES), F32),
        compiler_params=pltpu.CompilerParams(dimension_semantics=("arbitrary",),
                                             has_side_effects=True),
        name="disp",
    )(dest_flat, padpos, padlen, n_used, x1t)


def _moe_kernel(ce_ref, nu_ref, xs_ref, wgu_ref, bgu_ref, wd_ref, bd_ref, ys_ref, wgu_bf, wd_bf):
    c = pl.program_id(0)
    used = c < nu_ref[0]

    @pl.when(used & ((c == 0) | (ce_ref[c] != ce_ref[jnp.maximum(c - 1, 0)])))
    def _():
        wgu_bf[...] = wgu_ref[0].astype(BF16)
        wd_bf[...] = wd_ref[0].astype(BF16)

    @pl.when(used)
    def _():
        x = jnp.concatenate(
            [xs_ref[pl.ds(s, CHUNK, stride=ROW_TILES), :] for s in range(ROW_TILES)], axis=-1)
        gu = _dot(x.astype(BF16), wgu_bf[...]) + bgu_ref[0]
        gate = jnp.minimum(gu[:, :D_EXPERT], SWIGLU_LIMIT)
        up = jnp.clip(gu[:, D_EXPERT:], -SWIGLU_LIMIT, SWIGLU_LIMIT)
        h = (up + 1.0) * gate * jax.nn.sigmoid(SWIGLU_ALPHA * gate)
        y = _dot(h.astype(BF16), wd_bf[...]) + bd_ref[0]
        for s in range(ROW_TILES):
            ys_ref[pl.ds(s, CHUNK, stride=ROW_TILES), :] = y[:, s * LANES:(s + 1) * LANES]

    @pl.when(c >= nu_ref[0])
    def _():
        ys_ref[...] = jnp.zeros_like(ys_ref)


def _moe(chunk_e, n_used, xs, wgu, bgu, wd, bd):
    n_chunks = xs.shape[0] // (CHUNK * ROW_TILES)
    d = D_MODEL

    def cc(c, ce, nu):
        return jnp.minimum(c, nu[0] - 1)

    rows = pl.BlockSpec((CHUNK * ROW_TILES, LANES), lambda c, ce, nu: (cc(c, ce, nu), 0))
    wspec = lambda shape: pl.BlockSpec(shape, lambda c, ce, nu: (ce[cc(c, ce, nu)], 0, 0))
    return pl.pallas_call(
        _moe_kernel,
        grid_spec=pltpu.PrefetchScalarGridSpec(
            num_scalar_prefetch=2, grid=(n_chunks,),
            in_specs=[rows, wspec((1, d, 2 * D_EXPERT)), wspec((1, 1, 2 * D_EXPERT)),
                      wspec((1, D_EXPERT, d)), wspec((1, 1, d))],
            out_specs=pl.BlockSpec((CHUNK * ROW_TILES, LANES), lambda c, ce, nu: (c, 0)),
            scratch_shapes=[pltpu.VMEM((d, 2 * D_EXPERT), BF16), pltpu.VMEM((D_EXPERT, d), BF16)]),
        out_shape=jax.ShapeDtypeStruct(xs.shape, F32),
        compiler_params=pltpu.CompilerParams(dimension_semantics=("arbitrary",),
                                             vmem_limit_bytes=VMEM_LIMIT_BYTES),
        name="moe",
    )(chunk_e, n_used, xs, wgu, bgu, wd, bd)


def _fin_kernel(dest_ref, ys_ref, x1_ref, w_ref, p_ref, wpg_ref, wpp_ref, g2_ref, b2_ref,
                out_ref, rows, sems):
    ts = TS_FIN
    n = dest_ref.shape[0] // TOP_K
    i = pl.program_id(0)
    slot = i % 2

    def gather_tile(tile, into):
        def issue(tb, carry):
            for u in range(ISSUE_UNROLL):
                t = tb * ISSUE_UNROLL + u
                for k in range(TOP_K):
                    _row_copy(ys_ref, dest_ref[k * n + tile * ts + t], rows.at[into, k], t, 1,
                              sems.at[into]).start(priority=k % 2)
            return carry

        lax.fori_loop(0, ts // ISSUE_UNROLL, issue, 0)

    @pl.when(i == 0)
    def _():
        gather_tile(0, 0)

    @pl.when(i + 1 < pl.num_programs(0))
    def _():
        gather_tile(i + 1, 1 - slot)

    x1 = x1_ref[...]
    ple = jax.nn.sigmoid(_dot(x1.astype(BF16), wpg_ref[...])) * _dot(p_ref[...].astype(BF16), wpp_ref[...])
    z = DN_ALPHA * x1 + ple

    for k in range(TOP_K):
        _row_copy(ys_ref, 0, rows.at[slot, k], 0, ts, sems.at[slot]).wait()
    w = w_ref[...]
    for k in range(TOP_K):
        yk = jnp.concatenate(
            [rows[slot, k, pl.ds(s, ts, stride=ROW_TILES), :] for s in range(ROW_TILES)], axis=-1)
        z = z + w[:, k:k + 1] * yk
    out_ref[...] = _layer_norm(z, g2_ref[...], b2_ref[...])


def _fin(dest_flat, ys, x1, w_rows, p2, wpg, wpp, g2, b2):
    n, d = x1.shape
    const = lambda shape: pl.BlockSpec(shape, lambda i, s: (0,) * len(shape))
    tile = lambda width: pl.BlockSpec((TS_FIN, width), lambda i, s: (i, 0))
    return pl.pallas_call(
        _fin_kernel,
        grid_spec=pltpu.PrefetchScalarGridSpec(
            num_scalar_prefetch=1, grid=(n // TS_FIN,),
            in_specs=[pl.BlockSpec(memory_space=pl.ANY), tile(d), tile(TOP_K), tile(D_PLE),
                      const((d, d)), const((D_PLE, d)), const((1, d)), const((1, d))],
            out_specs=tile(d),
            scratch_shapes=[pltpu.VMEM((2, TOP_K, TS_FIN * ROW_TILES, LANES), F32),
                            pltpu.SemaphoreType.DMA((2,))]),
        out_shape=jax.ShapeDtypeStruct((n, d), F32),
        compiler_params=pltpu.CompilerParams(dimension_semantics=("arbitrary",),
                                             vmem_limit_bytes=VMEM_LIMIT_BYTES),
        name="fin",
    )(dest_flat, ys, x1, w_rows, p2, wpg, wpp, g2, b2)


def _permute_w_in(w_in):
    d = D_MODEL
    b_, c_, h_, q_ = (w_in[:, i * d:(i + 1) * d] for i in range(4))
    k_ = w_in[:, 4 * d:4 * d + KV_DIM]
    v_ = w_in[:, 4 * d + KV_DIM:4 * d + 2 * KV_DIM]
    gc_ = w_in[:, 4 * d + 2 * KV_DIM:5 * d + 2 * KV_DIM]
    ga_ = w_in[:, 5 * d + 2 * KV_DIM:]
    return jnp.concatenate([b_, c_, h_, q_, gc_, ga_, k_, v_], axis=1).astype(BF16)


def _replication_matrix():
    src = jnp.arange(KV_DIM)[:, None]
    dst = jnp.arange(D_MODEL)[None, :]
    same_head = (dst // (GQA_GROUP * HEAD_DIM)) == (src // HEAD_DIM)
    same_dim = (dst % HEAD_DIM) == (src % HEAD_DIM)
    return (same_head & same_dim).astype(BF16)


def kernel(x, p, w_in, conv_w, w_br_conv, w_br_attn, attn_sinks, w_out, ln1_g, ln1_b, w_router,
           b_router, w_gu, b_gu, w_down, b_down, w_ple_proj, w_ple_gate, ln2_g, ln2_b):
    bsz, seq, d = x.shape
    n = bsz * seq
    for i in range(DEPTH):
        x1, x1t = _mix(x, _permute_w_in(w_in[i]), _replication_matrix(), conv_w[i],
                       w_br_conv[i].astype(BF16), w_br_attn[i].astype(BF16),
                       attn_sinks[i], w_out[i].astype(BF16),
                       ln1_g[i][None, :], ln1_b[i][None, :])
        x1 = x1.reshape(n, d)

        wr_t = w_router[i].T
        wr_hi = wr_t.astype(BF16)
        wr_lo = (wr_t - wr_hi.astype(F32)).astype(BF16)
        idx, w_top, rank, cnt = _route(x1, wr_hi, wr_lo, b_router[i][:, None])

        counts = cnt[:, 0]
        padded = ((counts + CHUNK - 1) // CHUNK) * CHUNK
        end_padded = jnp.cumsum(padded)
        start_padded = end_padded - padded
        p_rows = n * TOP_K + N_EXPERTS * CHUNK
        n_chunks = p_rows // CHUNK
        chunk_start = jnp.arange(n_chunks, dtype=I32) * CHUNK
        chunk_e = jnp.minimum(
            jnp.sum((end_padded[None, :] <= chunk_start[:, None]).astype(I32), axis=1),
            N_EXPERTS - 1)
        n_used = (end_padded[-1:] // CHUNK).astype(I32)

        dest = _dest(start_padded.astype(I32), idx, rank).reshape(TOP_K * n)
        xs = _disp(dest, (start_padded + counts).astype(I32), (padded - counts).astype(I32),
                   n_used, x1t, p_rows)
        ys = _moe(chunk_e, n_used, xs, w_gu[i], b_gu[i][:, None, :],
                  w_down[i], b_down[i][:, None, :])
        out = _fin(dest, ys, x1, w_top.T, p[i].reshape(n, D_PLE),
                   w_ple_gate[i].astype(BF16), w_ple_proj[i].astype(BF16),
                   ln2_g[i][None, :], ln2_b[i][None, :])
        x = out.reshape(bsz, seq, d)
    return x
```

```python
import functools
import math

import jax
import jax.numpy as jnp
from jax import lax
from jax.experimental import pallas as pl
from jax.experimental.pallas import tpu as pltpu

F32 = jnp.float32
BF16 = jnp.bfloat16
I32 = jnp.int32

SUBLANES = 8
LANES = 128
VMEM_LIMIT_BYTES = 56 * 1024 * 1024

D_MODEL = 1024
HEAD_DIM = 64
N_Q_HEADS = 16
N_KV_HEADS = 4
GQA_GROUP = N_Q_HEADS // N_KV_HEADS
KV_DIM = N_KV_HEADS * HEAD_DIM
WINDOW = 128
N_EXPERTS = 32
TOP_K = 4
D_EXPERT = D_MODEL
SWIGLU_LIMIT = 7.0
SWIGLU_ALPHA = 1.702
D_PLE = 256
DEPTH = 1
DN_ALPHA = (2.0 * DEPTH) ** 0.25
LN_EPS = 1e-5
ROW_TILES = D_MODEL // LANES

OFF_B, OFF_C, OFF_H, OFF_Q, OFF_GC, OFF_GA = (i * D_MODEL for i in range(6))
OFF_K = 6 * D_MODEL
OFF_V = OFF_K + KV_DIM
IN_TOTAL = OFF_V + KV_DIM

TS_MIX = 512
TS_ROUTE = 512
TN_DEST = 2048
TD_DISP = 1024
CHUNK = 512
TS_FIN = 256
ISSUE_UNROLL = 4


def _layer_norm(z, g, b):
    mu = jnp.mean(z, axis=-1, keepdims=True)
    zc = z - mu
    var = jnp.mean(zc * zc, axis=-1, keepdims=True)
    return zc * lax.rsqrt(var + LN_EPS) * g + b


def _dot(a, b):
    return jnp.dot(a, b, preferred_element_type=F32)


def _dot_nt(a, b):
    return lax.dot_general(a, b, (((1,), (1,)), ((), ())), preferred_element_type=F32)


def _mix_kernel(sinks_ref, x_ref, win_ref, rep_ref, convw_ref, wbrc_ref, wbra_ref, wout_ref,
                g1_ref, b1_ref, x1_ref, x1t_ref, kext, vext, uext):
    ts = TS_MIX
    j = pl.program_id(1)

    @pl.when(j == 0)
    def _():
        kext[0:WINDOW, :] = jnp.zeros((WINDOW, D_MODEL), BF16)
        vext[0:WINDOW, :] = jnp.zeros((WINDOW, D_MODEL), BF16)
        uext[0:SUBLANES, :] = jnp.zeros((SUBLANES, D_MODEL), F32)

    x = x_ref[0]
    xb = x.astype(BF16)

    def proj(off, width):
        return _dot(xb, win_ref[:, off:off + width])

    u = proj(OFF_C, D_MODEL) * proj(OFF_H, D_MODEL)
    uext[SUBLANES:SUBLANES + ts, :] = u
    y = (convw_ref[2:3, :] * u
         + convw_ref[1:2, :] * uext[SUBLANES - 1:SUBLANES - 1 + ts, :]
         + convw_ref[0:1, :] * uext[SUBLANES - 2:SUBLANES - 2 + ts, :])
    uext[0:SUBLANES, :] = u[ts - SUBLANES:ts, :]
    yc_in = proj(OFF_B, D_MODEL) * y
    y_conv = _dot(yc_in.astype(BF16), wbrc_ref[...])
    acc = jax.nn.sigmoid(proj(OFF_GC, D_MODEL)) * y_conv

    q = proj(OFF_Q, D_MODEL).astype(BF16)
    kb = proj(OFF_K, KV_DIM).astype(BF16)
    vb = proj(OFF_V, KV_DIM).astype(BF16)
    kext[WINDOW:WINDOW + ts, :] = _dot(kb, rep_ref[...]).astype(BF16)
    vext[WINDOW:WINDOW + ts, :] = _dot(vb, rep_ref[...]).astype(BF16)

    grp = GQA_GROUP * HEAD_DIM
    row = lax.broadcasted_iota(I32, (WINDOW, 2 * WINDOW), 0)
    col = lax.broadcasted_iota(I32, (WINDOW, 2 * WINDOW), 1)
    band = (col > row) & (col <= row + WINDOW)
    q_lane_grp = lax.broadcasted_iota(I32, (WINDOW, grp), 1) // HEAD_DIM
    scale = 1.0 / math.sqrt(HEAD_DIM)
    o_blocks = []
    for i in range(ts // WINDOW):
        if i == 0:
            mask = band & ((col >= WINDOW) | (j > 0))
        else:
            mask = band
        o_heads = []
        for h in range(N_KV_HEADS):
            qh = q[i * WINDOW:(i + 1) * WINDOW, h * grp:(h + 1) * grp]
            kh = kext[i * WINDOW:i * WINDOW + 2 * WINDOW, h * grp:(h + 1) * grp]
            vh = vext[i * WINDOW:i * WINDOW + 2 * WINDOW, h * grp:(h + 1) * grp]
            qs = jnp.concatenate(
                [jnp.where(q_lane_grp == g, qh, jnp.zeros_like(qh)) for g in range(GQA_GROUP)], axis=0)
            s_all = _dot_nt(qs, kh) * scale
            ps = []
            for g in range(GQA_GROUP):
                s = jnp.where(mask, s_all[g * WINDOW:(g + 1) * WINDOW], -jnp.inf)
                sink = sinks_ref[h * GQA_GROUP + g]
                m = jnp.maximum(jnp.max(s, axis=-1, keepdims=True), sink)
                e = jnp.exp(s - m)
                den = jnp.sum(e, axis=-1, keepdims=True) + jnp.exp(sink - m)
                ps.append((e / den).astype(BF16))
            pv = _dot(jnp.concatenate(ps, axis=0), vh)
            oh = jnp.zeros((WINDOW, grp), F32)
            for g in range(GQA_GROUP):
                oh = jnp.where(q_lane_grp == g, pv[g * WINDOW:(g + 1) * WINDOW], oh)
            o_heads.append(oh)
        o_blocks.append(jnp.concatenate(o_heads, axis=-1))
    o = jnp.concatenate(o_blocks, axis=0)
    kext[0:WINDOW, :] = kext[ts:ts + WINDOW, :]
    vext[0:WINDOW, :] = vext[ts:ts + WINDOW, :]

    y_attn = _dot(o.astype(BF16), wbra_ref[...])
    acc = acc + jax.nn.sigmoid(proj(OFF_GA, D_MODEL)) * y_attn

    z = DN_ALPHA * x + _dot(acc.astype(BF16), wout_ref[...])
    x1 = _layer_norm(z, g1_ref[...], b1_ref[...])
    x1_ref[0] = x1
    for s in range(ROW_TILES):
        x1t_ref[pl.ds(s, ts, stride=ROW_TILES), :] = x1[:, s * LANES:(s + 1) * LANES]


def _mix(x, w_in_p, rep, conv_w, wbrc, wbra, sinks, wout, g1, b1):
    bsz, seq, d = x.shape
    n = bsz * seq
    nj = seq // TS_MIX
    const = lambda shape: pl.BlockSpec(shape, lambda b, j, s: (0,) * len(shape),
                                       pipeline_mode=pl.Buffered(1))
    grid_spec = pltpu.PrefetchScalarGridSpec(
        num_scalar_prefetch=1,
        grid=(bsz, nj),
        in_specs=[
            pl.BlockSpec((1, TS_MIX, d), lambda b, j, s: (b, j, 0)),
            const((d, IN_TOTAL)),
            const((KV_DIM, d)),
            const((3, d)),
            const((d, d)),
            const((d, d)),
            const((d, d)),
            const((1, d)),
            const((1, d)),
        ],
        out_specs=[
            pl.BlockSpec((1, TS_MIX, d), lambda b, j, s: (b, j, 0)),
            pl.BlockSpec((TS_MIX * ROW_TILES, LANES), lambda b, j, s: (b * nj + j, 0)),
        ],
        scratch_shapes=[
            pltpu.VMEM((WINDOW + TS_MIX, d), BF16),
            pltpu.VMEM((WINDOW + TS_MIX, d), BF16),
            pltpu.VMEM((SUBLANES + TS_MIX, d), F32),
        ],
    )
    return pl.pallas_call(
        _mix_kernel,
        grid_spec=grid_spec,
        out_shape=[jax.ShapeDtypeStruct((bsz, seq, d), F32),
                   jax.ShapeDtypeStruct((n * ROW_TILES, LANES), F32)],
        compiler_params=pltpu.CompilerParams(
            dimension_semantics=("arbitrary", "arbitrary"),
            vmem_limit_bytes=VMEM_LIMIT_BYTES),
        name="mix",
    )(sinks, x, w_in_p, rep, conv_w, wbrc, wbra, wout, g1, b1)


def _route_kernel(x1_ref, whi_ref, wlo_ref, br_ref, idx_ref, w_ref, rank_ref, cnt_ref, carry):
    ts = TS_ROUTE
    i = pl.program_id(0)

    @pl.when(i == 0)
    def _():
        carry[...] = jnp.zeros_like(carry)

    x = x1_ref[...]
    xh = x.astype(BF16)
    xl = (x - xh.astype(F32)).astype(BF16)
    whi = whi_ref[...]
    logits = _dot_nt(whi, xh) + _dot_nt(whi, xl) + _dot_nt(wlo_ref[...], xh) + br_ref[...]

    eid = lax.broadcasted_iota(I32, (N_EXPERTS, ts), 0)
    rest = logits
    sels, vals = [], []
    for k in range(TOP_K):
        m = jnp.max(rest, axis=0, keepdims=True)
        idx = jnp.min(jnp.where(rest == m, eid, N_EXPERTS), axis=0, keepdims=True)
        sel = eid == idx
        rest = jnp.where(sel, -jnp.inf, rest)
        sels.append(sel)
        vals.append(m)
        idx_ref[k:k + 1, :] = idx
    exps = [jnp.exp(v - vals[0]) for v in vals]
    den = exps[0] + exps[1] + exps[2] + exps[3]
    for k in range(TOP_K):
        w_ref[k:k + 1, :] = exps[k] / den

    member = jnp.zeros((N_EXPERTS, ts), F32)
    for sel in sels:
        member = member + sel.astype(F32)
    r = lax.broadcasted_iota(I32, (ts, ts), 0)
    c = lax.broadcasted_iota(I32, (ts, ts), 1)
    upper = (r < c).astype(BF16)
    cum = _dot(member.astype(BF16), upper) + carry[:, 0:1]
    for k in range(TOP_K):
        rk = jnp.sum(jnp.where(sels[k], cum, 0.0), axis=0, keepdims=True)
        rank_ref[k:k + 1, :] = rk.astype(I32)
    carry[...] = carry[...] + jnp.sum(member, axis=1, keepdims=True)
    cnt_ref[...] = carry[...].astype(I32)


def _route(x1, whi, wlo, br):
    n, d = x1.shape
    const = lambda shape: pl.BlockSpec(shape, lambda i: (0,) * len(shape))
    tok = pl.BlockSpec((TOP_K, TS_ROUTE), lambda i: (0, i))
    return pl.pallas_call(
        _route_kernel,
        grid=(n // TS_ROUTE,),
        in_specs=[pl.BlockSpec((TS_ROUTE, d), lambda i: (i, 0)),
                  const((N_EXPERTS, d)), const((N_EXPERTS, d)), const((N_EXPERTS, 1))],
        out_specs=[tok, tok, tok, const((N_EXPERTS, LANES))],
        out_shape=[jax.ShapeDtypeStruct((TOP_K, n), I32),
                   jax.ShapeDtypeStruct((TOP_K, n), F32),
                   jax.ShapeDtypeStruct((TOP_K, n), I32),
                   jax.ShapeDtypeStruct((N_EXPERTS, LANES), I32)],
        scratch_shapes=[pltpu.VMEM((N_EXPERTS, LANES), F32)],
        compiler_params=pltpu.CompilerParams(dimension_semantics=("arbitrary",)),
        name="route",
    )(x1, whi, wlo, br)


def _dest_kernel(start_ref, idx_ref, rank_ref, dest_ref):
    idx = idx_ref[...]
    acc = rank_ref[...]
    for e in range(N_EXPERTS):
        acc = acc + jnp.where(idx == e, start_ref[e], 0)
    dest_ref[...] = acc


def _dest(start_padded, idx, rank):
    k, n = idx.shape
    tok = pl.BlockSpec((k, TN_DEST), lambda i, s: (0, i))
    return pl.pallas_call(
        _dest_kernel,
        grid_spec=pltpu.PrefetchScalarGridSpec(
            num_scalar_prefetch=1, grid=(n // TN_DEST,), in_specs=[tok, tok], out_specs=tok),
        out_shape=jax.ShapeDtypeStruct((k, n), I32),
        name="dest",
    )(start_padded, idx, rank)


def _row_copy(src, src_row, dst, dst_row, rows, sem):
    s0 = pl.multiple_of(src_row * ROW_TILES, ROW_TILES)
    d0 = pl.multiple_of(dst_row * ROW_TILES, ROW_TILES)
    return pltpu.make_async_copy(src.at[pl.ds(s0, rows * ROW_TILES)],
                                 dst.at[pl.ds(d0, rows * ROW_TILES)], sem)


def _pad_pieces():
    return [1 << b for b in reversed(range(CHUNK.bit_length() - 1))]


def _disp_kernel(dest_ref, padpos_ref, padlen_ref, nu_ref, x1t_ref, xs_ref, zbuf, sem, zsem):
    n = dest_ref.shape[0] // TOP_K
    half = CHUNK // 2
    n_half = xs_ref.shape[0] // (half * ROW_TILES)
    i = pl.program_id(0)
    t0 = i * TD_DISP

    def issue(tb, carry):
        for u in range(ISSUE_UNROLL):
            t = tb * ISSUE_UNROLL + u
            for k in range(TOP_K):
                _row_copy(x1t_ref, t, xs_ref, dest_ref[k * n + t0 + t], 1, sem).start(priority=k % 2)
        return carry

    lax.fori_loop(0, TD_DISP // ISSUE_UNROLL, issue, 0)

    @pl.when(i == 0)
    def _():
        zbuf[...] = jnp.zeros_like(zbuf)
        for wait in (False, True):
            for e in range(N_EXPERTS):
                npad = padlen_ref[e]
                for piece in _pad_pieces():
                    pos = padpos_ref[e] + (npad & ~(2 * piece - 1))
                    cp = _row_copy(zbuf, 0, xs_ref, pos, piece, zsem)

                    @pl.when((npad & piece) != 0)
                    def _():
                        cp.wait() if wait else cp.start()

            def tail(hc, carry):
                cp = _row_copy(zbuf, 0, xs_ref, hc * half, half, zsem)
                cp.wait() if wait else cp.start()
                return carry

            lax.fori_loop(2 * nu_ref[0], n_half, tail, 0)

    for k in range(TOP_K):
        _row_copy(x1t_ref, 0, xs_ref, 0, TD_DISP, sem).wait()


def _disp(dest_flat, padpos, padlen, n_used, x1t, p_rows):
    n = dest_flat.shape[0] // TOP_K
    anyspec = pl.BlockSpec(memory_space=pl.ANY)
    return pl.pallas_call(
        _disp_kernel,
        grid_spec=pltpu.PrefetchScalarGridSpec(
            num_scalar_prefetch=4, grid=(n // TD_DISP,),
            in_specs=[pl.BlockSpec((TD_DISP * ROW_TILES, LANES), lambda i, *_: (i, 0))],
            out_specs=anyspec,
            scratch_shapes=[pltpu.VMEM((CHUNK // 2 * ROW_TILES, LANES), F32),
                            pltpu.SemaphoreType.DMA, pltpu.SemaphoreType.DMA]),
        out_shape=jax.ShapeDtypeStruct((p_rows * ROW_TILES, LANES), F32),
        compiler_params=pltpu.CompilerParams(dimension_semantics=("arbitrary",),
                                             has_side_effects=True),
        name="disp",
    )(dest_flat, padpos, padlen, n_used, x1t)


def _moe_kernel(ce_ref, nu_ref, slot_ref, nxt_ref, xs_ref, wgu_hbm, bgu_ref, wd_hbm, bd_ref, ys_ref,
                wgu_f32, wd_f32, wgu_bf, wd_bf, wsems):
    c = pl.program_id(0)
    used = c < nu_ref[0]

    def weight_copies(e, slot):
        return (pltpu.make_async_copy(wgu_hbm.at[e], wgu_f32.at[slot], wsems.at[0, slot]),
                pltpu.make_async_copy(wd_hbm.at[e], wd_f32.at[slot], wsems.at[1, slot]))

    @pl.when(used & ((c == 0) | (ce_ref[c] != ce_ref[jnp.maximum(c - 1, 0)])))
    def _():
        slot = slot_ref[c]

        @pl.when(c == 0)
        def _():
            for cp in weight_copies(ce_ref[0], slot):
                cp.start()

        for cp in weight_copies(ce_ref[c], slot):
            cp.wait()

        @pl.when(nxt_ref[c] >= 0)
        def _():
            for cp in weight_copies(nxt_ref[c], 1 - slot):
                cp.start()

        wgu_bf[...] = wgu_f32[slot].astype(BF16)
        wd_bf[...] = wd_f32[slot].astype(BF16)

    @pl.when(used)
    def _():
        x = jnp.concatenate(
            [xs_ref[pl.ds(s, CHUNK, stride=ROW_TILES), :] for s in range(ROW_TILES)], axis=-1)
        gu = _dot(x.astype(BF16), wgu_bf[...]) + bgu_ref[0]
        gate = jnp.minimum(gu[:, :D_EXPERT], SWIGLU_LIMIT)
        up = jnp.clip(gu[:, D_EXPERT:], -SWIGLU_LIMIT, SWIGLU_LIMIT)
        h = (up + 1.0) * gate * jax.nn.sigmoid(SWIGLU_ALPHA * gate)
        y = _dot(h.astype(BF16), wd_bf[...]) + bd_ref[0]
        for s in range(ROW_TILES):
            ys_ref[pl.ds(s, CHUNK, stride=ROW_TILES), :] = y[:, s * LANES:(s + 1) * LANES]

    @pl.when(c >= nu_ref[0])
    def _():
        ys_ref[...] = jnp.zeros_like(ys_ref)


def _moe(chunk_e, n_used, run_slot, next_e, xs, wgu, bgu, wd, bd):
    n_chunks = xs.shape[0] // (CHUNK * ROW_TILES)
    d = D_MODEL

    def cc(c, ce, nu, *_):
        return jnp.minimum(c, nu[0] - 1)

    rows = pl.BlockSpec((CHUNK * ROW_TILES, LANES), lambda c, *s: (cc(c, *s), 0))
    bspec = lambda shape: pl.BlockSpec(shape, lambda c, *s: (s[0][cc(c, *s)], 0, 0))
    anyspec = pl.BlockSpec(memory_space=pl.ANY)
    return pl.pallas_call(
        _moe_kernel,
        grid_spec=pltpu.PrefetchScalarGridSpec(
            num_scalar_prefetch=4, grid=(n_chunks,),
            in_specs=[rows, anyspec, bspec((1, 1, 2 * D_EXPERT)), anyspec, bspec((1, 1, d))],
            out_specs=pl.BlockSpec((CHUNK * ROW_TILES, LANES), lambda c, *s: (c, 0)),
            scratch_shapes=[pltpu.VMEM((2, d, 2 * D_EXPERT), F32), pltpu.VMEM((2, D_EXPERT, d), F32),
                            pltpu.VMEM((d, 2 * D_EXPERT), BF16), pltpu.VMEM((D_EXPERT, d), BF16),
                            pltpu.SemaphoreType.DMA((2, 2))]),
        out_shape=jax.ShapeDtypeStruct(xs.shape, F32),
        compiler_params=pltpu.CompilerParams(dimension_semantics=("arbitrary",),
                                             vmem_limit_bytes=VMEM_LIMIT_BYTES),
        name="moe",
    )(chunk_e, n_used, run_slot, next_e, xs, wgu, bgu, wd, bd)


def _fin_kernel(dest_ref, ys_ref, x1_ref, w_ref, p_ref, wpg_ref, wpp_ref, g2_ref, b2_ref,
                out_ref, rows, sems):
    ts = TS_FIN
    n = dest_ref.shape[0] // TOP_K
    i = pl.program_id(0)
    slot = i % 2

    def gather_tile(tile, into):
        def issue(tb, carry):
            for u in range(ISSUE_UNROLL):
                t = tb * ISSUE_UNROLL + u
                for k in range(TOP_K):
                    _row_copy(ys_ref, dest_ref[k * n + tile * ts + t], rows.at[into, k], t, 1,
                              sems.at[into]).start(priority=k % 2)
            return carry

        lax.fori_loop(0, ts // ISSUE_UNROLL, issue, 0)

    @pl.when(i == 0)
    def _():
        gather_tile(0, 0)

    last = pl.num_programs(0) - 1
    nxt = jnp.minimum(i + 1, last)
    for t in range(ts):
        for k in range(TOP_K):
            _row_copy(ys_ref, dest_ref[k * n + nxt * ts + t], rows.at[1 - slot, k], t, 1,
                      sems.at[1 - slot]).start(priority=k % 2)

    x1 = x1_ref[...]
    ple = jax.nn.sigmoid(_dot(x1.astype(BF16), wpg_ref[...])) * _dot(p_ref[...].astype(BF16), wpp_ref[...])
    z = DN_ALPHA * x1 + ple

    for k in range(TOP_K):
        _row_copy(ys_ref, 0, rows.at[slot, k], 0, ts, sems.at[slot]).wait()
    w = w_ref[...]
    for k in range(TOP_K):
        yk = jnp.concatenate(
            [rows[slot, k, pl.ds(s, ts, stride=ROW_TILES), :] for s in range(ROW_TILES)], axis=-1)
        z = z + w[:, k:k + 1] * yk
    out_ref[...] = _layer_norm(z, g2_ref[...], b2_ref[...])

    @pl.when(i == last)
    def _():
        for k in range(TOP_K):
            _row_copy(ys_ref, 0, rows.at[1 - slot, k], 0, ts, sems.at[1 - slot]).wait()


def _fin(dest_flat, ys, x1, w_rows, p2, wpg, wpp, g2, b2):
    n, d = x1.shape
    const = lambda shape: pl.BlockSpec(shape, lambda i, s: (0,) * len(shape))
    tile = lambda width: pl.BlockSpec((TS_FIN, width), lambda i, s: (i, 0))
    return pl.pallas_call(
        _fin_kernel,
        grid_spec=pltpu.PrefetchScalarGridSpec(
            num_scalar_prefetch=1, grid=(n // TS_FIN,),
            in_specs=[pl.BlockSpec(memory_space=pl.ANY), tile(d), tile(TOP_K), tile(D_PLE),
                      const((d, d)), const((D_PLE, d)), const((1, d)), const((1, d))],
            out_specs=tile(d),
            scratch_shapes=[pltpu.VMEM((2, TOP_K, TS_FIN * ROW_TILES, LANES), F32),
                            pltpu.SemaphoreType.DMA((2,))]),
        out_shape=jax.ShapeDtypeStruct((n, d), F32),
        compiler_params=pltpu.CompilerParams(dimension_semantics=("arbitrary",),
                                             vmem_limit_bytes=VMEM_LIMIT_BYTES),
        name="fin",
    )(dest_flat, ys, x1, w_rows, p2, wpg, wpp, g2, b2)


def _permute_w_in(w_in):
    d = D_MODEL
    b_, c_, h_, q_ = (w_in[:, i * d:(i + 1) * d] for i in range(4))
    k_ = w_in[:, 4 * d:4 * d + KV_DIM]
    v_ = w_in[:, 4 * d + KV_DIM:4 * d + 2 * KV_DIM]
    gc_ = w_in[:, 4 * d + 2 * KV_DIM:5 * d + 2 * KV_DIM]
    ga_ = w_in[:, 5 * d + 2 * KV_DIM:]
    return jnp.concatenate([b_, c_, h_, q_, gc_, ga_, k_, v_], axis=1).astype(BF16)


def _replication_matrix():
    src = jnp.arange(KV_DIM)[:, None]
    dst = jnp.arange(D_MODEL)[None, :]
    same_head = (dst // (GQA_GROUP * HEAD_DIM)) == (src // HEAD_DIM)
    same_dim = (dst % HEAD_DIM) == (src % HEAD_DIM)
    return (same_head & same_dim).astype(BF16)


def kernel(x, p, w_in, conv_w, w_br_conv, w_br_attn, attn_sinks, w_out, ln1_g, ln1_b, w_router,
           b_router, w_gu, b_gu, w_down, b_down, w_ple_proj, w_ple_gate, ln2_g, ln2_b):
    bsz, seq, d = x.shape
    n = bsz * seq
    for i in range(DEPTH):
        x1, x1t = _mix(x, _permute_w_in(w_in[i]), _replication_matrix(), conv_w[i],
                       w_br_conv[i].astype(BF16), w_br_attn[i].astype(BF16),
                       attn_sinks[i], w_out[i].astype(BF16),
                       ln1_g[i][None, :], ln1_b[i][None, :])
        x1 = x1.reshape(n, d)

        wr_t = w_router[i].T
        wr_hi = wr_t.astype(BF16)
        wr_lo = (wr_t - wr_hi.astype(F32)).astype(BF16)
        idx, w_top, rank, cnt = _route(x1, wr_hi, wr_lo, b_router[i][:, None])

        counts = cnt[:, 0]
        padded = ((counts + CHUNK - 1) // CHUNK) * CHUNK
        end_padded = jnp.cumsum(padded)
        start_padded = end_padded - padded
        p_rows = n * TOP_K + N_EXPERTS * CHUNK
        n_chunks = p_rows // CHUNK
        chunk_start = jnp.arange(n_chunks, dtype=I32) * CHUNK
        chunk_e = jnp.minimum(
            jnp.sum((end_padded[None, :] <= chunk_start[:, None]).astype(I32), axis=1),
            N_EXPERTS - 1)
        n_used = (end_padded[-1:] // CHUNK).astype(I32)

        dest = _dest(start_padded.astype(I32), idx, rank).reshape(TOP_K * n)
        xs = _disp(dest, (start_padded + counts).astype(I32), (padded - counts).astype(I32),
                   n_used, x1t, p_rows)
        eid = jnp.arange(N_EXPERTS, dtype=I32)
        active = padded > 0
        run_idx = jnp.cumsum(active.astype(I32)) - 1
        later = active[None, :] & (eid[None, :] > eid[:, None])
        nxt = jnp.min(jnp.where(later, eid[None, :], N_EXPERTS), axis=1)
        nxt = jnp.where(nxt == N_EXPERTS, -1, nxt)
        onehot = (chunk_e[:, None] == eid[None, :]).astype(I32)
        run_slot = jnp.sum(onehot * (run_idx & 1)[None, :], axis=1).astype(I32)
        next_e = jnp.sum(onehot * nxt[None, :], axis=1).astype(I32)
        ys = _moe(chunk_e, n_used, run_slot, next_e, xs, w_gu[i], b_gu[i][:, None, :],
                  w_down[i], b_down[i][:, None, :])
        out = _fin(dest, ys, x1, w_top.T, p[i].reshape(n, D_PLE),
                   w_ple_gate[i].astype(BF16), w_ple_proj[i].astype(BF16),
                   ln2_g[i][None, :], ln2_b[i][None, :])
        x = out.reshape(bsz, seq, d)
    return x
```

```python
import math

import jax
import jax.numpy as jnp
from jax import lax
from jax.experimental import pallas as pl
from jax.experimental.pallas import tpu as pltpu

F32 = jnp.float32
BF16 = jnp.bfloat16
I32 = jnp.int32

SUBLANES = 8
LANES = 128
VMEM_LIMIT_BYTES = 56 * 1024 * 1024

D_MODEL = 1024
HEAD_DIM = 64
N_Q_HEADS = 16
N_KV_HEADS = 4
GQA_GROUP = N_Q_HEADS // N_KV_HEADS
KV_DIM = N_KV_HEADS * HEAD_DIM
WINDOW = 128
N_EXPERTS = 32
TOP_K = 4
D_EXPERT = D_MODEL
SWIGLU_LIMIT = 7.0
SWIGLU_ALPHA = 1.702
D_PLE = 256
DEPTH = 1
DN_ALPHA = (2.0 * DEPTH) ** 0.25
LN_EPS = 1e-5
ROW_TILES = D_MODEL // LANES

OFF_B, OFF_C, OFF_H, OFF_Q, OFF_GC, OFF_GA = (i * D_MODEL for i in range(6))
OFF_K = 6 * D_MODEL
OFF_V = OFF_K + KV_DIM
IN_TOTAL = OFF_V + KV_DIM

TS_MIX = 512
TS_ROUTE = 512
TN_DEST = 2048
CHUNK = 512
TS_FIN = 256
INV_UNROLL = 8


def _layer_norm(z, g, b):
    mu = jnp.mean(z, axis=-1, keepdims=True)
    zc = z - mu
    var = jnp.mean(zc * zc, axis=-1, keepdims=True)
    return zc * lax.rsqrt(var + LN_EPS) * g + b


def _dot(a, b):
    return jnp.dot(a, b, preferred_element_type=F32)


def _dot_nt(a, b):
    return lax.dot_general(a, b, (((1,), (1,)), ((), ())), preferred_element_type=F32)


def _mix_kernel(sinks_ref, x_ref, win_ref, rep_ref, convw_ref, wbrc_ref, wbra_ref, wout_ref,
                g1_ref, b1_ref, x1_ref, x1t_ref, kext, vext, uext):
    ts = TS_MIX
    j = pl.program_id(1)

    @pl.when(j == 0)
    def _():
        kext[0:WINDOW, :] = jnp.zeros((WINDOW, D_MODEL), BF16)
        vext[0:WINDOW, :] = jnp.zeros((WINDOW, D_MODEL), BF16)
        uext[0:SUBLANES, :] = jnp.zeros((SUBLANES, D_MODEL), F32)

    x = x_ref[0]
    xb = x.astype(BF16)

    def proj(off, width):
        return _dot(xb, win_ref[:, off:off + width])

    u = proj(OFF_C, D_MODEL) * proj(OFF_H, D_MODEL)
    uext[SUBLANES:SUBLANES + ts, :] = u
    y = (convw_ref[2:3, :] * u
         + convw_ref[1:2, :] * uext[SUBLANES - 1:SUBLANES - 1 + ts, :]
         + convw_ref[0:1, :] * uext[SUBLANES - 2:SUBLANES - 2 + ts, :])
    uext[0:SUBLANES, :] = u[ts - SUBLANES:ts, :]
    yc_in = proj(OFF_B, D_MODEL) * y
    y_conv = _dot(yc_in.astype(BF16), wbrc_ref[...])
    acc = jax.nn.sigmoid(proj(OFF_GC, D_MODEL)) * y_conv

    q = proj(OFF_Q, D_MODEL).astype(BF16)
    kb = proj(OFF_K, KV_DIM).astype(BF16)
    vb = proj(OFF_V, KV_DIM).astype(BF16)
    kext[WINDOW:WINDOW + ts, :] = _dot(kb, rep_ref[...]).astype(BF16)
    vext[WINDOW:WINDOW + ts, :] = _dot(vb, rep_ref[...]).astype(BF16)

    grp = GQA_GROUP * HEAD_DIM
    row = lax.broadcasted_iota(I32, (WINDOW, 2 * WINDOW), 0)
    col = lax.broadcasted_iota(I32, (WINDOW, 2 * WINDOW), 1)
    band = (col > row) & (col <= row + WINDOW)
    q_lane_grp = lax.broadcasted_iota(I32, (WINDOW, grp), 1) // HEAD_DIM
    scale = 1.0 / math.sqrt(HEAD_DIM)
    o_blocks = []
    for i in range(ts // WINDOW):
        if i == 0:
            mask = band & ((col >= WINDOW) | (j > 0))
        else:
            mask = band
        o_heads = []
        for h in range(N_KV_HEADS):
            qh = q[i * WINDOW:(i + 1) * WINDOW, h * grp:(h + 1) * grp]
            kh = kext[i * WINDOW:i * WINDOW + 2 * WINDOW, h * grp:(h + 1) * grp]
            vh = vext[i * WINDOW:i * WINDOW + 2 * WINDOW, h * grp:(h + 1) * grp]
            qs = jnp.concatenate(
                [jnp.where(q_lane_grp == g, qh, jnp.zeros_like(qh)) for g in range(GQA_GROUP)], axis=0)
            s_all = _dot_nt(qs, kh) * scale
            ps = []
            for g in range(GQA_GROUP):
                s = jnp.where(mask, s_all[g * WINDOW:(g + 1) * WINDOW], -jnp.inf)
                sink = sinks_ref[h * GQA_GROUP + g]
                m = jnp.maximum(jnp.max(s, axis=-1, keepdims=True), sink)
                e = jnp.exp(s - m)
                den = jnp.sum(e, axis=-1, keepdims=True) + jnp.exp(sink - m)
                ps.append((e / den).astype(BF16))
            pv = _dot(jnp.concatenate(ps, axis=0), vh)
            oh = jnp.zeros((WINDOW, grp), F32)
            for g in range(GQA_GROUP):
                oh = jnp.where(q_lane_grp == g, pv[g * WINDOW:(g + 1) * WINDOW], oh)
            o_heads.append(oh)
        o_blocks.append(jnp.concatenate(o_heads, axis=-1))
    o = jnp.concatenate(o_blocks, axis=0)
    kext[0:WINDOW, :] = kext[ts:ts + WINDOW, :]
    vext[0:WINDOW, :] = vext[ts:ts + WINDOW, :]

    y_attn = _dot(o.astype(BF16), wbra_ref[...])
    acc = acc + jax.nn.sigmoid(proj(OFF_GA, D_MODEL)) * y_attn

    z = DN_ALPHA * x + _dot(acc.astype(BF16), wout_ref[...])
    x1 = _layer_norm(z, g1_ref[...], b1_ref[...])
    x1_ref[0] = x1
    for s in range(ROW_TILES):
        x1t_ref[pl.ds(s, ts, stride=ROW_TILES), :] = x1[:, s * LANES:(s + 1) * LANES]


def _mix(x, w_in_p, rep, conv_w, wbrc, wbra, sinks, wout, g1, b1):
    bsz, seq, d = x.shape
    n = bsz * seq
    nj = seq // TS_MIX
    const = lambda shape: pl.BlockSpec(shape, lambda b, j, s: (0,) * len(shape),
                                       pipeline_mode=pl.Buffered(1))
    grid_spec = pltpu.PrefetchScalarGridSpec(
        num_scalar_prefetch=1,
        grid=(bsz, nj),
        in_specs=[
            pl.BlockSpec((1, TS_MIX, d), lambda b, j, s: (b, j, 0)),
            const((d, IN_TOTAL)),
            const((KV_DIM, d)),
            const((3, d)),
            const((d, d)),
            const((d, d)),
            const((d, d)),
            const((1, d)),
            const((1, d)),
        ],
        out_specs=[
            pl.BlockSpec((1, TS_MIX, d), lambda b, j, s: (b, j, 0)),
            pl.BlockSpec((TS_MIX * ROW_TILES, LANES), lambda b, j, s: (b * nj + j, 0)),
        ],
        scratch_shapes=[
            pltpu.VMEM((WINDOW + TS_MIX, d), BF16),
            pltpu.VMEM((WINDOW + TS_MIX, d), BF16),
            pltpu.VMEM((SUBLANES + TS_MIX, d), F32),
        ],
    )
    return pl.pallas_call(
        _mix_kernel,
        grid_spec=grid_spec,
        out_shape=[jax.ShapeDtypeStruct((bsz, seq, d), F32),
                   jax.ShapeDtypeStruct((n * ROW_TILES, LANES), F32)],
        compiler_params=pltpu.CompilerParams(
            dimension_semantics=("arbitrary", "arbitrary"),
            vmem_limit_bytes=VMEM_LIMIT_BYTES),
        name="mix",
    )(sinks, x, w_in_p, rep, conv_w, wbrc, wbra, wout, g1, b1)


def _route_kernel(x1_ref, whi_ref, wlo_ref, br_ref, idx_ref, w_ref, rank_ref, cnt_ref, carry):
    ts = TS_ROUTE
    i = pl.program_id(0)

    @pl.when(i == 0)
    def _():
        carry[...] = jnp.zeros_like(carry)

    x = x1_ref[...]
    xh = x.astype(BF16)
    xl = (x - xh.astype(F32)).astype(BF16)
    whi = whi_ref[...]
    logits = _dot_nt(whi, xh) + _dot_nt(whi, xl) + _dot_nt(wlo_ref[...], xh) + br_ref[...]

    eid = lax.broadcasted_iota(I32, (N_EXPERTS, ts), 0)
    rest = logits
    sels, vals = [], []
    for k in range(TOP_K):
        m = jnp.max(rest, axis=0, keepdims=True)
        idx = jnp.min(jnp.where(rest == m, eid, N_EXPERTS), axis=0, keepdims=True)
        sel = eid == idx
        rest = jnp.where(sel, -jnp.inf, rest)
        sels.append(sel)
        vals.append(m)
        idx_ref[k:k + 1, :] = idx
    exps = [jnp.exp(v - vals[0]) for v in vals]
    den = exps[0] + exps[1] + exps[2] + exps[3]
    for k in range(TOP_K):
        w_ref[k:k + 1, :] = exps[k] / den

    member = jnp.zeros((N_EXPERTS, ts), F32)
    for sel in sels:
        member = member + sel.astype(F32)
    r = lax.broadcasted_iota(I32, (ts, ts), 0)
    c = lax.broadcasted_iota(I32, (ts, ts), 1)
    upper = (r < c).astype(BF16)
    cum = _dot(member.astype(BF16), upper) + carry[:, 0:1]
    for k in range(TOP_K):
        rk = jnp.sum(jnp.where(sels[k], cum, 0.0), axis=0, keepdims=True)
        rank_ref[k:k + 1, :] = rk.astype(I32)
    carry[...] = carry[...] + jnp.sum(member, axis=1, keepdims=True)
    cnt_ref[...] = carry[...].astype(I32)


def _route(x1, whi, wlo, br):
    n, d = x1.shape
    const = lambda shape: pl.BlockSpec(shape, lambda i: (0,) * len(shape))
    tok = pl.BlockSpec((TOP_K, TS_ROUTE), lambda i: (0, i))
    return pl.pallas_call(
        _route_kernel,
        grid=(n // TS_ROUTE,),
        in_specs=[pl.BlockSpec((TS_ROUTE, d), lambda i: (i, 0)),
                  const((N_EXPERTS, d)), const((N_EXPERTS, d)), const((N_EXPERTS, 1))],
        out_specs=[tok, tok, tok, const((N_EXPERTS, LANES))],
        out_shape=[jax.ShapeDtypeStruct((TOP_K, n), I32),
                   jax.ShapeDtypeStruct((TOP_K, n), F32),
                   jax.ShapeDtypeStruct((TOP_K, n), I32),
                   jax.ShapeDtypeStruct((N_EXPERTS, LANES), I32)],
        scratch_shapes=[pltpu.VMEM((N_EXPERTS, LANES), F32)],
        compiler_params=pltpu.CompilerParams(dimension_semantics=("arbitrary",)),
        name="route",
    )(x1, whi, wlo, br)


def _dest_kernel(start_ref, idx_ref, rank_ref, dest_ref):
    idx = idx_ref[...]
    acc = rank_ref[...]
    for e in range(N_EXPERTS):
        acc = acc + jnp.where(idx == e, start_ref[e], 0)
    dest_ref[...] = acc


def _dest(start_padded, idx, rank):
    k, n = idx.shape
    tok = pl.BlockSpec((k, TN_DEST), lambda i, s: (0, i))
    return pl.pallas_call(
        _dest_kernel,
        grid_spec=pltpu.PrefetchScalarGridSpec(
            num_scalar_prefetch=1, grid=(n // TN_DEST,), in_specs=[tok, tok], out_specs=tok),
        out_shape=jax.ShapeDtypeStruct((k, n), I32),
        name="dest",
    )(start_padded, idx, rank)


def _row_copy(src, src_row, dst, dst_row, rows, sem):
    s0 = pl.multiple_of(src_row * ROW_TILES, ROW_TILES)
    d0 = pl.multiple_of(dst_row * ROW_TILES, ROW_TILES)
    return pltpu.make_async_copy(src.at[pl.ds(s0, rows * ROW_TILES)],
                                 dst.at[pl.ds(d0, rows * ROW_TILES)], sem)


def _inv_kernel(dest_ref, padpos_ref, padlen_ref, nu_ref, inv_ref):
    n4 = dest_ref.shape[0]
    p_rows = inv_ref.shape[0] - CHUNK

    def spare(pos):
        return n4 + (pos & (CHUNK - 1))

    def lead(r, carry):
        inv_ref[r] = spare(r)
        return carry

    lax.fori_loop(0, CHUNK, lead, 0)

    def scatter(ab, carry):
        for u in range(INV_UNROLL):
            a = ab * INV_UNROLL + u
            inv_ref[CHUNK + dest_ref[a]] = a
        return carry

    lax.fori_loop(0, n4 // INV_UNROLL, scatter, 0)

    def fill(pos, carry):
        inv_ref[CHUNK + pos] = spare(pos)
        return carry

    for e in range(N_EXPERTS):
        lax.fori_loop(padpos_ref[e], padpos_ref[e] + padlen_ref[e], fill, 0)
    lax.fori_loop(nu_ref[0] * CHUNK, p_rows, fill, 0)


def _inv(dest_flat, padpos, padlen, n_used, p_rows):
    return pl.pallas_call(
        _inv_kernel,
        grid_spec=pltpu.PrefetchScalarGridSpec(
            num_scalar_prefetch=4, grid=(1,), in_specs=[],
            out_specs=pl.BlockSpec(memory_space=pltpu.SMEM)),
        out_shape=jax.ShapeDtypeStruct((CHUNK + p_rows,), I32),
        name="inv",
    )(dest_flat, padpos, padlen, n_used)


def _moe_kernel(ce_ref, nu_ref, slot_ref, nxt_ref, inv_ref,
                x1t_hbm, wgu_hbm, bgu_ref, wd_hbm, bd_ref, y4_hbm,
                xa, xb, ya, yb, wgu_f32, wd_f32, wgu_bf, wd_bf, wsems, gsems, ssems):
    n_tok = x1t_hbm.shape[0] // ROW_TILES
    c = pl.program_id(0)
    n_used = nu_ref[0]
    xbufs, ybufs = (xa, xb), (ya, yb)

    def gather(chunk, buf, sem, start):
        base = (chunk + 1) * CHUNK
        if not start:
            return _row_copy(x1t_hbm, 0, buf, 0, CHUNK, sem).wait()
        for r in range(CHUNK):
            tok = inv_ref[base + r] & (n_tok - 1)
            _row_copy(x1t_hbm, tok, buf, r, 1, sem).start(priority=0)

    def scatter(chunk, buf, sem, start):
        base = (chunk + 1) * CHUNK
        if not start:
            return _row_copy(buf, 0, y4_hbm, 0, CHUNK, sem).wait()
        for r in range(CHUNK):
            _row_copy(buf, r, y4_hbm, inv_ref[base + r], 1, sem).start(priority=1)

    def weight_copies(e, slot):
        return (pltpu.make_async_copy(wgu_hbm.at[e], wgu_f32.at[slot], wsems.at[0, slot]),
                pltpu.make_async_copy(wd_hbm.at[e], wd_f32.at[slot], wsems.at[1, slot]))

    @pl.when(c == 0)
    def _():
        yb[...] = jnp.zeros_like(yb)
        gather(0, xa, gsems.at[0], True)

    cc = jnp.minimum(c, ce_ref.shape[0] - 1)

    @pl.when((c < n_used) & ((c == 0) | (ce_ref[cc] != ce_ref[jnp.maximum(cc - 1, 0)])))
    def _():
        slot = slot_ref[c]

        @pl.when(c == 0)
        def _():
            for cp in weight_copies(ce_ref[0], slot):
                cp.start()

        for cp in weight_copies(ce_ref[cc], slot):
            cp.wait()

        @pl.when(nxt_ref[c] >= 0)
        def _():
            for cp in weight_copies(nxt_ref[c], 1 - slot):
                cp.start()

        wgu_bf[...] = wgu_f32[slot].astype(BF16)
        wd_bf[...] = wd_f32[slot].astype(BF16)

    for par in range(2):
        x_cur, x_nxt, y_cur, y_prv = xbufs[par], xbufs[1 - par], ybufs[par], ybufs[1 - par]

        @pl.when((c % 2 == par) & (c <= n_used))
        def _():
            gather(c, x_cur, gsems.at[par], False)

            @pl.when(c >= 1)
            def _():
                scatter(c - 2, y_cur, ssems.at[par], False)

        @pl.when((c % 2 == par) & (c < n_used))
        def _():
            gather(jnp.minimum(c + 1, n_used - 1), x_nxt, gsems.at[1 - par], True)
            scatter(c - 1, y_prv, ssems.at[1 - par], True)
            x = jnp.concatenate(
                [x_cur[pl.ds(s, CHUNK, stride=ROW_TILES), :] for s in range(ROW_TILES)], axis=-1)
            gu = _dot(x.astype(BF16), wgu_bf[...]) + bgu_ref[0]
            gate = jnp.minimum(gu[:, :D_EXPERT], SWIGLU_LIMIT)
            up = jnp.clip(gu[:, D_EXPERT:], -SWIGLU_LIMIT, SWIGLU_LIMIT)
            h = (up + 1.0) * gate * jax.nn.sigmoid(SWIGLU_ALPHA * gate)
            y = _dot(h.astype(BF16), wd_bf[...]) + bd_ref[0]
            for s in range(ROW_TILES):
                y_cur[pl.ds(s, CHUNK, stride=ROW_TILES), :] = y[:, s * LANES:(s + 1) * LANES]

        @pl.when((c % 2 == par) & (c == n_used))
        def _():
            scatter(c - 1, y_prv, ssems.at[1 - par], True)
            scatter(c - 1, y_prv, ssems.at[1 - par], False)


def _moe(chunk_e, n_used, run_slot, next_e, inv, x1t, wgu, bgu, wd, bd):
    n_chunks = inv.shape[0] // CHUNK - 1
    n_tok = x1t.shape[0] // ROW_TILES
    d = D_MODEL
    bspec = lambda shape: pl.BlockSpec(
        shape, lambda c, ce, nu, *_: (ce[jnp.minimum(c, nu[0] - 1)], 0, 0))
    anyspec = pl.BlockSpec(memory_space=pl.ANY)
    rowbuf = pltpu.VMEM((CHUNK * ROW_TILES, LANES), F32)
    return pl.pallas_call(
        _moe_kernel,
        grid_spec=pltpu.PrefetchScalarGridSpec(
            num_scalar_prefetch=5, grid=(n_chunks + 1,),
            in_specs=[anyspec, anyspec, bspec((1, 1, 2 * D_EXPERT)), anyspec, bspec((1, 1, d))],
            out_specs=anyspec,
            scratch_shapes=[rowbuf, rowbuf, rowbuf, rowbuf,
                            pltpu.VMEM((2, d, 2 * D_EXPERT), F32), pltpu.VMEM((2, D_EXPERT, d), F32),
                            pltpu.VMEM((d, 2 * D_EXPERT), BF16), pltpu.VMEM((D_EXPERT, d), BF16),
                            pltpu.SemaphoreType.DMA((2, 2)), pltpu.SemaphoreType.DMA((2,)),
                            pltpu.SemaphoreType.DMA((2,))]),
        out_shape=jax.ShapeDtypeStruct(((TOP_K * n_tok + CHUNK) * ROW_TILES, LANES), F32),
        compiler_params=pltpu.CompilerParams(dimension_semantics=("arbitrary",),
                                             vmem_limit_bytes=VMEM_LIMIT_BYTES,
                                             has_side_effects=True),
        name="moe",
    )(chunk_e, n_used, run_slot, next_e, inv, x1t, wgu, bgu, wd, bd)


def _fin_kernel(x1_ref, w_ref, p_ref, y0_ref, y1_ref, y2_ref, y3_ref, wpg_ref, wpp_ref, g2_ref,
                b2_ref, out_ref):
    ts = TS_FIN
    x1 = x1_ref[...]
    ple = jax.nn.sigmoid(_dot(x1.astype(BF16), wpg_ref[...])) * _dot(p_ref[...].astype(BF16), wpp_ref[...])
    z = DN_ALPHA * x1 + ple
    w = w_ref[...]
    for k, yk_ref in enumerate((y0_ref, y1_ref, y2_ref, y3_ref)):
        yk = jnp.concatenate(
            [yk_ref[pl.ds(s, ts, stride=ROW_TILES), :] for s in range(ROW_TILES)], axis=-1)
        z = z + w[:, k:k + 1] * yk
    out_ref[...] = _layer_norm(z, g2_ref[...], b2_ref[...])


def _fin(y4, x1, w_rows, p2, wpg, wpp, g2, b2):
    n, d = x1.shape
    nt = n // TS_FIN
    const = lambda shape: pl.BlockSpec(shape, lambda i: (0,) * len(shape))
    tile = lambda width: pl.BlockSpec((TS_FIN, width), lambda i: (i, 0))
    yspec = lambda k: pl.BlockSpec((TS_FIN * ROW_TILES, LANES), lambda i: (k * nt + i, 0))
    return pl.pallas_call(
        _fin_kernel,
        grid=(nt,),
        in_specs=[tile(d), tile(TOP_K), tile(D_PLE)] + [yspec(k) for k in range(TOP_K)]
                 + [const((d, d)), const((D_PLE, d)), const((1, d)), const((1, d))],
        out_specs=tile(d),
        out_shape=jax.ShapeDtypeStruct((n, d), F32),
        compiler_params=pltpu.CompilerParams(dimension_semantics=("arbitrary",),
                                             vmem_limit_bytes=VMEM_LIMIT_BYTES),
        name="fin",
    )(x1, w_rows, p2, y4, y4, y4, y4, wpg, wpp, g2, b2)


def _permute_w_in(w_in):
    d = D_MODEL
    b_, c_, h_, q_ = (w_in[:, i * d:(i + 1) * d] for i in range(4))
    k_ = w_in[:, 4 * d:4 * d + KV_DIM]
    v_ = w_in[:, 4 * d + KV_DIM:4 * d + 2 * KV_DIM]
    gc_ = w_in[:, 4 * d + 2 * KV_DIM:5 * d + 2 * KV_DIM]
    ga_ = w_in[:, 5 * d + 2 * KV_DIM:]
    return jnp.concatenate([b_, c_, h_, q_, gc_, ga_, k_, v_], axis=1).astype(BF16)


def _replication_matrix():
    src = jnp.arange(KV_DIM)[:, None]
    dst = jnp.arange(D_MODEL)[None, :]
    same_head = (dst // (GQA_GROUP * HEAD_DIM)) == (src // HEAD_DIM)
    same_dim = (dst % HEAD_DIM) == (src % HEAD_DIM)
    return (same_head & same_dim).astype(BF16)


def kernel(x, p, w_in, conv_w, w_br_conv, w_br_attn, attn_sinks, w_out, ln1_g, ln1_b, w_router,
           b_router, w_gu, b_gu, w_down, b_down, w_ple_proj, w_ple_gate, ln2_g, ln2_b):
    bsz, seq, d = x.shape
    n = bsz * seq
    for i in range(DEPTH):
        x1, x1t = _mix(x, _permute_w_in(w_in[i]), _replication_matrix(), conv_w[i],
                       w_br_conv[i].astype(BF16), w_br_attn[i].astype(BF16),
                       attn_sinks[i], w_out[i].astype(BF16),
                       ln1_g[i][None, :], ln1_b[i][None, :])
        x1 = x1.reshape(n, d)

        wr_t = w_router[i].T
        wr_hi = wr_t.astype(BF16)
        wr_lo = (wr_t - wr_hi.astype(F32)).astype(BF16)
        idx, w_top, rank, cnt = _route(x1, wr_hi, wr_lo, b_router[i][:, None])

        counts = cnt[:, 0]
        padded = ((counts + CHUNK - 1) // CHUNK) * CHUNK
        end_padded = jnp.cumsum(padded)
        start_padded = end_padded - padded
        p_rows = n * TOP_K + N_EXPERTS * CHUNK
        n_chunks = p_rows // CHUNK
        chunk_start = jnp.arange(n_chunks, dtype=I32) * CHUNK
        chunk_e = jnp.minimum(
            jnp.sum((end_padded[None, :] <= chunk_start[:, None]).astype(I32), axis=1),
            N_EXPERTS - 1)
        n_used = (end_padded[-1:] // CHUNK).astype(I32)

        dest = _dest(start_padded.astype(I32), idx, rank).reshape(TOP_K * n)
        inv = _inv(dest, (start_padded + counts).astype(I32), (padded - counts).astype(I32),
                   n_used, p_rows)
        eid = jnp.arange(N_EXPERTS, dtype=I32)
        active = padded > 0
        run_idx = jnp.cumsum(active.astype(I32)) - 1
        later = active[None, :] & (eid[None, :] > eid[:, None])
        nxt = jnp.min(jnp.where(later, eid[None, :], N_EXPERTS), axis=1)
        nxt = jnp.where(nxt == N_EXPERTS, -1, nxt)
        onehot = (chunk_e[:, None] == eid[None, :]).astype(I32)
        run_slot = jnp.sum(onehot * (run_idx & 1)[None, :], axis=1).astype(I32)
        next_e = jnp.sum(onehot * nxt[None, :], axis=1).astype(I32)
        y4 = _moe(chunk_e, n_used, run_slot, next_e, inv, x1t, w_gu[i], b_gu[i][:, None, :],
                  w_down[i], b_down[i][:, None, :])
        out = _fin(y4, x1, w_top.T, p[i].reshape(n, D_PLE),
                   w_ple_gate[i].astype(BF16), w_ple_proj[i].astype(BF16),
                   ln2_g[i][None, :], ln2_b[i][None, :])
        x = out.reshape(bsz, seq, d)
    return x
```

```python
import math

import jax
import jax.numpy as jnp
from jax import lax
from jax.experimental import pallas as pl
from jax.experimental.pallas import tpu as pltpu

F32 = jnp.float32
BF16 = jnp.bfloat16
I32 = jnp.int32

SUBLANES = 8
LANES = 128
VMEM_LIMIT_BYTES = 56 * 1024 * 1024

D_MODEL = 1024
HEAD_DIM = 64
N_Q_HEADS = 16
N_KV_HEADS = 4
GQA_GROUP = N_Q_HEADS // N_KV_HEADS
KV_DIM = N_KV_HEADS * HEAD_DIM
WINDOW = 128
N_EXPERTS = 32
TOP_K = 4
D_EXPERT = D_MODEL
SWIGLU_LIMIT = 7.0
SWIGLU_ALPHA = 1.702
D_PLE = 256
DEPTH = 1
DN_ALPHA = (2.0 * DEPTH) ** 0.25
LN_EPS = 1e-5
ROW_TILES = D_MODEL // LANES

OFF_B, OFF_C, OFF_H, OFF_Q, OFF_GC, OFF_GA = (i * D_MODEL for i in range(6))
OFF_K = 6 * D_MODEL
OFF_V = OFF_K + KV_DIM
IN_TOTAL = OFF_V + KV_DIM

TS_MIX = 512
TS_ROUTE = 512
TN_DEST = 2048
TD_DISP = 1024
CHUNK = 512
TS_FIN = 256
ISSUE_UNROLL = 4


def _layer_norm(z, g, b):
    mu = jnp.mean(z, axis=-1, keepdims=True)
    zc = z - mu
    var = jnp.mean(zc * zc, axis=-1, keepdims=True)
    return zc * lax.rsqrt(var + LN_EPS) * g + b


def _dot(a, b):
    return jnp.dot(a, b, preferred_element_type=F32)


def _dot_nt(a, b):
    return lax.dot_general(a, b, (((1,), (1,)), ((), ())), preferred_element_type=F32)


def _mix_kernel(sinks_ref, x_ref, win_ref, rep_ref, convw_ref, wbrc_ref, wbra_ref, wout_ref,
                g1_ref, b1_ref, x1_ref, x1t_ref, kext, vext, uext):
    ts = TS_MIX
    j = pl.program_id(1)

    @pl.when(j == 0)
    def _():
        kext[0:WINDOW, :] = jnp.zeros((WINDOW, D_MODEL), BF16)
        vext[0:WINDOW, :] = jnp.zeros((WINDOW, D_MODEL), BF16)
        uext[0:SUBLANES, :] = jnp.zeros((SUBLANES, D_MODEL), F32)

    x = x_ref[0]
    xb = x.astype(BF16)

    def proj(off, width):
        return _dot(xb, win_ref[:, off:off + width])

    u = proj(OFF_C, D_MODEL) * proj(OFF_H, D_MODEL)
    uext[SUBLANES:SUBLANES + ts, :] = u
    y = (convw_ref[2:3, :] * u
         + convw_ref[1:2, :] * uext[SUBLANES - 1:SUBLANES - 1 + ts, :]
         + convw_ref[0:1, :] * uext[SUBLANES - 2:SUBLANES - 2 + ts, :])
    uext[0:SUBLANES, :] = u[ts - SUBLANES:ts, :]
    yc_in = proj(OFF_B, D_MODEL) * y
    y_conv = _dot(yc_in.astype(BF16), wbrc_ref[...])
    acc = jax.nn.sigmoid(proj(OFF_GC, D_MODEL)) * y_conv

    q = proj(OFF_Q, D_MODEL).astype(BF16)
    kb = proj(OFF_K, KV_DIM).astype(BF16)
    vb = proj(OFF_V, KV_DIM).astype(BF16)
    kext[WINDOW:WINDOW + ts, :] = _dot(kb, rep_ref[...]).astype(BF16)
    vext[WINDOW:WINDOW + ts, :] = _dot(vb, rep_ref[...]).astype(BF16)

    grp = GQA_GROUP * HEAD_DIM
    row = lax.broadcasted_iota(I32, (WINDOW, 2 * WINDOW), 0)
    col = lax.broadcasted_iota(I32, (WINDOW, 2 * WINDOW), 1)
    band = (col > row) & (col <= row + WINDOW)
    q_lane_grp = lax.broadcasted_iota(I32, (WINDOW, grp), 1) // HEAD_DIM
    scale = 1.0 / math.sqrt(HEAD_DIM)
    o_blocks = []
    for i in range(ts // WINDOW):
        if i == 0:
            mask = band & ((col >= WINDOW) | (j > 0))
        else:
            mask = band
        o_heads = []
        for h in range(N_KV_HEADS):
            qh = q[i * WINDOW:(i + 1) * WINDOW, h * grp:(h + 1) * grp]
            kh = kext[i * WINDOW:i * WINDOW + 2 * WINDOW, h * grp:(h + 1) * grp]
            vh = vext[i * WINDOW:i * WINDOW + 2 * WINDOW, h * grp:(h + 1) * grp]
            qs = jnp.concatenate(
                [jnp.where(q_lane_grp == g, qh, jnp.zeros_like(qh)) for g in range(GQA_GROUP)], axis=0)
            s_all = _dot_nt(qs, kh) * scale
            ps = []
            for g in range(GQA_GROUP):
                s = jnp.where(mask, s_all[g * WINDOW:(g + 1) * WINDOW], -jnp.inf)
                sink = sinks_ref[h * GQA_GROUP + g]
                m = jnp.maximum(jnp.max(s, axis=-1, keepdims=True), sink)
                e = jnp.exp(s - m)
                den = jnp.sum(e, axis=-1, keepdims=True) + jnp.exp(sink - m)
                ps.append((e / den).astype(BF16))
            pv = _dot(jnp.concatenate(ps, axis=0), vh)
            oh = jnp.zeros((WINDOW, grp), F32)
            for g in range(GQA_GROUP):
                oh = jnp.where(q_lane_grp == g, pv[g * WINDOW:(g + 1) * WINDOW], oh)
            o_heads.append(oh)
        o_blocks.append(jnp.concatenate(o_heads, axis=-1))
    o = jnp.concatenate(o_blocks, axis=0)
    kext[0:WINDOW, :] = kext[ts:ts + WINDOW, :]
    vext[0:WINDOW, :] = vext[ts:ts + WINDOW, :]

    y_attn = _dot(o.astype(BF16), wbra_ref[...])
    acc = acc + jax.nn.sigmoid(proj(OFF_GA, D_MODEL)) * y_attn

    z = DN_ALPHA * x + _dot(acc.astype(BF16), wout_ref[...])
    x1 = _layer_norm(z, g1_ref[...], b1_ref[...])
    x1_ref[0] = x1
    for s in range(ROW_TILES):
        x1t_ref[pl.ds(s, ts, stride=ROW_TILES), :] = x1[:, s * LANES:(s + 1) * LANES]


def _mix(x, w_in_p, rep, conv_w, wbrc, wbra, sinks, wout, g1, b1):
    bsz, seq, d = x.shape
    n = bsz * seq
    nj = seq // TS_MIX
    const = lambda shape: pl.BlockSpec(shape, lambda b, j, s: (0,) * len(shape),
                                       pipeline_mode=pl.Buffered(1))
    grid_spec = pltpu.PrefetchScalarGridSpec(
        num_scalar_prefetch=1,
        grid=(bsz, nj),
        in_specs=[
            pl.BlockSpec((1, TS_MIX, d), lambda b, j, s: (b, j, 0)),
            const((d, IN_TOTAL)),
            const((KV_DIM, d)),
            const((3, d)),
            const((d, d)),
            const((d, d)),
            const((d, d)),
            const((1, d)),
            const((1, d)),
        ],
        out_specs=[
            pl.BlockSpec((1, TS_MIX, d), lambda b, j, s: (b, j, 0)),
            pl.BlockSpec((TS_MIX * ROW_TILES, LANES), lambda b, j, s: (b * nj + j, 0)),
        ],
        scratch_shapes=[
            pltpu.VMEM((WINDOW + TS_MIX, d), BF16),
            pltpu.VMEM((WINDOW + TS_MIX, d), BF16),
            pltpu.VMEM((SUBLANES + TS_MIX, d), F32),
        ],
    )
    return pl.pallas_call(
        _mix_kernel,
        grid_spec=grid_spec,
        out_shape=[jax.ShapeDtypeStruct((bsz, seq, d), F32),
                   jax.ShapeDtypeStruct((n * ROW_TILES, LANES), F32)],
        compiler_params=pltpu.CompilerParams(
            dimension_semantics=("arbitrary", "arbitrary"),
            vmem_limit_bytes=VMEM_LIMIT_BYTES),
        name="mix",
    )(sinks, x, w_in_p, rep, conv_w, wbrc, wbra, wout, g1, b1)


def _route_kernel(x1_ref, whi_ref, wlo_ref, br_ref, idx_ref, w_ref, rank_ref, cnt_ref, carry):
    ts = TS_ROUTE
    i = pl.program_id(0)

    @pl.when(i == 0)
    def _():
        carry[...] = jnp.zeros_like(carry)

    x = x1_ref[...]
    xh = x.astype(BF16)
    xl = (x - xh.astype(F32)).astype(BF16)
    whi = whi_ref[...]
    logits = _dot_nt(whi, xh) + _dot_nt(whi, xl) + _dot_nt(wlo_ref[...], xh) + br_ref[...]

    eid = lax.broadcasted_iota(I32, (N_EXPERTS, ts), 0)
    rest = logits
    sels, vals = [], []
    for k in range(TOP_K):
        m = jnp.max(rest, axis=0, keepdims=True)
        idx = jnp.min(jnp.where(rest == m, eid, N_EXPERTS), axis=0, keepdims=True)
        sel = eid == idx
        rest = jnp.where(sel, -jnp.inf, rest)
        sels.append(sel)
        vals.append(m)
        idx_ref[k:k + 1, :] = idx
    exps = [jnp.exp(v - vals[0]) for v in vals]
    den = exps[0] + exps[1] + exps[2] + exps[3]
    for k in range(TOP_K):
        w_ref[k:k + 1, :] = exps[k] / den

    member = jnp.zeros((N_EXPERTS, ts), F32)
    for sel in sels:
        member = member + sel.astype(F32)
    r = lax.broadcasted_iota(I32, (ts, ts), 0)
    c = lax.broadcasted_iota(I32, (ts, ts), 1)
    upper = (r < c).astype(BF16)
    cum = _dot(member.astype(BF16), upper) + carry[:, 0:1]
    for k in range(TOP_K):
        rk = jnp.sum(jnp.where(sels[k], cum, 0.0), axis=0, keepdims=True)
        rank_ref[k:k + 1, :] = rk.astype(I32)
    carry[...] = carry[...] + jnp.sum(member, axis=1, keepdims=True)
    cnt_ref[...] = carry[...].astype(I32)


def _route(x1, whi, wlo, br):
    n, d = x1.shape
    const = lambda shape: pl.BlockSpec(shape, lambda i: (0,) * len(shape))
    tok = pl.BlockSpec((TOP_K, TS_ROUTE), lambda i: (0, i))
    return pl.pallas_call(
        _route_kernel,
        grid=(n // TS_ROUTE,),
        in_specs=[pl.BlockSpec((TS_ROUTE, d), lambda i: (i, 0)),
                  const((N_EXPERTS, d)), const((N_EXPERTS, d)), const((N_EXPERTS, 1))],
        out_specs=[tok, tok, tok, const((N_EXPERTS, LANES))],
        out_shape=[jax.ShapeDtypeStruct((TOP_K, n), I32),
                   jax.ShapeDtypeStruct((TOP_K, n), F32),
                   jax.ShapeDtypeStruct((TOP_K, n), I32),
                   jax.ShapeDtypeStruct((N_EXPERTS, LANES), I32)],
        scratch_shapes=[pltpu.VMEM((N_EXPERTS, LANES), F32)],
        compiler_params=pltpu.CompilerParams(dimension_semantics=("arbitrary",)),
        name="route",
    )(x1, whi, wlo, br)


def _dest_kernel(start_ref, idx_ref, rank_ref, dest_ref):
    idx = idx_ref[...]
    acc = rank_ref[...]
    for e in range(N_EXPERTS):
        acc = acc + jnp.where(idx == e, start_ref[e], 0)
    dest_ref[...] = acc


def _dest(start_padded, idx, rank):
    k, n = idx.shape
    tok = pl.BlockSpec((k, TN_DEST), lambda i, s: (0, i))
    return pl.pallas_call(
        _dest_kernel,
        grid_spec=pltpu.PrefetchScalarGridSpec(
            num_scalar_prefetch=1, grid=(n // TN_DEST,), in_specs=[tok, tok], out_specs=tok),
        out_shape=jax.ShapeDtypeStruct((k, n), I32),
        name="dest",
    )(start_padded, idx, rank)


def _row_copy(src, src_row, dst, dst_row, rows, sem):
    s0 = pl.multiple_of(src_row * ROW_TILES, ROW_TILES)
    d0 = pl.multiple_of(dst_row * ROW_TILES, ROW_TILES)
    return pltpu.make_async_copy(src.at[pl.ds(s0, rows * ROW_TILES)],
                                 dst.at[pl.ds(d0, rows * ROW_TILES)], sem)


def _pad_pieces():
    return [1 << b for b in reversed(range(CHUNK.bit_length() - 1))]


def _tile_index(idx_ref, tile_tokens, k, t):
    per = tile_tokens // LANES
    return idx_ref[0, k * per + t // LANES, t % LANES]


def _disp_kernel(padpos_ref, padlen_ref, nu_ref, didx_ref, x1t_ref, xs_ref, zbuf, sem, zsem):
    half = CHUNK // 2
    n_half = xs_ref.shape[0] // (half * ROW_TILES)
    i = pl.program_id(0)

    def issue(tb, carry):
        for u in range(ISSUE_UNROLL):
            t = tb * ISSUE_UNROLL + u
            for k in range(TOP_K):
                _row_copy(x1t_ref, t, xs_ref, _tile_index(didx_ref, TD_DISP, k, t), 1,
                          sem).start(priority=k % 2)
        return carry

    lax.fori_loop(0, TD_DISP // ISSUE_UNROLL, issue, 0)

    @pl.when(i == 0)
    def _():
        zbuf[...] = jnp.zeros_like(zbuf)
        for wait in (False, True):
            for e in range(N_EXPERTS):
                npad = padlen_ref[e]
                for piece in _pad_pieces():
                    pos = padpos_ref[e] + (npad & ~(2 * piece - 1))
                    cp = _row_copy(zbuf, 0, xs_ref, pos, piece, zsem)

                    @pl.when((npad & piece) != 0)
                    def _():
                        cp.wait() if wait else cp.start()

            def tail(hc, carry):
                cp = _row_copy(zbuf, 0, xs_ref, hc * half, half, zsem)
                cp.wait() if wait else cp.start()
                return carry

            lax.fori_loop(2 * nu_ref[0], n_half, tail, 0)

    for k in range(TOP_K):
        _row_copy(x1t_ref, 0, xs_ref, 0, TD_DISP, sem).wait()


def _tiled_index(dest, tile_tokens):
    k, n = dest.shape
    nt = n // tile_tokens
    return dest.reshape(k, nt, tile_tokens).transpose(1, 0, 2).reshape(nt, k * tile_tokens // LANES, LANES)


def _disp(dest, padpos, padlen, n_used, x1t, p_rows):
    n = dest.shape[1]
    didx = _tiled_index(dest, TD_DISP)
    anyspec = pl.BlockSpec(memory_space=pl.ANY)
    return pl.pallas_call(
        _disp_kernel,
        grid_spec=pltpu.PrefetchScalarGridSpec(
            num_scalar_prefetch=3, grid=(n // TD_DISP,),
            in_specs=[pl.BlockSpec((1,) + didx.shape[1:], lambda i, *_: (i, 0, 0),
                                   memory_space=pltpu.SMEM),
                      pl.BlockSpec((TD_DISP * ROW_TILES, LANES), lambda i, *_: (i, 0))],
            out_specs=anyspec,
            scratch_shapes=[pltpu.VMEM((CHUNK // 2 * ROW_TILES, LANES), F32),
                            pltpu.SemaphoreType.DMA, pltpu.SemaphoreType.DMA]),
        out_shape=jax.ShapeDtypeStruct((p_rows * ROW_TILES, LANES), F32),
        compiler_params=pltpu.CompilerParams(dimension_semantics=("arbitrary",),
                                             has_side_effects=True),
        name="disp",
    )(padpos, padlen, n_used, didx, x1t)


def _moe_kernel(ce_ref, nu_ref, slot_ref, nxt_ref, xs_ref, wgu_hbm, bgu_ref, wd_hbm, bd_ref, ys_ref,
                wgu_f32, wd_f32, wgu_bf, wd_bf, wsems):
    c = pl.program_id(0)
    used = c < nu_ref[0]

    def weight_copies(e, slot):
        return (pltpu.make_async_copy(wgu_hbm.at[e], wgu_f32.at[slot], wsems.at[0, slot]),
                pltpu.make_async_copy(wd_hbm.at[e], wd_f32.at[slot], wsems.at[1, slot]))

    @pl.when(used & ((c == 0) | (ce_ref[c] != ce_ref[jnp.maximum(c - 1, 0)])))
    def _():
        slot = slot_ref[c]

        @pl.when(c == 0)
        def _():
            for cp in weight_copies(ce_ref[0], slot):
                cp.start()

        for cp in weight_copies(ce_ref[c], slot):
            cp.wait()

        @pl.when(nxt_ref[c] >= 0)
        def _():
            for cp in weight_copies(nxt_ref[c], 1 - slot):
                cp.start()

        wgu_bf[...] = wgu_f32[slot].astype(BF16)
        wd_bf[...] = wd_f32[slot].astype(BF16)

    @pl.when(used)
    def _():
        x = jnp.concatenate(
            [xs_ref[pl.ds(s, CHUNK, stride=ROW_TILES), :] for s in range(ROW_TILES)], axis=-1)
        gu = _dot(x.astype(BF16), wgu_bf[...]) + bgu_ref[0]
        gate = jnp.minimum(gu[:, :D_EXPERT], SWIGLU_LIMIT)
        up = jnp.clip(gu[:, D_EXPERT:], -SWIGLU_LIMIT, SWIGLU_LIMIT)
        h = (up + 1.0) * gate * jax.nn.sigmoid(SWIGLU_ALPHA * gate)
        y = _dot(h.astype(BF16), wd_bf[...]) + bd_ref[0]
        for s in range(ROW_TILES):
            ys_ref[pl.ds(s, CHUNK, stride=ROW_TILES), :] = y[:, s * LANES:(s + 1) * LANES]

    @pl.when(c >= nu_ref[0])
    def _():
        ys_ref[...] = jnp.zeros_like(ys_ref)


def _moe(chunk_e, n_used, run_slot, next_e, xs, wgu, bgu, wd, bd):
    n_chunks = xs.shape[0] // (CHUNK * ROW_TILES)
    d = D_MODEL

    def cc(c, ce, nu, *_):
        return jnp.minimum(c, nu[0] - 1)

    rows = pl.BlockSpec((CHUNK * ROW_TILES, LANES), lambda c, *s: (cc(c, *s), 0))
    bspec = lambda shape: pl.BlockSpec(shape, lambda c, *s: (s[0][cc(c, *s)], 0, 0))
    anyspec = pl.BlockSpec(memory_space=pl.ANY)
    return pl.pallas_call(
        _moe_kernel,
        grid_spec=pltpu.PrefetchScalarGridSpec(
            num_scalar_prefetch=4, grid=(n_chunks,),
            in_specs=[rows, anyspec, bspec((1, 1, 2 * D_EXPERT)), anyspec, bspec((1, 1, d))],
            out_specs=pl.BlockSpec((CHUNK * ROW_TILES, LANES), lambda c, *s: (c, 0)),
            scratch_shapes=[pltpu.VMEM((2, d, 2 * D_EXPERT), F32), pltpu.VMEM((2, D_EXPERT, d), F32),
                            pltpu.VMEM((d, 2 * D_EXPERT), BF16), pltpu.VMEM((D_EXPERT, d), BF16),
                            pltpu.SemaphoreType.DMA((2, 2))]),
        out_shape=jax.ShapeDtypeStruct(xs.shape, F32),
        compiler_params=pltpu.CompilerParams(dimension_semantics=("arbitrary",),
                                             vmem_limit_bytes=VMEM_LIMIT_BYTES),
        name="moe",
    )(chunk_e, n_used, run_slot, next_e, xs, wgu, bgu, wd, bd)


def _fin_kernel(dcur_ref, dnxt_ref, ys_ref, x1_ref, w_ref, p_ref, wpg_ref, wpp_ref, g2_ref, b2_ref,
                out_ref, rows_a, rows_b, sems):
    ts = TS_FIN
    i = pl.program_id(0)
    last = pl.num_programs(0) - 1
    bufs = (rows_a, rows_b)

    def gather_row(idx_ref, par, k, t):
        _row_copy(ys_ref, _tile_index(idx_ref, ts, k, t), bufs[par].at[k], t, 1,
                  sems.at[par]).start(priority=k % 2)

    def wait_rows(par):
        for k in range(TOP_K):
            _row_copy(ys_ref, 0, bufs[par].at[k], 0, ts, sems.at[par]).wait()

    @pl.when(i == 0)
    def _():
        def issue(tb, carry):
            for u in range(ISSUE_UNROLL):
                for k in range(TOP_K):
                    gather_row(dcur_ref, 0, k, tb * ISSUE_UNROLL + u)
            return carry

        lax.fori_loop(0, ts // ISSUE_UNROLL, issue, 0)

    for par in range(2):
        @pl.when(i % 2 == par)
        def _():
            wait_rows(par)
            for t in range(ts):
                for k in range(TOP_K):
                    gather_row(dnxt_ref, 1 - par, k, t)

            x1 = x1_ref[...]
            ple = (jax.nn.sigmoid(_dot(x1.astype(BF16), wpg_ref[...]))
                   * _dot(p_ref[...].astype(BF16), wpp_ref[...]))
            z = DN_ALPHA * x1 + ple
            w = w_ref[...]
            for k in range(TOP_K):
                yk = jnp.concatenate(
                    [bufs[par][k, pl.ds(s, ts, stride=ROW_TILES), :] for s in range(ROW_TILES)],
                    axis=-1)
                z = z + w[:, k:k + 1] * yk
            out_ref[...] = _layer_norm(z, g2_ref[...], b2_ref[...])

            @pl.when(i == last)
            def _():
                wait_rows(1 - par)


def _fin(dest, ys, x1, w_rows, p2, wpg, wpp, g2, b2):
    n, d = x1.shape
    nt = n // TS_FIN
    didx = _tiled_index(dest, TS_FIN)
    const = lambda shape: pl.BlockSpec(shape, lambda i: (0,) * len(shape))
    tile = lambda width: pl.BlockSpec((TS_FIN, width), lambda i: (i, 0))
    ispec = lambda tile_of: pl.BlockSpec((1,) + didx.shape[1:], lambda i: (tile_of(i), 0, 0),
                                         memory_space=pltpu.SMEM)
    return pl.pallas_call(
        _fin_kernel,
        grid=(nt,),
        in_specs=[ispec(lambda i: i), ispec(lambda i: jnp.minimum(i + 1, nt - 1)),
                  pl.BlockSpec(memory_space=pl.ANY), tile(d), tile(TOP_K), tile(D_PLE),
                  const((d, d)), const((D_PLE, d)), const((1, d)), const((1, d))],
        out_specs=tile(d),
        scratch_shapes=[pltpu.VMEM((TOP_K, TS_FIN * ROW_TILES, LANES), F32),
                        pltpu.VMEM((TOP_K, TS_FIN * ROW_TILES, LANES), F32),
                        pltpu.SemaphoreType.DMA((2,))],
        out_shape=jax.ShapeDtypeStruct((n, d), F32),
        compiler_params=pltpu.CompilerParams(dimension_semantics=("arbitrary",),
                                             vmem_limit_bytes=VMEM_LIMIT_BYTES),
        name="fin",
    )(didx, didx, ys, x1, w_rows, p2, wpg, wpp, g2, b2)


def _permute_w_in(w_in):
    d = D_MODEL
    b_, c_, h_, q_ = (w_in[:, i * d:(i + 1) * d] for i in range(4))
    k_ = w_in[:, 4 * d:4 * d + KV_DIM]
    v_ = w_in[:, 4 * d + KV_DIM:4 * d + 2 * KV_DIM]
    gc_ = w_in[:, 4 * d + 2 * KV_DIM:5 * d + 2 * KV_DIM]
    ga_ = w_in[:, 5 * d + 2 * KV_DIM:]
    return jnp.concatenate([b_, c_, h_, q_, gc_, ga_, k_, v_], axis=1).astype(BF16)


def _replication_matrix():
    src = jnp.arange(KV_DIM)[:, None]
    dst = jnp.arange(D_MODEL)[None, :]
    same_head = (dst // (GQA_GROUP * HEAD_DIM)) == (src // HEAD_DIM)
    same_dim = (dst % HEAD_DIM) == (src % HEAD_DIM)
    return (same_head & same_dim).astype(BF16)


def kernel(x, p, w_in, conv_w, w_br_conv, w_br_attn, attn_sinks, w_out, ln1_g, ln1_b, w_router,
           b_router, w_gu, b_gu, w_down, b_down, w_ple_proj, w_ple_gate, ln2_g, ln2_b):
    bsz, seq, d = x.shape
    n = bsz * seq
    for i in range(DEPTH):
        x1, x1t = _mix(x, _permute_w_in(w_in[i]), _replication_matrix(), conv_w[i],
                       w_br_conv[i].astype(BF16), w_br_attn[i].astype(BF16),
                       attn_sinks[i], w_out[i].astype(BF16),
                       ln1_g[i][None, :], ln1_b[i][None, :])
        x1 = x1.reshape(n, d)

        wr_t = w_router[i].T
        wr_hi = wr_t.astype(BF16)
        wr_lo = (wr_t - wr_hi.astype(F32)).astype(BF16)
        idx, w_top, rank, cnt = _route(x1, wr_hi, wr_lo, b_router[i][:, None])

        counts = cnt[:, 0]
        padded = ((counts + CHUNK - 1) // CHUNK) * CHUNK
        end_padded = jnp.cumsum(padded)
        start_padded = end_padded - padded
        p_rows = n * TOP_K + N_EXPERTS * CHUNK
        n_chunks = p_rows // CHUNK
        chunk_start = jnp.arange(n_chunks, dtype=I32) * CHUNK
        chunk_e = jnp.minimum(
            jnp.sum((end_padded[None, :] <= chunk_start[:, None]).astype(I32), axis=1),
            N_EXPERTS - 1)
        n_used = (end_padded[-1:] // CHUNK).astype(I32)

        dest = _dest(start_padded.astype(I32), idx, rank)
        xs = _disp(dest, (start_padded + counts).astype(I32), (padded - counts).astype(I32),
                   n_used, x1t, p_rows)
        eid = jnp.arange(N_EXPERTS, dtype=I32)
        active = padded > 0
        run_idx = jnp.cumsum(active.astype(I32)) - 1
        later = active[None, :] & (eid[None, :] > eid[:, None])
        nxt = jnp.min(jnp.where(later, eid[None, :], N_EXPERTS), axis=1)
        nxt = jnp.where(nxt == N_EXPERTS, -1, nxt)
        onehot = (chunk_e[:, None] == eid[None, :]).astype(I32)
        run_slot = jnp.sum(onehot * (run_idx & 1)[None, :], axis=1).astype(I32)
        next_e = jnp.sum(onehot * nxt[None, :], axis=1).astype(I32)
        ys = _moe(chunk_e, n_used, run_slot, next_e, xs, w_gu[i], b_gu[i][:, None, :],
                  w_down[i], b_down[i][:, None, :])
        out = _fin(dest, ys, x1, w_top.T, p[i].reshape(n, D_PLE),
                   w_ple_gate[i].astype(BF16), w_ple_proj[i].astype(BF16),
                   ln2_g[i][None, :], ln2_b[i][None, :])
        x = out.reshape(bsz, seq, d)
    return x
```

```python
import math

import jax
import jax.numpy as jnp
from jax import lax
from jax.experimental import pallas as pl
from jax.experimental.pallas import tpu as pltpu

F32 = jnp.float32
BF16 = jnp.bfloat16
I32 = jnp.int32

SUBLANES = 8
LANES = 128
VMEM_LIMIT_BYTES = 56 * 1024 * 1024

D_MODEL = 1024
HEAD_DIM = 64
N_Q_HEADS = 16
N_KV_HEADS = 4
GQA_GROUP = N_Q_HEADS // N_KV_HEADS
KV_DIM = N_KV_HEADS * HEAD_DIM
WINDOW = 128
N_EXPERTS = 32
TOP_K = 4
D_EXPERT = D_MODEL
SWIGLU_LIMIT = 7.0
SWIGLU_ALPHA = 1.702
D_PLE = 256
DEPTH = 1
DN_ALPHA = (2.0 * DEPTH) ** 0.25
LN_EPS = 1e-5
ROW_TILES = D_MODEL // LANES

OFF_B, OFF_C, OFF_H, OFF_Q, OFF_GC, OFF_GA = (i * D_MODEL for i in range(6))
OFF_K = 6 * D_MODEL
OFF_V = OFF_K + KV_DIM
IN_TOTAL = OFF_V + KV_DIM

TS_MIX = 512
TS_ROUTE = 512
TN_DEST = 2048
TD_DISP = 1024
CHUNK = 512
TS_FIN = 256
ISSUE_UNROLL = 4


def _layer_norm(z, g, b):
    mu = jnp.mean(z, axis=-1, keepdims=True)
    zc = z - mu
    var = jnp.mean(zc * zc, axis=-1, keepdims=True)
    return zc * lax.rsqrt(var + LN_EPS) * g + b


def _dot(a, b):
    return jnp.dot(a, b, preferred_element_type=F32)


def _dot_nt(a, b):
    return lax.dot_general(a, b, (((1,), (1,)), ((), ())), preferred_element_type=F32)


def _mix_kernel(sinks_ref, x_ref, win_ref, rep_ref, convw_ref, wbrc_ref, wbra_ref, wout_ref,
                g1_ref, b1_ref, x1_ref, x1t_ref, kext, vext, uext):
    ts = TS_MIX
    j = pl.program_id(1)

    @pl.when(j == 0)
    def _():
        kext[0:WINDOW, :] = jnp.zeros((WINDOW, D_MODEL), BF16)
        vext[0:WINDOW, :] = jnp.zeros((WINDOW, D_MODEL), BF16)
        uext[0:SUBLANES, :] = jnp.zeros((SUBLANES, D_MODEL), F32)

    x = x_ref[0]
    xb = x.astype(BF16)

    def proj(off, width):
        return _dot(xb, win_ref[:, off:off + width])

    u = proj(OFF_C, D_MODEL) * proj(OFF_H, D_MODEL)
    uext[SUBLANES:SUBLANES + ts, :] = u
    y = (convw_ref[2:3, :] * u
         + convw_ref[1:2, :] * uext[SUBLANES - 1:SUBLANES - 1 + ts, :]
         + convw_ref[0:1, :] * uext[SUBLANES - 2:SUBLANES - 2 + ts, :])
    uext[0:SUBLANES, :] = u[ts - SUBLANES:ts, :]
    yc_in = proj(OFF_B, D_MODEL) * y
    y_conv = _dot(yc_in.astype(BF16), wbrc_ref[...])
    acc = jax.nn.sigmoid(proj(OFF_GC, D_MODEL)) * y_conv

    q = proj(OFF_Q, D_MODEL).astype(BF16)
    kb = proj(OFF_K, KV_DIM).astype(BF16)
    vb = proj(OFF_V, KV_DIM).astype(BF16)
    kext[WINDOW:WINDOW + ts, :] = _dot(kb, rep_ref[...]).astype(BF16)
    vext[WINDOW:WINDOW + ts, :] = _dot(vb, rep_ref[...]).astype(BF16)

    grp = GQA_GROUP * HEAD_DIM
    row = lax.broadcasted_iota(I32, (WINDOW, 2 * WINDOW), 0)
    col = lax.broadcasted_iota(I32, (WINDOW, 2 * WINDOW), 1)
    band = (col > row) & (col <= row + WINDOW)
    q_lane_grp = lax.broadcasted_iota(I32, (WINDOW, grp), 1) // HEAD_DIM
    scale = 1.0 / math.sqrt(HEAD_DIM)
    o_blocks = []
    for i in range(ts // WINDOW):
        if i == 0:
            mask = band & ((col >= WINDOW) | (j > 0))
        else:
            mask = band
        o_heads = []
        for h in range(N_KV_HEADS):
            qh = q[i * WINDOW:(i + 1) * WINDOW, h * grp:(h + 1) * grp]
            kh = kext[i * WINDOW:i * WINDOW + 2 * WINDOW, h * grp:(h + 1) * grp]
            vh = vext[i * WINDOW:i * WINDOW + 2 * WINDOW, h * grp:(h + 1) * grp]
            qs = jnp.concatenate(
                [jnp.where(q_lane_grp == g, qh, jnp.zeros_like(qh)) for g in range(GQA_GROUP)], axis=0)
            s_all = _dot_nt(qs, kh) * scale
            ps = []
            for g in range(GQA_GROUP):
                s = jnp.where(mask, s_all[g * WINDOW:(g + 1) * WINDOW], -jnp.inf)
                sink = sinks_ref[h * GQA_GROUP + g]
                m = jnp.maximum(jnp.max(s, axis=-1, keepdims=True), sink)
                e = jnp.exp(s - m)
                den = jnp.sum(e, axis=-1, keepdims=True) + jnp.exp(sink - m)
                ps.append((e / den).astype(BF16))
            pv = _dot(jnp.concatenate(ps, axis=0), vh)
            oh = jnp.zeros((WINDOW, grp), F32)
            for g in range(GQA_GROUP):
                oh = jnp.where(q_lane_grp == g, pv[g * WINDOW:(g + 1) * WINDOW], oh)
            o_heads.append(oh)
        o_blocks.append(jnp.concatenate(o_heads, axis=-1))
    o = jnp.concatenate(o_blocks, axis=0)
    kext[0:WINDOW, :] = kext[ts:ts + WINDOW, :]
    vext[0:WINDOW, :] = vext[ts:ts + WINDOW, :]

    y_attn = _dot(o.astype(BF16), wbra_ref[...])
    acc = acc + jax.nn.sigmoid(proj(OFF_GA, D_MODEL)) * y_attn

    z = DN_ALPHA * x + _dot(acc.astype(BF16), wout_ref[...])
    x1 = _layer_norm(z, g1_ref[...], b1_ref[...])
    x1_ref[0] = x1
    for s in range(ROW_TILES):
        x1t_ref[pl.ds(s, ts, stride=ROW_TILES), :] = x1[:, s * LANES:(s + 1) * LANES]


def _mix(x, w_in_p, rep, conv_w, wbrc, wbra, sinks, wout, g1, b1):
    bsz, seq, d = x.shape
    n = bsz * seq
    nj = seq // TS_MIX
    const = lambda shape: pl.BlockSpec(shape, lambda b, j, s: (0,) * len(shape),
                                       pipeline_mode=pl.Buffered(1))
    grid_spec = pltpu.PrefetchScalarGridSpec(
        num_scalar_prefetch=1,
        grid=(bsz, nj),
        in_specs=[
            pl.BlockSpec((1, TS_MIX, d), lambda b, j, s: (b, j, 0)),
            const((d, IN_TOTAL)),
            const((KV_DIM, d)),
            const((3, d)),
            const((d, d)),
            const((d, d)),
            const((d, d)),
            const((1, d)),
            const((1, d)),
        ],
        out_specs=[
            pl.BlockSpec((1, TS_MIX, d), lambda b, j, s: (b, j, 0)),
            pl.BlockSpec((TS_MIX * ROW_TILES, LANES), lambda b, j, s: (b * nj + j, 0)),
        ],
        scratch_shapes=[
            pltpu.VMEM((WINDOW + TS_MIX, d), BF16),
            pltpu.VMEM((WINDOW + TS_MIX, d), BF16),
            pltpu.VMEM((SUBLANES + TS_MIX, d), F32),
        ],
    )
    return pl.pallas_call(
        _mix_kernel,
        grid_spec=grid_spec,
        out_shape=[jax.ShapeDtypeStruct((bsz, seq, d), F32),
                   jax.ShapeDtypeStruct((n * ROW_TILES, LANES), F32)],
        compiler_params=pltpu.CompilerParams(
            dimension_semantics=("arbitrary", "arbitrary"),
            vmem_limit_bytes=VMEM_LIMIT_BYTES),
        name="mix",
    )(sinks, x, w_in_p, rep, conv_w, wbrc, wbra, wout, g1, b1)


def _route_kernel(x1_ref, whi_ref, wlo_ref, br_ref, idx_ref, w_ref, rank_ref, cnt_ref, carry):
    ts = TS_ROUTE
    i = pl.program_id(0)

    @pl.when(i == 0)
    def _():
        carry[...] = jnp.zeros_like(carry)

    x = x1_ref[...]
    xh = x.astype(BF16)
    xl = (x - xh.astype(F32)).astype(BF16)
    whi = whi_ref[...]
    logits = _dot_nt(whi, xh) + _dot_nt(whi, xl) + _dot_nt(wlo_ref[...], xh) + br_ref[...]

    eid = lax.broadcasted_iota(I32, (N_EXPERTS, ts), 0)
    rest = logits
    sels, vals = [], []
    for k in range(TOP_K):
        m = jnp.max(rest, axis=0, keepdims=True)
        idx = jnp.min(jnp.where(rest == m, eid, N_EXPERTS), axis=0, keepdims=True)
        sel = eid == idx
        rest = jnp.where(sel, -jnp.inf, rest)
        sels.append(sel)
        vals.append(m)
        idx_ref[k:k + 1, :] = idx
    exps = [jnp.exp(v - vals[0]) for v in vals]
    den = exps[0] + exps[1] + exps[2] + exps[3]
    for k in range(TOP_K):
        w_ref[k:k + 1, :] = exps[k] / den

    member = jnp.zeros((N_EXPERTS, ts), F32)
    for sel in sels:
        member = member + sel.astype(F32)
    r = lax.broadcasted_iota(I32, (ts, ts), 0)
    c = lax.broadcasted_iota(I32, (ts, ts), 1)
    upper = (r < c).astype(BF16)
    cum = _dot(member.astype(BF16), upper) + carry[:, 0:1]
    for k in range(TOP_K):
        rk = jnp.sum(jnp.where(sels[k], cum, 0.0), axis=0, keepdims=True)
        rank_ref[k:k + 1, :] = rk.astype(I32)
    carry[...] = carry[...] + jnp.sum(member, axis=1, keepdims=True)
    cnt_ref[...] = carry[...].astype(I32)


def _route(x1, whi, wlo, br):
    n, d = x1.shape
    const = lambda shape: pl.BlockSpec(shape, lambda i: (0,) * len(shape))
    tok = pl.BlockSpec((TOP_K, TS_ROUTE), lambda i: (0, i))
    return pl.pallas_call(
        _route_kernel,
        grid=(n // TS_ROUTE,),
        in_specs=[pl.BlockSpec((TS_ROUTE, d), lambda i: (i, 0)),
                  const((N_EXPERTS, d)), const((N_EXPERTS, d)), const((N_EXPERTS, 1))],
        out_specs=[tok, tok, tok, const((N_EXPERTS, LANES))],
        out_shape=[jax.ShapeDtypeStruct((TOP_K, n), I32),
                   jax.ShapeDtypeStruct((TOP_K, n), F32),
                   jax.ShapeDtypeStruct((TOP_K, n), I32),
                   jax.ShapeDtypeStruct((N_EXPERTS, LANES), I32)],
        scratch_shapes=[pltpu.VMEM((N_EXPERTS, LANES), F32)],
        compiler_params=pltpu.CompilerParams(dimension_semantics=("arbitrary",)),
        name="route",
    )(x1, whi, wlo, br)


def _dest_kernel(start_ref, idx_ref, rank_ref, dest_ref):
    idx = idx_ref[...]
    acc = rank_ref[...]
    for e in range(N_EXPERTS):
        acc = acc + jnp.where(idx == e, start_ref[e], 0)
    dest_ref[...] = acc


def _dest(start_padded, idx, rank):
    k, n = idx.shape
    tok = pl.BlockSpec((k, TN_DEST), lambda i, s: (0, i))
    return pl.pallas_call(
        _dest_kernel,
        grid_spec=pltpu.PrefetchScalarGridSpec(
            num_scalar_prefetch=1, grid=(n // TN_DEST,), in_specs=[tok, tok], out_specs=tok),
        out_shape=jax.ShapeDtypeStruct((k, n), I32),
        name="dest",
    )(start_padded, idx, rank)


def _row_copy(src, src_row, dst, dst_row, rows, sem):
    s0 = pl.multiple_of(src_row * ROW_TILES, ROW_TILES)
    d0 = pl.multiple_of(dst_row * ROW_TILES, ROW_TILES)
    return pltpu.make_async_copy(src.at[pl.ds(s0, rows * ROW_TILES)],
                                 dst.at[pl.ds(d0, rows * ROW_TILES)], sem)


def _pad_pieces():
    return [1 << b for b in reversed(range(CHUNK.bit_length() - 1))]


def _index_copy(idx_hbm, tile, idx_smem, slot, isems, tile_tokens):
    n_idx = TOP_K * tile_tokens
    last = idx_hbm.shape[0] // n_idx - 1
    src = pl.multiple_of(jnp.minimum(tile, last) * n_idx, n_idx)
    return pltpu.make_async_copy(idx_hbm.at[pl.ds(src, n_idx)],
                                 idx_smem.at[pl.ds(slot * n_idx, n_idx)], isems.at[slot])


def _disp_kernel(padpos_ref, padlen_ref, nu_ref, didx_hbm, x1t_ref, xs_ref, zbuf, idx_smem, sem, zsem,
                 isems):
    half = CHUNK // 2
    n_half = xs_ref.shape[0] // (half * ROW_TILES)
    i = pl.program_id(0)
    slot = i % 2

    @pl.when(i == 0)
    def _():
        _index_copy(didx_hbm, 0, idx_smem, 0, isems, TD_DISP).start()

    _index_copy(didx_hbm, i, idx_smem, slot, isems, TD_DISP).wait()
    _index_copy(didx_hbm, i + 1, idx_smem, 1 - slot, isems, TD_DISP).start()
    base = slot * (TOP_K * TD_DISP)

    def issue(tb, carry):
        for u in range(ISSUE_UNROLL):
            t = tb * ISSUE_UNROLL + u
            for k in range(TOP_K):
                _row_copy(x1t_ref, t, xs_ref, idx_smem[base + k * TD_DISP + t], 1,
                          sem).start(priority=k % 2)
        return carry

    lax.fori_loop(0, TD_DISP // ISSUE_UNROLL, issue, 0)

    @pl.when(i == pl.num_programs(0) - 1)
    def _():
        _index_copy(didx_hbm, i + 1, idx_smem, 1 - slot, isems, TD_DISP).wait()

    @pl.when(i == 0)
    def _():
        zbuf[...] = jnp.zeros_like(zbuf)
        for wait in (False, True):
            for e in range(N_EXPERTS):
                npad = padlen_ref[e]
                for piece in _pad_pieces():
                    pos = padpos_ref[e] + (npad & ~(2 * piece - 1))
                    cp = _row_copy(zbuf, 0, xs_ref, pos, piece, zsem)

                    @pl.when((npad & piece) != 0)
                    def _():
                        cp.wait() if wait else cp.start()

            def tail(hc, carry):
                cp = _row_copy(zbuf, 0, xs_ref, hc * half, half, zsem)
                cp.wait() if wait else cp.start()
                return carry

            lax.fori_loop(2 * nu_ref[0], n_half, tail, 0)

    for k in range(TOP_K):
        _row_copy(x1t_ref, 0, xs_ref, 0, TD_DISP, sem).wait()


def _tiled_index(dest, tile_tokens):
    k, n = dest.shape
    return dest.reshape(k, n // tile_tokens, tile_tokens).transpose(1, 0, 2).reshape(k * n)


def _disp(dest, padpos, padlen, n_used, x1t, p_rows):
    n = dest.shape[1]
    didx = _tiled_index(dest, TD_DISP)
    anyspec = pl.BlockSpec(memory_space=pl.ANY)
    return pl.pallas_call(
        _disp_kernel,
        grid_spec=pltpu.PrefetchScalarGridSpec(
            num_scalar_prefetch=3, grid=(n // TD_DISP,),
            in_specs=[anyspec,
                      pl.BlockSpec((TD_DISP * ROW_TILES, LANES), lambda i, *_: (i, 0))],
            out_specs=anyspec,
            scratch_shapes=[pltpu.VMEM((CHUNK // 2 * ROW_TILES, LANES), F32),
                            pltpu.SMEM((2 * TOP_K * TD_DISP,), I32),
                            pltpu.SemaphoreType.DMA, pltpu.SemaphoreType.DMA,
                            pltpu.SemaphoreType.DMA((2,))]),
        out_shape=jax.ShapeDtypeStruct((p_rows * ROW_TILES, LANES), F32),
        compiler_params=pltpu.CompilerParams(dimension_semantics=("arbitrary",),
                                             has_side_effects=True),
        name="disp",
    )(padpos, padlen, n_used, didx, x1t)


def _moe_kernel(ce_ref, nu_ref, slot_ref, nxt_ref, xs_ref, wgu_hbm, bgu_ref, wd_hbm, bd_ref, ys_ref,
                wgu_f32, wd_f32, wgu_bf, wd_bf, wsems):
    c = pl.program_id(0)
    used = c < nu_ref[0]

    def weight_copies(e, slot):
        return (pltpu.make_async_copy(wgu_hbm.at[e], wgu_f32.at[slot], wsems.at[0, slot]),
                pltpu.make_async_copy(wd_hbm.at[e], wd_f32.at[slot], wsems.at[1, slot]))

    @pl.when(used & ((c == 0) | (ce_ref[c] != ce_ref[jnp.maximum(c - 1, 0)])))
    def _():
        slot = slot_ref[c]

        @pl.when(c == 0)
        def _():
            for cp in weight_copies(ce_ref[0], slot):
                cp.start()

        for cp in weight_copies(ce_ref[c], slot):
            cp.wait()

        @pl.when(nxt_ref[c] >= 0)
        def _():
            for cp in weight_copies(nxt_ref[c], 1 - slot):
                cp.start()

        wgu_bf[...] = wgu_f32[slot].astype(BF16)
        wd_bf[...] = wd_f32[slot].astype(BF16)

    @pl.when(used)
    def _():
        x = jnp.concatenate(
            [xs_ref[pl.ds(s, CHUNK, stride=ROW_TILES), :] for s in range(ROW_TILES)], axis=-1)
        gu = _dot(x.astype(BF16), wgu_bf[...]) + bgu_ref[0]
        gate = jnp.minimum(gu[:, :D_EXPERT], SWIGLU_LIMIT)
        up = jnp.clip(gu[:, D_EXPERT:], -SWIGLU_LIMIT, SWIGLU_LIMIT)
        h = (up + 1.0) * gate * jax.nn.sigmoid(SWIGLU_ALPHA * gate)
        y = _dot(h.astype(BF16), wd_bf[...]) + bd_ref[0]
        for s in range(ROW_TILES):
            ys_ref[pl.ds(s, CHUNK, stride=ROW_TILES), :] = y[:, s * LANES:(s + 1) * LANES]

    @pl.when(c >= nu_ref[0])
    def _():
        ys_ref[...] = jnp.zeros_like(ys_ref)


def _moe(chunk_e, n_used, run_slot, next_e, xs, wgu, bgu, wd, bd):
    n_chunks = xs.shape[0] // (CHUNK * ROW_TILES)
    d = D_MODEL

    def cc(c, ce, nu, *_):
        return jnp.minimum(c, nu[0] - 1)

    rows = pl.BlockSpec((CHUNK * ROW_TILES, LANES), lambda c, *s: (cc(c, *s), 0))
    bspec = lambda shape: pl.BlockSpec(shape, lambda c, *s: (s[0][cc(c, *s)], 0, 0))
    anyspec = pl.BlockSpec(memory_space=pl.ANY)
    return pl.pallas_call(
        _moe_kernel,
        grid_spec=pltpu.PrefetchScalarGridSpec(
            num_scalar_prefetch=4, grid=(n_chunks,),
            in_specs=[rows, anyspec, bspec((1, 1, 2 * D_EXPERT)), anyspec, bspec((1, 1, d))],
            out_specs=pl.BlockSpec((CHUNK * ROW_TILES, LANES), lambda c, *s: (c, 0)),
            scratch_shapes=[pltpu.VMEM((2, d, 2 * D_EXPERT), F32), pltpu.VMEM((2, D_EXPERT, d), F32),
                            pltpu.VMEM((d, 2 * D_EXPERT), BF16), pltpu.VMEM((D_EXPERT, d), BF16),
                            pltpu.SemaphoreType.DMA((2, 2))]),
        out_shape=jax.ShapeDtypeStruct(xs.shape, F32),
        compiler_params=pltpu.CompilerParams(dimension_semantics=("arbitrary",),
                                             vmem_limit_bytes=VMEM_LIMIT_BYTES),
        name="moe",
    )(chunk_e, n_used, run_slot, next_e, xs, wgu, bgu, wd, bd)


def _fin_kernel(didx_hbm, ys_ref, x1_ref, w_ref, p_ref, wpg_ref, wpp_ref, g2_ref, b2_ref,
                out_ref, rows_a, rows_b, idx_smem, sems, isems):
    ts = TS_FIN
    i = pl.program_id(0)
    last = pl.num_programs(0) - 1
    bufs = (rows_a, rows_b)

    def gather_row(par, k, t):
        _row_copy(ys_ref, idx_smem[par * (TOP_K * ts) + k * ts + t], bufs[par].at[k], t, 1,
                  sems.at[par]).start(priority=k % 2)

    def wait_rows(par):
        for k in range(TOP_K):
            _row_copy(ys_ref, 0, bufs[par].at[k], 0, ts, sems.at[par]).wait()

    def index_copy(tile, par):
        return _index_copy(didx_hbm, tile, idx_smem, par, isems, ts)

    @pl.when(i == 0)
    def _():
        index_copy(0, 0).start()
        index_copy(1, 1).start()
        index_copy(0, 0).wait()

        def issue(tb, carry):
            for u in range(ISSUE_UNROLL):
                for k in range(TOP_K):
                    gather_row(0, k, tb * ISSUE_UNROLL + u)
            return carry

        lax.fori_loop(0, ts // ISSUE_UNROLL, issue, 0)

    for par in range(2):
        @pl.when(i % 2 == par)
        def _():
            wait_rows(par)
            index_copy(i + 1, 1 - par).wait()
            index_copy(i + 2, par).start()
            for t in range(ts):
                for k in range(TOP_K):
                    gather_row(1 - par, k, t)

            x1 = x1_ref[...]
            ple = (jax.nn.sigmoid(_dot(x1.astype(BF16), wpg_ref[...]))
                   * _dot(p_ref[...].astype(BF16), wpp_ref[...]))
            z = DN_ALPHA * x1 + ple
            w = w_ref[...]
            for k in range(TOP_K):
                yk = jnp.concatenate(
                    [bufs[par][k, pl.ds(s, ts, stride=ROW_TILES), :] for s in range(ROW_TILES)],
                    axis=-1)
                z = z + w[:, k:k + 1] * yk
            out_ref[...] = _layer_norm(z, g2_ref[...], b2_ref[...])

            @pl.when(i == last)
            def _():
                wait_rows(1 - par)
                index_copy(i + 2, par).wait()


def _fin(dest, ys, x1, w_rows, p2, wpg, wpp, g2, b2):
    n, d = x1.shape
    nt = n // TS_FIN
    didx = _tiled_index(dest, TS_FIN)
    const = lambda shape: pl.BlockSpec(shape, lambda i: (0,) * len(shape))
    tile = lambda width: pl.BlockSpec((TS_FIN, width), lambda i: (i, 0))
    anyspec = pl.BlockSpec(memory_space=pl.ANY)
    return pl.pallas_call(
        _fin_kernel,
        grid=(nt,),
        in_specs=[anyspec, anyspec, tile(d), tile(TOP_K), tile(D_PLE),
                  const((d, d)), const((D_PLE, d)), const((1, d)), const((1, d))],
        out_specs=tile(d),
        scratch_shapes=[pltpu.VMEM((TOP_K, TS_FIN * ROW_TILES, LANES), F32),
                        pltpu.VMEM((TOP_K, TS_FIN * ROW_TILES, LANES), F32),
                        pltpu.SMEM((2 * TOP_K * TS_FIN,), I32),
                        pltpu.SemaphoreType.DMA((2,)), pltpu.SemaphoreType.DMA((2,))],
        out_shape=jax.ShapeDtypeStruct((n, d), F32),
        compiler_params=pltpu.CompilerParams(dimension_semantics=("arbitrary",),
                                             vmem_limit_bytes=VMEM_LIMIT_BYTES),
        name="fin",
    )(didx, ys, x1, w_rows, p2, wpg, wpp, g2, b2)


def _permute_w_in(w_in):
    d = D_MODEL
    b_, c_, h_, q_ = (w_in[:, i * d:(i + 1) * d] for i in range(4))
    k_ = w_in[:, 4 * d:4 * d + KV_DIM]
    v_ = w_in[:, 4 * d + KV_DIM:4 * d + 2 * KV_DIM]
    gc_ = w_in[:, 4 * d + 2 * KV_DIM:5 * d + 2 * KV_DIM]
    ga_ = w_in[:, 5 * d + 2 * KV_DIM:]
    return jnp.concatenate([b_, c_, h_, q_, gc_, ga_, k_, v_], axis=1).astype(BF16)


def _replication_matrix():
    src = jnp.arange(KV_DIM)[:, None]
    dst = jnp.arange(D_MODEL)[None, :]
    same_head = (dst // (GQA_GROUP * HEAD_DIM)) == (src // HEAD_DIM)
    same_dim = (dst % HEAD_DIM) == (src % HEAD_DIM)
    return (same_head & same_dim).astype(BF16)


def kernel(x, p, w_in, conv_w, w_br_conv, w_br_attn, attn_sinks, w_out, ln1_g, ln1_b, w_router,
           b_router, w_gu, b_gu, w_down, b_down, w_ple_proj, w_ple_gate, ln2_g, ln2_b):
    bsz, seq, d = x.shape
    n = bsz * seq
    for i in range(DEPTH):
        x1, x1t = _mix(x, _permute_w_in(w_in[i]), _replication_matrix(), conv_w[i],
                       w_br_conv[i].astype(BF16), w_br_attn[i].astype(BF16),
                       attn_sinks[i], w_out[i].astype(BF16),
                       ln1_g[i][None, :], ln1_b[i][None, :])
        x1 = x1.reshape(n, d)

        wr_t = w_router[i].T
        wr_hi = wr_t.astype(BF16)
        wr_lo = (wr_t - wr_hi.astype(F32)).astype(BF16)
        idx, w_top, rank, cnt = _route(x1, wr_hi, wr_lo, b_router[i][:, None])

        counts = cnt[:, 0]
        padded = ((counts + CHUNK - 1) // CHUNK) * CHUNK
        end_padded = jnp.cumsum(padded)
        start_padded = end_padded - padded
        p_rows = n * TOP_K + N_EXPERTS * CHUNK
        n_chunks = p_rows // CHUNK
        chunk_start = jnp.arange(n_chunks, dtype=I32) * CHUNK
        chunk_e = jnp.minimum(
            jnp.sum((end_padded[None, :] <= chunk_start[:, None]).astype(I32), axis=1),
            N_EXPERTS - 1)
        n_used = (end_padded[-1:] // CHUNK).astype(I32)

        dest = _dest(start_padded.astype(I32), idx, rank)
        xs = _disp(dest, (start_padded + counts).astype(I32), (padded - counts).astype(I32),
                   n_used, x1t, p_rows)
        eid = jnp.arange(N_EXPERTS, dtype=I32)
        active = padded > 0
        run_idx = jnp.cumsum(active.astype(I32)) - 1
        later = active[None, :] & (eid[None, :] > eid[:, None])
        nxt = jnp.min(jnp.where(later, eid[None, :], N_EXPERTS), axis=1)
        nxt = jnp.where(nxt == N_EXPERTS, -1, nxt)
        onehot = (chunk_e[:, None] == eid[None, :]).astype(I32)
        run_slot = jnp.sum(onehot * (run_idx & 1)[None, :], axis=1).astype(I32)
        next_e = jnp.sum(onehot * nxt[None, :], axis=1).astype(I32)
        ys = _moe(chunk_e, n_used, run_slot, next_e, xs, w_gu[i], b_gu[i][:, None, :],
                  w_down[i], b_down[i][:, None, :])
        out = _fin(dest, ys, x1, w_top.T, p[i].reshape(n, D_PLE),
                   w_ple_gate[i].astype(BF16), w_ple_proj[i].astype(BF16),
                   ln2_g[i][None, :], ln2_b[i][None, :])
        x = out.reshape(bsz, seq, d)
    return x
```

```python
import math

import jax
import jax.numpy as jnp
from jax import lax
from jax.experimental import pallas as pl
from jax.experimental.pallas import tpu as pltpu

F32 = jnp.float32
BF16 = jnp.bfloat16
I32 = jnp.int32

SUBLANES = 8
LANES = 128
VMEM_LIMIT_BYTES = 56 * 1024 * 1024

D_MODEL = 1024
HEAD_DIM = 64
N_Q_HEADS = 16
N_KV_HEADS = 4
GQA_GROUP = N_Q_HEADS // N_KV_HEADS
KV_DIM = N_KV_HEADS * HEAD_DIM
WINDOW = 128
N_EXPERTS = 32
TOP_K = 4
D_EXPERT = D_MODEL
SWIGLU_LIMIT = 7.0
SWIGLU_ALPHA = 1.702
D_PLE = 256
DEPTH = 1
DN_ALPHA = (2.0 * DEPTH) ** 0.25
LN_EPS = 1e-5
ROW_TILES = D_MODEL // LANES

OFF_B, OFF_C, OFF_H, OFF_Q, OFF_GC, OFF_GA = (i * D_MODEL for i in range(6))
OFF_K = 6 * D_MODEL
OFF_V = OFF_K + KV_DIM
IN_TOTAL = OFF_V + KV_DIM

TS_MIX = 512
TS_ROUTE = 512
TN_DEST = 2048
TD_DISP = 1024
CHUNK = 512
TS_FIN = 256
ISSUE_UNROLL = 4


def _layer_norm(z, g, b):
    mu = jnp.mean(z, axis=-1, keepdims=True)
    zc = z - mu
    var = jnp.mean(zc * zc, axis=-1, keepdims=True)
    return zc * lax.rsqrt(var + LN_EPS) * g + b


def _dot(a, b):
    return jnp.dot(a, b, preferred_element_type=F32)


def _dot_nt(a, b):
    return lax.dot_general(a, b, (((1,), (1,)), ((), ())), preferred_element_type=F32)


def _mix_kernel(sinks_ref, x_ref, win_ref, rep_ref, convw_ref, wbrc_ref, wbra_ref, wout_ref,
                g1_ref, b1_ref, x1_ref, x1t_ref, kext, vext, uext):
    ts = TS_MIX
    j = pl.program_id(1)

    @pl.when(j == 0)
    def _():
        kext[0:WINDOW, :] = jnp.zeros((WINDOW, D_MODEL), BF16)
        vext[0:WINDOW, :] = jnp.zeros((WINDOW, D_MODEL), BF16)
        uext[0:SUBLANES, :] = jnp.zeros((SUBLANES, D_MODEL), F32)

    x = x_ref[0]
    xb = x.astype(BF16)

    def proj(off, width):
        return _dot(xb, win_ref[:, off:off + width])

    u = proj(OFF_C, D_MODEL) * proj(OFF_H, D_MODEL)
    uext[SUBLANES:SUBLANES + ts, :] = u
    y = (convw_ref[2:3, :] * u
         + convw_ref[1:2, :] * uext[SUBLANES - 1:SUBLANES - 1 + ts, :]
         + convw_ref[0:1, :] * uext[SUBLANES - 2:SUBLANES - 2 + ts, :])
    uext[0:SUBLANES, :] = u[ts - SUBLANES:ts, :]
    yc_in = proj(OFF_B, D_MODEL) * y
    y_conv = _dot(yc_in.astype(BF16), wbrc_ref[...])
    acc = jax.nn.sigmoid(proj(OFF_GC, D_MODEL)) * y_conv

    q = proj(OFF_Q, D_MODEL).astype(BF16)
    kb = proj(OFF_K, KV_DIM).astype(BF16)
    vb = proj(OFF_V, KV_DIM).astype(BF16)
    kext[WINDOW:WINDOW + ts, :] = _dot(kb, rep_ref[...]).astype(BF16)
    vext[WINDOW:WINDOW + ts, :] = _dot(vb, rep_ref[...]).astype(BF16)

    grp = GQA_GROUP * HEAD_DIM
    row = lax.broadcasted_iota(I32, (WINDOW, 2 * WINDOW), 0)
    col = lax.broadcasted_iota(I32, (WINDOW, 2 * WINDOW), 1)
    band = (col > row) & (col <= row + WINDOW)
    q_lane_grp = lax.broadcasted_iota(I32, (WINDOW, grp), 1) // HEAD_DIM
    scale = 1.0 / math.sqrt(HEAD_DIM)
    o_blocks = []
    for i in range(ts // WINDOW):
        if i == 0:
            mask = band & ((col >= WINDOW) | (j > 0))
        else:
            mask = band
        o_heads = []
        for h in range(N_KV_HEADS):
            qh = q[i * WINDOW:(i + 1) * WINDOW, h * grp:(h + 1) * grp]
            kh = kext[i * WINDOW:i * WINDOW + 2 * WINDOW, h * grp:(h + 1) * grp]
            vh = vext[i * WINDOW:i * WINDOW + 2 * WINDOW, h * grp:(h + 1) * grp]
            qs = jnp.concatenate(
                [jnp.where(q_lane_grp == g, qh, jnp.zeros_like(qh)) for g in range(GQA_GROUP)], axis=0)
            s_all = _dot_nt(qs, kh) * scale
            ps = []
            for g in range(GQA_GROUP):
                s = jnp.where(mask, s_all[g * WINDOW:(g + 1) * WINDOW], -jnp.inf)
                sink = sinks_ref[h * GQA_GROUP + g]
                m = jnp.maximum(jnp.max(s, axis=-1, keepdims=True), sink)
                e = jnp.exp(s - m)
                den = jnp.sum(e, axis=-1, keepdims=True) + jnp.exp(sink - m)
                ps.append((e / den).astype(BF16))
            pv = _dot(jnp.concatenate(ps, axis=0), vh)
            oh = jnp.zeros((WINDOW, grp), F32)
            for g in range(GQA_GROUP):
                oh = jnp.where(q_lane_grp == g, pv[g * WINDOW:(g + 1) * WINDOW], oh)
            o_heads.append(oh)
        o_blocks.append(jnp.concatenate(o_heads, axis=-1))
    o = jnp.concatenate(o_blocks, axis=0)
    kext[0:WINDOW, :] = kext[ts:ts + WINDOW, :]
    vext[0:WINDOW, :] = vext[ts:ts + WINDOW, :]

    y_attn = _dot(o.astype(BF16), wbra_ref[...])
    acc = acc + jax.nn.sigmoid(proj(OFF_GA, D_MODEL)) * y_attn

    z = DN_ALPHA * x + _dot(acc.astype(BF16), wout_ref[...])
    x1 = _layer_norm(z, g1_ref[...], b1_ref[...])
    x1_ref[0] = x1
    for s in range(ROW_TILES):
        x1t_ref[pl.ds(s, ts, stride=ROW_TILES), :] = x1[:, s * LANES:(s + 1) * LANES]


def _mix(x, w_in_p, rep, conv_w, wbrc, wbra, sinks, wout, g1, b1):
    bsz, seq, d = x.shape
    n = bsz * seq
    nj = seq // TS_MIX
    const = lambda shape: pl.BlockSpec(shape, lambda b, j, s: (0,) * len(shape),
                                       pipeline_mode=pl.Buffered(1))
    grid_spec = pltpu.PrefetchScalarGridSpec(
        num_scalar_prefetch=1,
        grid=(bsz, nj),
        in_specs=[
            pl.BlockSpec((1, TS_MIX, d), lambda b, j, s: (b, j, 0)),
            const((d, IN_TOTAL)),
            const((KV_DIM, d)),
            const((3, d)),
            const((d, d)),
            const((d, d)),
            const((d, d)),
            const((1, d)),
            const((1, d)),
        ],
        out_specs=[
            pl.BlockSpec((1, TS_MIX, d), lambda b, j, s: (b, j, 0)),
            pl.BlockSpec((TS_MIX * ROW_TILES, LANES), lambda b, j, s: (b * nj + j, 0)),
        ],
        scratch_shapes=[
            pltpu.VMEM((WINDOW + TS_MIX, d), BF16),
            pltpu.VMEM((WINDOW + TS_MIX, d), BF16),
            pltpu.VMEM((SUBLANES + TS_MIX, d), F32),
        ],
    )
    return pl.pallas_call(
        _mix_kernel,
        grid_spec=grid_spec,
        out_shape=[jax.ShapeDtypeStruct((bsz, seq, d), F32),
                   jax.ShapeDtypeStruct((n * ROW_TILES, LANES), F32)],
        compiler_params=pltpu.CompilerParams(
            dimension_semantics=("arbitrary", "arbitrary"),
            vmem_limit_bytes=VMEM_LIMIT_BYTES),
        name="mix",
    )(sinks, x, w_in_p, rep, conv_w, wbrc, wbra, wout, g1, b1)


def _route_kernel(x1_ref, whi_ref, wlo_ref, br_ref, idx_ref, w_ref, rank_ref, cnt_ref, carry):
    ts = TS_ROUTE
    i = pl.program_id(0)

    @pl.when(i == 0)
    def _():
        carry[...] = jnp.zeros_like(carry)

    x = x1_ref[...]
    xh = x.astype(BF16)
    xl = (x - xh.astype(F32)).astype(BF16)
    whi = whi_ref[...]
    logits = _dot_nt(whi, xh) + _dot_nt(whi, xl) + _dot_nt(wlo_ref[...], xh) + br_ref[...]

    eid = lax.broadcasted_iota(I32, (N_EXPERTS, ts), 0)
    rest = logits
    sels, vals = [], []
    for k in range(TOP_K):
        m = jnp.max(rest, axis=0, keepdims=True)
        idx = jnp.min(jnp.where(rest == m, eid, N_EXPERTS), axis=0, keepdims=True)
        sel = eid == idx
        rest = jnp.where(sel, -jnp.inf, rest)
        sels.append(sel)
        vals.append(m)
        idx_ref[k:k + 1, :] = idx
    exps = [jnp.exp(v - vals[0]) for v in vals]
    den = exps[0] + exps[1] + exps[2] + exps[3]
    for k in range(TOP_K):
        w_ref[k:k + 1, :] = exps[k] / den

    member = jnp.zeros((N_EXPERTS, ts), F32)
    for sel in sels:
        member = member + sel.astype(F32)
    r = lax.broadcasted_iota(I32, (ts, ts), 0)
    c = lax.broadcasted_iota(I32, (ts, ts), 1)
    upper = (r < c).astype(BF16)
    cum = _dot(member.astype(BF16), upper) + carry[:, 0:1]
    for k in range(TOP_K):
        rk = jnp.sum(jnp.where(sels[k], cum, 0.0), axis=0, keepdims=True)
        rank_ref[k:k + 1, :] = rk.astype(I32)
    carry[...] = carry[...] + jnp.sum(member, axis=1, keepdims=True)
    cnt_ref[...] = carry[...].astype(I32)


def _route(x1, whi, wlo, br):
    n, d = x1.shape
    const = lambda shape: pl.BlockSpec(shape, lambda i: (0,) * len(shape))
    tok = pl.BlockSpec((TOP_K, TS_ROUTE), lambda i: (0, i))
    return pl.pallas_call(
        _route_kernel,
        grid=(n // TS_ROUTE,),
        in_specs=[pl.BlockSpec((TS_ROUTE, d), lambda i: (i, 0)),
                  const((N_EXPERTS, d)), const((N_EXPERTS, d)), const((N_EXPERTS, 1))],
        out_specs=[tok, tok, tok, const((N_EXPERTS, LANES))],
        out_shape=[jax.ShapeDtypeStruct((TOP_K, n), I32),
                   jax.ShapeDtypeStruct((TOP_K, n), F32),
                   jax.ShapeDtypeStruct((TOP_K, n), I32),
                   jax.ShapeDtypeStruct((N_EXPERTS, LANES), I32)],
        scratch_shapes=[pltpu.VMEM((N_EXPERTS, LANES), F32)],
        compiler_params=pltpu.CompilerParams(dimension_semantics=("arbitrary",)),
        name="route",
    )(x1, whi, wlo, br)


def _dest_kernel(start_ref, idx_ref, rank_ref, dest_ref):
    idx = idx_ref[...]
    acc = rank_ref[...]
    for e in range(N_EXPERTS):
        acc = acc + jnp.where(idx == e, start_ref[e], 0)
    dest_ref[...] = acc


def _dest(start_padded, idx, rank):
    k, n = idx.shape
    tok = pl.BlockSpec((k, TN_DEST), lambda i, s: (0, i))
    return pl.pallas_call(
        _dest_kernel,
        grid_spec=pltpu.PrefetchScalarGridSpec(
            num_scalar_prefetch=1, grid=(n // TN_DEST,), in_specs=[tok, tok], out_specs=tok),
        out_shape=jax.ShapeDtypeStruct((k, n), I32),
        name="dest",
    )(start_padded, idx, rank)


def _row_copy(src, src_row, dst, dst_row, rows, sem):
    s0 = pl.multiple_of(src_row * ROW_TILES, ROW_TILES)
    d0 = pl.multiple_of(dst_row * ROW_TILES, ROW_TILES)
    return pltpu.make_async_copy(src.at[pl.ds(s0, rows * ROW_TILES)],
                                 dst.at[pl.ds(d0, rows * ROW_TILES)], sem)


def _pad_pieces():
    return [1 << b for b in reversed(range(CHUNK.bit_length() - 1))]


def _index_copy(idx_hbm, tile, idx_smem, slot, isems, tile_tokens):
    n_idx = TOP_K * tile_tokens
    last = idx_hbm.shape[0] // n_idx - 1
    src = pl.multiple_of(jnp.minimum(tile, last) * n_idx, n_idx)
    return pltpu.make_async_copy(idx_hbm.at[pl.ds(src, n_idx)],
                                 idx_smem.at[pl.ds(slot * n_idx, n_idx)], isems.at[slot])


def _disp_kernel(padpos_ref, padlen_ref, nu_ref, didx_hbm, x1t_hbm, xs_ref, zbuf, xbuf, idx_smem, sems,
                 zsem, isems, xsems):
    half = CHUNK // 2
    n_half = xs_ref.shape[0] // (half * ROW_TILES)
    i = pl.program_id(0)
    last = pl.num_programs(0) - 1
    slot = i % 2

    def tile_copy(tile):
        ring = tile % 3
        return _row_copy(x1t_hbm, jnp.minimum(tile, last) * TD_DISP, xbuf, ring * TD_DISP, TD_DISP,
                         xsems.at[ring])

    @pl.when(i == 0)
    def _():
        tile_copy(0).start()
        _index_copy(didx_hbm, 0, idx_smem, 0, isems, TD_DISP).start()

    tile_copy(i).wait()
    tile_copy(i + 1).start()
    _index_copy(didx_hbm, i, idx_smem, slot, isems, TD_DISP).wait()
    _index_copy(didx_hbm, i + 1, idx_smem, 1 - slot, isems, TD_DISP).start()
    base = slot * (TOP_K * TD_DISP)
    src0 = (i % 3) * TD_DISP

    def issue(tb, carry):
        for u in range(ISSUE_UNROLL):
            t = tb * ISSUE_UNROLL + u
            for k in range(TOP_K):
                _row_copy(xbuf, src0 + t, xs_ref, idx_smem[base + k * TD_DISP + t], 1,
                          sems.at[slot]).start(priority=k % 2)
        return carry

    lax.fori_loop(0, TD_DISP // ISSUE_UNROLL, issue, 0)

    @pl.when(i == last)
    def _():
        tile_copy(i + 1).wait()
        _index_copy(didx_hbm, i + 1, idx_smem, 1 - slot, isems, TD_DISP).wait()

    @pl.when(i == 0)
    def _():
        zbuf[...] = jnp.zeros_like(zbuf)
        for wait in (False, True):
            for e in range(N_EXPERTS):
                npad = padlen_ref[e]
                for piece in _pad_pieces():
                    pos = padpos_ref[e] + (npad & ~(2 * piece - 1))
                    cp = _row_copy(zbuf, 0, xs_ref, pos, piece, zsem)

                    @pl.when((npad & piece) != 0)
                    def _():
                        cp.wait() if wait else cp.start()

            def tail(hc, carry):
                cp = _row_copy(zbuf, 0, xs_ref, hc * half, half, zsem)
                cp.wait() if wait else cp.start()
                return carry

            lax.fori_loop(2 * nu_ref[0], n_half, tail, 0)

    def drain(par):
        for k in range(TOP_K):
            _row_copy(xbuf, 0, xs_ref, 0, TD_DISP, sems.at[par]).wait()

    @pl.when(i >= 1)
    def _():
        drain(1 - slot)

    @pl.when(i == last)
    def _():
        drain(slot)


def _tiled_index(dest, tile_tokens):
    k, n = dest.shape
    return dest.reshape(k, n // tile_tokens, tile_tokens).transpose(1, 0, 2).reshape(k * n)


def _disp(dest, padpos, padlen, n_used, x1t, p_rows):
    n = dest.shape[1]
    didx = _tiled_index(dest, TD_DISP)
    anyspec = pl.BlockSpec(memory_space=pl.ANY)
    return pl.pallas_call(
        _disp_kernel,
        grid_spec=pltpu.PrefetchScalarGridSpec(
            num_scalar_prefetch=3, grid=(n // TD_DISP,),
            in_specs=[anyspec, anyspec],
            out_specs=anyspec,
            scratch_shapes=[pltpu.VMEM((CHUNK // 2 * ROW_TILES, LANES), F32),
                            pltpu.VMEM((3 * TD_DISP * ROW_TILES, LANES), F32),
                            pltpu.SMEM((2 * TOP_K * TD_DISP,), I32),
                            pltpu.SemaphoreType.DMA((2,)), pltpu.SemaphoreType.DMA,
                            pltpu.SemaphoreType.DMA((2,)), pltpu.SemaphoreType.DMA((3,))]),
        out_shape=jax.ShapeDtypeStruct((p_rows * ROW_TILES, LANES), F32),
        compiler_params=pltpu.CompilerParams(dimension_semantics=("arbitrary",),
                                             vmem_limit_bytes=VMEM_LIMIT_BYTES,
                                             has_side_effects=True),
        name="disp",
    )(padpos, padlen, n_used, didx, x1t)


def _moe_kernel(ce_ref, nu_ref, slot_ref, nxt_ref, xs_ref, wgu_hbm, bgu_ref, wd_hbm, bd_ref, ys_ref,
                wgu_f32, wd_f32, wgu_bf, wd_bf, wsems):
    c = pl.program_id(0)
    used = c < nu_ref[0]

    def weight_copies(e, slot):
        return (pltpu.make_async_copy(wgu_hbm.at[e], wgu_f32.at[slot], wsems.at[0, slot]),
                pltpu.make_async_copy(wd_hbm.at[e], wd_f32.at[slot], wsems.at[1, slot]))

    @pl.when(used & ((c == 0) | (ce_ref[c] != ce_ref[jnp.maximum(c - 1, 0)])))
    def _():
        slot = slot_ref[c]

        @pl.when(c == 0)
        def _():
            for cp in weight_copies(ce_ref[0], slot):
                cp.start()

        for cp in weight_copies(ce_ref[c], slot):
            cp.wait()

        @pl.when(nxt_ref[c] >= 0)
        def _():
            for cp in weight_copies(nxt_ref[c], 1 - slot):
                cp.start()

        wgu_bf[...] = wgu_f32[slot].astype(BF16)
        wd_bf[...] = wd_f32[slot].astype(BF16)

    @pl.when(used)
    def _():
        x = jnp.concatenate(
            [xs_ref[pl.ds(s, CHUNK, stride=ROW_TILES), :] for s in range(ROW_TILES)], axis=-1)
        gu = _dot(x.astype(BF16), wgu_bf[...]) + bgu_ref[0]
        gate = jnp.minimum(gu[:, :D_EXPERT], SWIGLU_LIMIT)
        up = jnp.clip(gu[:, D_EXPERT:], -SWIGLU_LIMIT, SWIGLU_LIMIT)
        h = (up + 1.0) * gate * jax.nn.sigmoid(SWIGLU_ALPHA * gate)
        y = _dot(h.astype(BF16), wd_bf[...]) + bd_ref[0]
        for s in range(ROW_TILES):
            ys_ref[pl.ds(s, CHUNK, stride=ROW_TILES), :] = y[:, s * LANES:(s + 1) * LANES]

    @pl.when(c >= nu_ref[0])
    def _():
        ys_ref[...] = jnp.zeros_like(ys_ref)


def _moe(chunk_e, n_used, run_slot, next_e, xs, wgu, bgu, wd, bd):
    n_chunks = xs.shape[0] // (CHUNK * ROW_TILES)
    d = D_MODEL

    def cc(c, ce, nu, *_):
        return jnp.minimum(c, nu[0] - 1)

    rows = pl.BlockSpec((CHUNK * ROW_TILES, LANES), lambda c, *s: (cc(c, *s), 0))
    bspec = lambda shape: pl.BlockSpec(shape, lambda c, *s: (s[0][cc(c, *s)], 0, 0))
    anyspec = pl.BlockSpec(memory_space=pl.ANY)
    return pl.pallas_call(
        _moe_kernel,
        grid_spec=pltpu.PrefetchScalarGridSpec(
            num_scalar_prefetch=4, grid=(n_chunks,),
            in_specs=[rows, anyspec, bspec((1, 1, 2 * D_EXPERT)), anyspec, bspec((1, 1, d))],
            out_specs=pl.BlockSpec((CHUNK * ROW_TILES, LANES), lambda c, *s: (c, 0)),
            scratch_shapes=[pltpu.VMEM((2, d, 2 * D_EXPERT), F32), pltpu.VMEM((2, D_EXPERT, d), F32),
                            pltpu.VMEM((d, 2 * D_EXPERT), BF16), pltpu.VMEM((D_EXPERT, d), BF16),
                            pltpu.SemaphoreType.DMA((2, 2))]),
        out_shape=jax.ShapeDtypeStruct(xs.shape, F32),
        compiler_params=pltpu.CompilerParams(dimension_semantics=("arbitrary",),
                                             vmem_limit_bytes=VMEM_LIMIT_BYTES),
        name="moe",
    )(chunk_e, n_used, run_slot, next_e, xs, wgu, bgu, wd, bd)


def _fin_kernel(didx_ref, ys_ref, x1_ref, w_ref, p_ref, wpg_ref, wpp_ref, g2_ref, b2_ref,
                out_ref, rows_a, rows_b, sems):
    ts = TS_FIN
    i = pl.program_id(0)
    last = pl.num_programs(0) - 1
    bufs = (rows_a, rows_b)

    def gather_row(par, tile_base, k, t):
        _row_copy(ys_ref, didx_ref[tile_base + k * ts + t], bufs[par].at[k], t, 1,
                  sems.at[par]).start(priority=k % 2)

    def wait_rows(par):
        for k in range(TOP_K):
            _row_copy(ys_ref, 0, bufs[par].at[k], 0, ts, sems.at[par]).wait()

    @pl.when(i == 0)
    def _():
        def issue(tb, carry):
            for u in range(ISSUE_UNROLL):
                for k in range(TOP_K):
                    gather_row(0, 0, k, tb * ISSUE_UNROLL + u)
            return carry

        lax.fori_loop(0, ts // ISSUE_UNROLL, issue, 0)

    next_base = jnp.minimum(i + 1, last) * (TOP_K * ts)

    for par in range(2):
        @pl.when(i % 2 == par)
        def _():
            wait_rows(par)
            for t in range(ts):
                for k in range(TOP_K):
                    gather_row(1 - par, next_base, k, t)

            x1 = x1_ref[...]
            ple = (jax.nn.sigmoid(_dot(x1.astype(BF16), wpg_ref[...]))
                   * _dot(p_ref[...].astype(BF16), wpp_ref[...]))
            z = DN_ALPHA * x1 + ple
            w = w_ref[...]
            for k in range(TOP_K):
                yk = jnp.concatenate(
                    [bufs[par][k, pl.ds(s, ts, stride=ROW_TILES), :] for s in range(ROW_TILES)],
                    axis=-1)
                z = z + w[:, k:k + 1] * yk
            out_ref[...] = _layer_norm(z, g2_ref[...], b2_ref[...])

            @pl.when(i == last)
            def _():
                wait_rows(1 - par)


def _fin(dest, ys, x1, w_rows, p2, wpg, wpp, g2, b2):
    n, d = x1.shape
    nt = n // TS_FIN
    didx = _tiled_index(dest, TS_FIN)
    const = lambda shape: pl.BlockSpec(shape, lambda i, s: (0,) * len(shape))
    tile = lambda width: pl.BlockSpec((TS_FIN, width), lambda i, s: (i, 0))
    return pl.pallas_call(
        _fin_kernel,
        grid_spec=pltpu.PrefetchScalarGridSpec(
            num_scalar_prefetch=1, grid=(nt,),
            in_specs=[pl.BlockSpec(memory_space=pl.ANY), tile(d), tile(TOP_K), tile(D_PLE),
                      const((d, d)), const((D_PLE, d)), const((1, d)), const((1, d))],
            out_specs=tile(d),
            scratch_shapes=[pltpu.VMEM((TOP_K, TS_FIN * ROW_TILES, LANES), F32),
                            pltpu.VMEM((TOP_K, TS_FIN * ROW_TILES, LANES), F32),
                            pltpu.SemaphoreType.DMA((2,))]),
        out_shape=jax.ShapeDtypeStruct((n, d), F32),
        compiler_params=pltpu.CompilerParams(dimension_semantics=("arbitrary",),
                                             vmem_limit_bytes=VMEM_LIMIT_BYTES),
        name="fin",
    )(didx, ys, x1, w_rows, p2, wpg, wpp, g2, b2)


def _permute_w_in(w_in):
    d = D_MODEL
    b_, c_, h_, q_ = (w_in[:, i * d:(i + 1) * d] for i in range(4))
    k_ = w_in[:, 4 * d:4 * d + KV_DIM]
    v_ = w_in[:, 4 * d + KV_DIM:4 * d + 2 * KV_DIM]
    gc_ = w_in[:, 4 * d + 2 * KV_DIM:5 * d + 2 * KV_DIM]
    ga_ = w_in[:, 5 * d + 2 * KV_DIM:]
    return jnp.concatenate([b_, c_, h_, q_, gc_, ga_, k_, v_], axis=1).astype(BF16)


def _replication_matrix():
    src = jnp.arange(KV_DIM)[:, None]
    dst = jnp.arange(D_MODEL)[None, :]
    same_head = (dst // (GQA_GROUP * HEAD_DIM)) == (src // HEAD_DIM)
    same_dim = (dst % HEAD_DIM) == (src % HEAD_DIM)
    return (same_head & same_dim).astype(BF16)


def kernel(x, p, w_in, conv_w, w_br_conv, w_br_attn, attn_sinks, w_out, ln1_g, ln1_b, w_router,
           b_router, w_gu, b_gu, w_down, b_down, w_ple_proj, w_ple_gate, ln2_g, ln2_b):
    bsz, seq, d = x.shape
    n = bsz * seq
    for i in range(DEPTH):
        x1, x1t = _mix(x, _permute_w_in(w_in[i]), _replication_matrix(), conv_w[i],
                       w_br_conv[i].astype(BF16), w_br_attn[i].astype(BF16),
                       attn_sinks[i], w_out[i].astype(BF16),
                       ln1_g[i][None, :], ln1_b[i][None, :])
        x1 = x1.reshape(n, d)

        wr_t = w_router[i].T
        wr_hi = wr_t.astype(BF16)
        wr_lo = (wr_t - wr_hi.astype(F32)).astype(BF16)
        idx, w_top, rank, cnt = _route(x1, wr_hi, wr_lo, b_router[i][:, None])

        counts = cnt[:, 0]
        padded = ((counts + CHUNK - 1) // CHUNK) * CHUNK
        end_padded = jnp.cumsum(padded)
        start_padded = end_padded - padded
        p_rows = n * TOP_K + N_EXPERTS * CHUNK
        n_chunks = p_rows // CHUNK
        chunk_start = jnp.arange(n_chunks, dtype=I32) * CHUNK
        chunk_e = jnp.minimum(
            jnp.sum((end_padded[None, :] <= chunk_start[:, None]).astype(I32), axis=1),
            N_EXPERTS - 1)
        n_used = (end_padded[-1:] // CHUNK).astype(I32)

        dest = _dest(start_padded.astype(I32), idx, rank)
        xs = _disp(dest, (start_padded + counts).astype(I32), (padded - counts).astype(I32),
                   n_used, x1t, p_rows)
        eid = jnp.arange(N_EXPERTS, dtype=I32)
        active = padded > 0
        run_idx = jnp.cumsum(active.astype(I32)) - 1
        later = active[None, :] & (eid[None, :] > eid[:, None])
        nxt = jnp.min(jnp.where(later, eid[None, :], N_EXPERTS), axis=1)
        nxt = jnp.where(nxt == N_EXPERTS, -1, nxt)
        onehot = (chunk_e[:, None] == eid[None, :]).astype(I32)
        run_slot = jnp.sum(onehot * (run_idx & 1)[None, :], axis=1).astype(I32)
        next_e = jnp.sum(onehot * nxt[None, :], axis=1).astype(I32)
        ys = _moe(chunk_e, n_used, run_slot, next_e, xs, w_gu[i], b_gu[i][:, None, :],
                  w_down[i], b_down[i][:, None, :])
        out = _fin(dest, ys, x1, w_top.T, p[i].reshape(n, D_PLE),
                   w_ple_gate[i].astype(BF16), w_ple_proj[i].astype(BF16),
                   ln2_g[i][None, :], ln2_b[i][None, :])
        x = out.reshape(bsz, seq, d)
    return x
```

```python
import math

import jax
import jax.numpy as jnp
from jax import lax
from jax.experimental import pallas as pl
from jax.experimental.pallas import tpu as pltpu

F32 = jnp.float32
BF16 = jnp.bfloat16
I32 = jnp.int32

SUBLANES = 8
LANES = 128
VMEM_LIMIT_BYTES = 56 * 1024 * 1024

D_MODEL = 1024
HEAD_DIM = 64
N_Q_HEADS = 16
N_KV_HEADS = 4
GQA_GROUP = N_Q_HEADS // N_KV_HEADS
KV_DIM = N_KV_HEADS * HEAD_DIM
WINDOW = 128
N_EXPERTS = 32
TOP_K = 4
D_EXPERT = D_MODEL
SWIGLU_LIMIT = 7.0
SWIGLU_ALPHA = 1.702
D_PLE = 256
DEPTH = 1
DN_ALPHA = (2.0 * DEPTH) ** 0.25
LN_EPS = 1e-5
ROW_TILES = D_MODEL // LANES

OFF_B, OFF_C, OFF_H, OFF_Q, OFF_GC, OFF_GA = (i * D_MODEL for i in range(6))
OFF_K = 6 * D_MODEL
OFF_V = OFF_K + KV_DIM
IN_TOTAL = OFF_V + KV_DIM

TS_MIX = 512
TN_DEST = 2048
TD_DISP = 1024
CHUNK = 512
TS_FIN = 256
ISSUE_UNROLL = 4


def _layer_norm(z, g, b):
    mu = jnp.mean(z, axis=-1, keepdims=True)
    zc = z - mu
    var = jnp.mean(zc * zc, axis=-1, keepdims=True)
    return zc * lax.rsqrt(var + LN_EPS) * g + b


def _dot(a, b):
    return jnp.dot(a, b, preferred_element_type=F32)


def _dot_nt(a, b):
    return lax.dot_general(a, b, (((1,), (1,)), ((), ())), preferred_element_type=F32)


def _mix_kernel(sinks_ref, x_ref, win_ref, rep_ref, convw_ref, wbrc_ref, wbra_ref, wout_ref,
                g1_ref, b1_ref, whi_ref, wlo_ref, br_ref,
                x1_ref, x1t_ref, idx_ref, w_ref, rank_ref, cnt_ref,
                kext, vext, uext, upper, carry, xprev):
    ts = TS_MIX
    j = pl.program_id(1)
    tile_id = pl.program_id(0) * pl.num_programs(1) + j
    last_tile = idx_ref.shape[1] // ts - 1

    @pl.when(j == 0)
    def _():
        kext[0:WINDOW, :] = jnp.zeros((WINDOW, D_MODEL), BF16)
        vext[0:WINDOW, :] = jnp.zeros((WINDOW, D_MODEL), BF16)
        uext[0:SUBLANES, :] = jnp.zeros((SUBLANES, D_MODEL), F32)

    @pl.when(tile_id == 0)
    def _():
        carry[...] = jnp.zeros_like(carry)
        xprev[...] = jnp.zeros_like(xprev)
        r = lax.broadcasted_iota(I32, (ts, ts), 0)
        c = lax.broadcasted_iota(I32, (ts, ts), 1)
        upper[...] = (r < c).astype(BF16)

    route = lambda x1_tile, tile, live: _route_tile(
        x1_tile, tile, live, whi_ref, wlo_ref, br_ref, upper, carry, idx_ref, w_ref, rank_ref, cnt_ref)
    route(xprev[...], jnp.maximum(tile_id - 1, 0), tile_id >= 1)

    x = x_ref[0]
    xb = x.astype(BF16)

    def proj(off, width):
        return _dot(xb, win_ref[:, off:off + width])

    u = proj(OFF_C, D_MODEL) * proj(OFF_H, D_MODEL)
    uext[SUBLANES:SUBLANES + ts, :] = u
    y = (convw_ref[2:3, :] * u
         + convw_ref[1:2, :] * uext[SUBLANES - 1:SUBLANES - 1 + ts, :]
         + convw_ref[0:1, :] * uext[SUBLANES - 2:SUBLANES - 2 + ts, :])
    uext[0:SUBLANES, :] = u[ts - SUBLANES:ts, :]
    yc_in = proj(OFF_B, D_MODEL) * y
    y_conv = _dot(yc_in.astype(BF16), wbrc_ref[...])
    acc = jax.nn.sigmoid(proj(OFF_GC, D_MODEL)) * y_conv

    q = (proj(OFF_Q, D_MODEL) * (1.0 / math.sqrt(HEAD_DIM))).astype(BF16)
    kb = proj(OFF_K, KV_DIM).astype(BF16)
    vb = proj(OFF_V, KV_DIM).astype(BF16)
    kext[WINDOW:WINDOW + ts, :] = _dot(kb, rep_ref[...]).astype(BF16)
    vext[WINDOW:WINDOW + ts, :] = _dot(vb, rep_ref[...]).astype(BF16)

    grp = GQA_GROUP * HEAD_DIM
    row = lax.broadcasted_iota(I32, (WINDOW, 2 * WINDOW), 0)
    col = lax.broadcasted_iota(I32, (WINDOW, 2 * WINDOW), 1)
    band = (col > row) & (col <= row + WINDOW)
    q_lane_grp = lax.broadcasted_iota(I32, (WINDOW, grp), 1) // HEAD_DIM
    o_blocks = []
    for i in range(ts // WINDOW):
        if i == 0:
            mask = band & ((col >= WINDOW) | (j > 0))
        else:
            mask = band
        o_heads = []
        for h in range(N_KV_HEADS):
            qh = q[i * WINDOW:(i + 1) * WINDOW, h * grp:(h + 1) * grp]
            kh = kext[i * WINDOW:i * WINDOW + 2 * WINDOW, h * grp:(h + 1) * grp]
            vh = vext[i * WINDOW:i * WINDOW + 2 * WINDOW, h * grp:(h + 1) * grp]
            qs = jnp.concatenate(
                [jnp.where(q_lane_grp == g, qh, jnp.zeros_like(qh)) for g in range(GQA_GROUP)], axis=0)
            s_all = _dot_nt(qs, kh)
            ps = []
            for g in range(GQA_GROUP):
                s = jnp.where(mask, s_all[g * WINDOW:(g + 1) * WINDOW], -jnp.inf)
                sink = sinks_ref[h * GQA_GROUP + g]
                m = jnp.maximum(jnp.max(s, axis=-1, keepdims=True), sink)
                e = jnp.exp(s - m)
                den = jnp.sum(e, axis=-1, keepdims=True) + jnp.exp(sink - m)
                ps.append((e / den).astype(BF16))
            pv = _dot(jnp.concatenate(ps, axis=0), vh)
            oh = jnp.zeros((WINDOW, grp), F32)
            for g in range(GQA_GROUP):
                oh = jnp.where(q_lane_grp == g, pv[g * WINDOW:(g + 1) * WINDOW], oh)
            o_heads.append(oh)
        o_blocks.append(jnp.concatenate(o_heads, axis=-1))
    o = jnp.concatenate(o_blocks, axis=0)
    kext[0:WINDOW, :] = kext[ts:ts + WINDOW, :]
    vext[0:WINDOW, :] = vext[ts:ts + WINDOW, :]

    y_attn = _dot(o.astype(BF16), wbra_ref[...])
    acc = acc + jax.nn.sigmoid(proj(OFF_GA, D_MODEL)) * y_attn

    z = DN_ALPHA * x + _dot(acc.astype(BF16), wout_ref[...])
    x1 = _layer_norm(z, g1_ref[...], b1_ref[...])
    x1_ref[0] = x1
    for s in range(ROW_TILES):
        x1t_ref[pl.ds(s, ts, stride=ROW_TILES), :] = x1[:, s * LANES:(s + 1) * LANES]

    xprev[...] = x1

    @pl.when(tile_id == last_tile)
    def _():
        route(x1, tile_id, True)


def _route_tile(x1, tile, live, whi_ref, wlo_ref, br_ref, upper, carry, idx_ref, w_ref, rank_ref,
                cnt_ref):
    ts = x1.shape[0]
    cols = pl.ds(pl.multiple_of(tile * ts, ts), ts)
    xh = x1.astype(BF16)
    xl = (x1 - xh.astype(F32)).astype(BF16)
    whi = whi_ref[...]
    logits = _dot_nt(whi, xh) + _dot_nt(whi, xl) + _dot_nt(wlo_ref[...], xh) + br_ref[...]

    eid = lax.broadcasted_iota(I32, (N_EXPERTS, ts), 0)
    rest = logits
    sels, vals = [], []
    for k in range(TOP_K):
        m = jnp.max(rest, axis=0, keepdims=True)
        idx = jnp.min(jnp.where(rest == m, eid, N_EXPERTS), axis=0, keepdims=True)
        sel = eid == idx
        rest = jnp.where(sel, -jnp.inf, rest)
        sels.append(sel)
        vals.append(m)
        idx_ref[k:k + 1, cols] = idx
    exps = [jnp.exp(v - vals[0]) for v in vals]
    den = exps[0] + exps[1] + exps[2] + exps[3]
    for k in range(TOP_K):
        w_ref[k:k + 1, cols] = exps[k] / den

    member = jnp.zeros((N_EXPERTS, ts), F32)
    for sel in sels:
        member = member + sel.astype(F32)
    cum = _dot(member.astype(BF16), upper[...]) + carry[:, 0:1]
    for k in range(TOP_K):
        rk = jnp.sum(jnp.where(sels[k], cum, 0.0), axis=0, keepdims=True)
        rank_ref[k:k + 1, cols] = rk.astype(I32)
    carry[...] = carry[...] + jnp.where(live, jnp.sum(member, axis=1, keepdims=True), 0.0)
    cnt_ref[...] = carry[...].astype(I32)


def _mix(x, w_in_p, rep, conv_w, wbrc, wbra, sinks, wout, g1, b1, whi, wlo, br):
    bsz, seq, d = x.shape
    n = bsz * seq
    nj = seq // TS_MIX
    const = lambda shape: pl.BlockSpec(shape, lambda b, j, s: (0,) * len(shape),
                                       pipeline_mode=pl.Buffered(1))
    tok = pl.BlockSpec((TOP_K, n), lambda b, j, s: (0, 0))
    grid_spec = pltpu.PrefetchScalarGridSpec(
        num_scalar_prefetch=1,
        grid=(bsz, nj),
        in_specs=[
            pl.BlockSpec((1, TS_MIX, d), lambda b, j, s: (b, j, 0)),
            const((d, IN_TOTAL)),
            const((KV_DIM, d)),
            const((3, d)),
            const((d, d)),
            const((d, d)),
            const((d, d)),
            const((1, d)),
            const((1, d)),
            const((N_EXPERTS, d)),
            const((N_EXPERTS, d)),
            const((N_EXPERTS, 1)),
        ],
        out_specs=[
            pl.BlockSpec((1, TS_MIX, d), lambda b, j, s: (b, j, 0)),
            pl.BlockSpec((TS_MIX * ROW_TILES, LANES), lambda b, j, s: (b * nj + j, 0)),
            tok, tok, tok,
            pl.BlockSpec((N_EXPERTS, LANES), lambda b, j, s: (0, 0)),
        ],
        scratch_shapes=[
            pltpu.VMEM((WINDOW + TS_MIX, d), BF16),
            pltpu.VMEM((WINDOW + TS_MIX, d), BF16),
            pltpu.VMEM((SUBLANES + TS_MIX, d), F32),
            pltpu.VMEM((TS_MIX, TS_MIX), BF16),
            pltpu.VMEM((N_EXPERTS, LANES), F32),
            pltpu.VMEM((TS_MIX, d), F32),
        ],
    )
    return pl.pallas_call(
        _mix_kernel,
        grid_spec=grid_spec,
        out_shape=[jax.ShapeDtypeStruct((bsz, seq, d), F32),
                   jax.ShapeDtypeStruct((n * ROW_TILES, LANES), F32),
                   jax.ShapeDtypeStruct((TOP_K, n), I32),
                   jax.ShapeDtypeStruct((TOP_K, n), F32),
                   jax.ShapeDtypeStruct((TOP_K, n), I32),
                   jax.ShapeDtypeStruct((N_EXPERTS, LANES), I32)],
        compiler_params=pltpu.CompilerParams(
            dimension_semantics=("arbitrary", "arbitrary"),
            vmem_limit_bytes=VMEM_LIMIT_BYTES),
        name="mix",
    )(sinks, x, w_in_p, rep, conv_w, wbrc, wbra, wout, g1, b1, whi, wlo, br)


def _dest_kernel(start_ref, idx_ref, rank_ref, dest_ref):
    idx = idx_ref[...]
    acc = rank_ref[...]
    for e in range(N_EXPERTS):
        acc = acc + jnp.where(idx == e, start_ref[e], 0)
    dest_ref[...] = acc


def _dest(start_padded, idx, rank):
    k, n = idx.shape
    tok = pl.BlockSpec((k, TN_DEST), lambda i, s: (0, i))
    return pl.pallas_call(
        _dest_kernel,
        grid_spec=pltpu.PrefetchScalarGridSpec(
            num_scalar_prefetch=1, grid=(n // TN_DEST,), in_specs=[tok, tok], out_specs=tok),
        out_shape=jax.ShapeDtypeStruct((k, n), I32),
        name="dest",
    )(start_padded, idx, rank)


def _row_copy(src, src_row, dst, dst_row, rows, sem):
    s0 = pl.multiple_of(src_row * ROW_TILES, ROW_TILES)
    d0 = pl.multiple_of(dst_row * ROW_TILES, ROW_TILES)
    return pltpu.make_async_copy(src.at[pl.ds(s0, rows * ROW_TILES)],
                                 dst.at[pl.ds(d0, rows * ROW_TILES)], sem)


def _pad_pieces():
    return [1 << b for b in reversed(range(CHUNK.bit_length() - 1))]


def _index_copy(idx_hbm, tile, idx_smem, slot, isems, tile_tokens):
    n_idx = TOP_K * tile_tokens
    last = idx_hbm.shape[0] // n_idx - 1
    src = pl.multiple_of(jnp.minimum(tile, last) * n_idx, n_idx)
    return pltpu.make_async_copy(idx_hbm.at[pl.ds(src, n_idx)],
                                 idx_smem.at[pl.ds(slot * n_idx, n_idx)], isems.at[slot])


def _disp_kernel(padpos_ref, padlen_ref, nu_ref, didx_hbm, x1t_hbm, xs_ref, zbuf, xbuf, idx_smem, sems,
                 zsem, isems, xsems):
    half = CHUNK // 2
    n_half = xs_ref.shape[0] // (half * ROW_TILES)
    i = pl.program_id(0)
    last = pl.num_programs(0) - 1
    slot = i % 2

    def tile_copy(tile):
        ring = tile % 3
        return _row_copy(x1t_hbm, jnp.minimum(tile, last) * TD_DISP, xbuf, ring * TD_DISP, TD_DISP,
                         xsems.at[ring])

    @pl.when(i == 0)
    def _():
        tile_copy(0).start()
        _index_copy(didx_hbm, 0, idx_smem, 0, isems, TD_DISP).start()

    tile_copy(i).wait()
    tile_copy(i + 1).start()
    _index_copy(didx_hbm, i, idx_smem, slot, isems, TD_DISP).wait()
    _index_copy(didx_hbm, i + 1, idx_smem, 1 - slot, isems, TD_DISP).start()
    base = slot * (TOP_K * TD_DISP)
    src0 = (i % 3) * TD_DISP

    def issue(tb, carry):
        for u in range(ISSUE_UNROLL):
            t = tb * ISSUE_UNROLL + u
            for k in range(TOP_K):
                _row_copy(xbuf, src0 + t, xs_ref, idx_smem[base + k * TD_DISP + t], 1,
                          sems.at[slot]).start(priority=k % 2)
        return carry

    lax.fori_loop(0, TD_DISP // ISSUE_UNROLL, issue, 0)

    @pl.when(i == last)
    def _():
        tile_copy(i + 1).wait()
        _index_copy(didx_hbm, i + 1, idx_smem, 1 - slot, isems, TD_DISP).wait()

    @pl.when(i == 0)
    def _():
        zbuf[...] = jnp.zeros_like(zbuf)
        for wait in (False, True):
            for e in range(N_EXPERTS):
                npad = padlen_ref[e]
                for piece in _pad_pieces():
                    pos = padpos_ref[e] + (npad & ~(2 * piece - 1))
                    cp = _row_copy(zbuf, 0, xs_ref, pos, piece, zsem)

                    @pl.when((npad & piece) != 0)
                    def _():
                        cp.wait() if wait else cp.start()

            def tail(hc, carry):
                cp = _row_copy(zbuf, 0, xs_ref, hc * half, half, zsem)
                cp.wait() if wait else cp.start()
                return carry

            lax.fori_loop(2 * nu_ref[0], n_half, tail, 0)

    def drain(par):
        for k in range(TOP_K):
            _row_copy(xbuf, 0, xs_ref, 0, TD_DISP, sems.at[par]).wait()

    @pl.when(i >= 1)
    def _():
        drain(1 - slot)

    @pl.when(i == last)
    def _():
        drain(slot)


def _tiled_index(dest, tile_tokens):
    k, n = dest.shape
    return dest.reshape(k, n // tile_tokens, tile_tokens).transpose(1, 0, 2).reshape(k * n)


def _disp(dest, padpos, padlen, n_used, x1t, p_rows):
    n = dest.shape[1]
    didx = _tiled_index(dest, TD_DISP)
    anyspec = pl.BlockSpec(memory_space=pl.ANY)
    return pl.pallas_call(
        _disp_kernel,
        grid_spec=pltpu.PrefetchScalarGridSpec(
            num_scalar_prefetch=3, grid=(n // TD_DISP,),
            in_specs=[anyspec, anyspec],
            out_specs=anyspec,
            scratch_shapes=[pltpu.VMEM((CHUNK // 2 * ROW_TILES, LANES), F32),
                            pltpu.VMEM((3 * TD_DISP * ROW_TILES, LANES), F32),
                            pltpu.SMEM((2 * TOP_K * TD_DISP,), I32),
                            pltpu.SemaphoreType.DMA((2,)), pltpu.SemaphoreType.DMA,
                            pltpu.SemaphoreType.DMA((2,)), pltpu.SemaphoreType.DMA((3,))]),
        out_shape=jax.ShapeDtypeStruct((p_rows * ROW_TILES, LANES), F32),
        compiler_params=pltpu.CompilerParams(dimension_semantics=("arbitrary",),
                                             vmem_limit_bytes=VMEM_LIMIT_BYTES,
                                             has_side_effects=True),
        name="disp",
    )(padpos, padlen, n_used, didx, x1t)


def _moe_kernel(ce_ref, nu_ref, slot_ref, nxt_ref, xs_ref, wgu_hbm, bgu_ref, wd_hbm, bd_ref, ys_ref,
                wgu_f32, wd_f32, wgu_bf, wd_bf, wsems):
    c = pl.program_id(0)
    used = c < nu_ref[0]

    def weight_copies(e, slot):
        return (pltpu.make_async_copy(wgu_hbm.at[e], wgu_f32.at[slot], wsems.at[0, slot]),
                pltpu.make_async_copy(wd_hbm.at[e], wd_f32.at[slot], wsems.at[1, slot]))

    @pl.when(used & ((c == 0) | (ce_ref[c] != ce_ref[jnp.maximum(c - 1, 0)])))
    def _():
        slot = slot_ref[c]

        @pl.when(c == 0)
        def _():
            for cp in weight_copies(ce_ref[0], slot):
                cp.start()

        for cp in weight_copies(ce_ref[c], slot):
            cp.wait()

        @pl.when(nxt_ref[c] >= 0)
        def _():
            for cp in weight_copies(nxt_ref[c], 1 - slot):
                cp.start()

        wgu_bf[...] = wgu_f32[slot].astype(BF16)
        wd_bf[...] = wd_f32[slot].astype(BF16)

    @pl.when(used)
    def _():
        x = jnp.concatenate(
            [xs_ref[pl.ds(s, CHUNK, stride=ROW_TILES), :] for s in range(ROW_TILES)], axis=-1)
        gu = _dot(x.astype(BF16), wgu_bf[...]) + bgu_ref[0]
        gate = jnp.minimum(gu[:, :D_EXPERT], SWIGLU_LIMIT)
        up = jnp.clip(gu[:, D_EXPERT:], -SWIGLU_LIMIT, SWIGLU_LIMIT)
        h = (up + 1.0) * gate * jax.nn.sigmoid(SWIGLU_ALPHA * gate)
        y = _dot(h.astype(BF16), wd_bf[...]) + bd_ref[0]
        for s in range(ROW_TILES):
            ys_ref[pl.ds(s, CHUNK, stride=ROW_TILES), :] = y[:, s * LANES:(s + 1) * LANES]

    @pl.when(c >= nu_ref[0])
    def _():
        ys_ref[...] = jnp.zeros_like(ys_ref)


def _moe(chunk_e, n_used, run_slot, next_e, xs, wgu, bgu, wd, bd):
    n_chunks = xs.shape[0] // (CHUNK * ROW_TILES)
    d = D_MODEL

    def cc(c, ce, nu, *_):
        return jnp.minimum(c, nu[0] - 1)

    rows = pl.BlockSpec((CHUNK * ROW_TILES, LANES), lambda c, *s: (cc(c, *s), 0))
    bspec = lambda shape: pl.BlockSpec(shape, lambda c, *s: (s[0][cc(c, *s)], 0, 0))
    anyspec = pl.BlockSpec(memory_space=pl.ANY)
    return pl.pallas_call(
        _moe_kernel,
        grid_spec=pltpu.PrefetchScalarGridSpec(
            num_scalar_prefetch=4, grid=(n_chunks,),
            in_specs=[rows, anyspec, bspec((1, 1, 2 * D_EXPERT)), anyspec, bspec((1, 1, d))],
            out_specs=pl.BlockSpec((CHUNK * ROW_TILES, LANES), lambda c, *s: (c, 0)),
            scratch_shapes=[pltpu.VMEM((2, d, 2 * D_EXPERT), F32), pltpu.VMEM((2, D_EXPERT, d), F32),
                            pltpu.VMEM((d, 2 * D_EXPERT), BF16), pltpu.VMEM((D_EXPERT, d), BF16),
                            pltpu.SemaphoreType.DMA((2, 2))]),
        out_shape=jax.ShapeDtypeStruct(xs.shape, F32),
        compiler_params=pltpu.CompilerParams(dimension_semantics=("arbitrary",),
                                             vmem_limit_bytes=VMEM_LIMIT_BYTES),
        name="moe",
    )(chunk_e, n_used, run_slot, next_e, xs, wgu, bgu, wd, bd)


def _fin_kernel(didx_ref, ys_ref, x1_ref, w_ref, p_ref, wpg_ref, wpp_ref, g2_ref, b2_ref,
                out_ref, rows_a, rows_b, sems):
    ts = TS_FIN
    i = pl.program_id(0)
    last = pl.num_programs(0) - 1
    bufs = (rows_a, rows_b)

    def gather_row(par, tile_base, k, t):
        _row_copy(ys_ref, didx_ref[tile_base + k * ts + t], bufs[par].at[k], t, 1,
                  sems.at[par]).start(priority=k % 2)

    def wait_rows(par):
        for k in range(TOP_K):
            _row_copy(ys_ref, 0, bufs[par].at[k], 0, ts, sems.at[par]).wait()

    @pl.when(i == 0)
    def _():
        def issue(tb, carry):
            for u in range(ISSUE_UNROLL):
                for k in range(TOP_K):
                    gather_row(0, 0, k, tb * ISSUE_UNROLL + u)
            return carry

        lax.fori_loop(0, ts // ISSUE_UNROLL, issue, 0)

    next_base = jnp.minimum(i + 1, last) * (TOP_K * ts)

    for par in range(2):
        @pl.when(i % 2 == par)
        def _():
            wait_rows(par)
            for t in range(ts):
                for k in range(TOP_K):
                    gather_row(1 - par, next_base, k, t)

            x1 = x1_ref[...]
            ple = (jax.nn.sigmoid(_dot(x1.astype(BF16), wpg_ref[...]))
                   * _dot(p_ref[...].astype(BF16), wpp_ref[...]))
            z = DN_ALPHA * x1 + ple
            w = w_ref[...]
            for k in range(TOP_K):
                yk = jnp.concatenate(
                    [bufs[par][k, pl.ds(s, ts, stride=ROW_TILES), :] for s in range(ROW_TILES)],
                    axis=-1)
                z = z + w[:, k:k + 1] * yk
            out_ref[...] = _layer_norm(z, g2_ref[...], b2_ref[...])

            @pl.when(i == last)
            def _():
                wait_rows(1 - par)


def _fin(dest, ys, x1, w_rows, p2, wpg, wpp, g2, b2):
    n, d = x1.shape
    nt = n // TS_FIN
    didx = _tiled_index(dest, TS_FIN)
    const = lambda shape: pl.BlockSpec(shape, lambda i, s: (0,) * len(shape))
    tile = lambda width: pl.BlockSpec((TS_FIN, width), lambda i, s: (i, 0))
    return pl.pallas_call(
        _fin_kernel,
        grid_spec=pltpu.PrefetchScalarGridSpec(
            num_scalar_prefetch=1, grid=(nt,),
            in_specs=[pl.BlockSpec(memory_space=pl.ANY), tile(d), tile(TOP_K), tile(D_PLE),
                      const((d, d)), const((D_PLE, d)), const((1, d)), const((1, d))],
            out_specs=tile(d),
            scratch_shapes=[pltpu.VMEM((TOP_K, TS_FIN * ROW_TILES, LANES), F32),
                            pltpu.VMEM((TOP_K, TS_FIN * ROW_TILES, LANES), F32),
                            pltpu.SemaphoreType.DMA((2,))]),
        out_shape=jax.ShapeDtypeStruct((n, d), F32),
        compiler_params=pltpu.CompilerParams(dimension_semantics=("arbitrary",),
                                             vmem_limit_bytes=VMEM_LIMIT_BYTES),
        name="fin",
    )(didx, ys, x1, w_rows, p2, wpg, wpp, g2, b2)


def _permute_w_in(w_in):
    d = D_MODEL
    b_, c_, h_, q_ = (w_in[:, i * d:(i + 1) * d] for i in range(4))
    k_ = w_in[:, 4 * d:4 * d + KV_DIM]
    v_ = w_in[:, 4 * d + KV_DIM:4 * d + 2 * KV_DIM]
    gc_ = w_in[:, 4 * d + 2 * KV_DIM:5 * d + 2 * KV_DIM]
    ga_ = w_in[:, 5 * d + 2 * KV_DIM:]
    return jnp.concatenate([b_, c_, h_, q_, gc_, ga_, k_, v_], axis=1).astype(BF16)


def _replication_matrix():
    src = jnp.arange(KV_DIM)[:, None]
    dst = jnp.arange(D_MODEL)[None, :]
    same_head = (dst // (GQA_GROUP * HEAD_DIM)) == (src // HEAD_DIM)
    same_dim = (dst % HEAD_DIM) == (src % HEAD_DIM)
    return (same_head & same_dim).astype(BF16)


def kernel(x, p, w_in, conv_w, w_br_conv, w_br_attn, attn_sinks, w_out, ln1_g, ln1_b, w_router,
           b_router, w_gu, b_gu, w_down, b_down, w_ple_proj, w_ple_gate, ln2_g, ln2_b):
    bsz, seq, d = x.shape
    n = bsz * seq
    for i in range(DEPTH):
        wr_t = w_router[i].T
        wr_hi = wr_t.astype(BF16)
        wr_lo = (wr_t - wr_hi.astype(F32)).astype(BF16)
        x1, x1t, idx, w_top, rank, cnt = _mix(
            x, _permute_w_in(w_in[i]), _replication_matrix(), conv_w[i],
            w_br_conv[i].astype(BF16), w_br_attn[i].astype(BF16), attn_sinks[i],
            w_out[i].astype(BF16), ln1_g[i][None, :], ln1_b[i][None, :],
            wr_hi, wr_lo, b_router[i][:, None])
        x1 = x1.reshape(n, d)

        counts = cnt[:, 0]
        padded = ((counts + CHUNK - 1) // CHUNK) * CHUNK
        end_padded = jnp.cumsum(padded)
        start_padded = end_padded - padded
        p_rows = n * TOP_K + N_EXPERTS * CHUNK
        n_chunks = p_rows // CHUNK
        chunk_start = jnp.arange(n_chunks, dtype=I32) * CHUNK
        chunk_e = jnp.minimum(
            jnp.sum((end_padded[None, :] <= chunk_start[:, None]).astype(I32), axis=1),
            N_EXPERTS - 1)
        n_used = (end_padded[-1:] // CHUNK).astype(I32)

        dest = _dest(start_padded.astype(I32), idx, rank)
        xs = _disp(dest, (start_padded + counts).astype(I32), (padded - counts).astype(I32),
                   n_used, x1t, p_rows)
        eid = jnp.arange(N_EXPERTS, dtype=I32)
        active = padded > 0
        run_idx = jnp.cumsum(active.astype(I32)) - 1
        later = active[None, :] & (eid[None, :] > eid[:, None])
        nxt = jnp.min(jnp.where(later, eid[None, :], N_EXPERTS), axis=1)
        nxt = jnp.where(nxt == N_EXPERTS, -1, nxt)
        onehot = (chunk_e[:, None] == eid[None, :]).astype(I32)
        run_slot = jnp.sum(onehot * (run_idx & 1)[None, :], axis=1).astype(I32)
        next_e = jnp.sum(onehot * nxt[None, :], axis=1).astype(I32)
        ys = _moe(chunk_e, n_used, run_slot, next_e, xs, w_gu[i], b_gu[i][:, None, :],
                  w_down[i], b_down[i][:, None, :])
        out = _fin(dest, ys, x1, w_top.T, p[i].reshape(n, D_PLE),
                   w_ple_gate[i].astype(BF16), w_ple_proj[i].astype(BF16),
                   ln2_g[i][None, :], ln2_b[i][None, :])
        x = out.reshape(bsz, seq, d)
    return x
```

```python
import math

import jax
import jax.numpy as jnp
from jax import lax
from jax.experimental import pallas as pl
from jax.experimental.pallas import tpu as pltpu

F32 = jnp.float32
BF16 = jnp.bfloat16
I32 = jnp.int32

SUBLANES = 8
LANES = 128
VMEM_LIMIT_BYTES = 56 * 1024 * 1024

D_MODEL = 1024
HEAD_DIM = 64
N_Q_HEADS = 16
N_KV_HEADS = 4
GQA_GROUP = N_Q_HEADS // N_KV_HEADS
KV_DIM = N_KV_HEADS * HEAD_DIM
WINDOW = 128
N_EXPERTS = 32
TOP_K = 4
D_EXPERT = D_MODEL
SWIGLU_LIMIT = 7.0
SWIGLU_ALPHA = 1.702
D_PLE = 256
DEPTH = 1
DN_ALPHA = (2.0 * DEPTH) ** 0.25
LN_EPS = 1e-5
ROW_TILES = D_MODEL // LANES

OFF_B, OFF_C, OFF_H, OFF_Q, OFF_GC, OFF_GA = (i * D_MODEL for i in range(6))
OFF_K = 6 * D_MODEL
OFF_V = OFF_K + KV_DIM
IN_TOTAL = OFF_V + KV_DIM

TS_MIX = 512
TN_DEST = 2048
TD_DISP = 1024
CHUNK = 512
MOE_ROW_VARIANTS = (128, 256, CHUNK)
TS_FIN = 256
ISSUE_UNROLL = 4


def _layer_norm(z, g, b):
    mu = jnp.mean(z, axis=-1, keepdims=True)
    zc = z - mu
    var = jnp.mean(zc * zc, axis=-1, keepdims=True)
    return zc * lax.rsqrt(var + LN_EPS) * g + b


def _dot(a, b):
    return jnp.dot(a, b, preferred_element_type=F32)


def _dot_nt(a, b):
    return lax.dot_general(a, b, (((1,), (1,)), ((), ())), preferred_element_type=F32)


def _mix_kernel(sinks_ref, x_ref, win_ref, rep_ref, convw_ref, wbrc_ref, wbra_ref, wout_ref,
                g1_ref, b1_ref, whi_ref, wlo_ref, br_ref,
                x1_ref, x1t_ref, idx_ref, w_ref, rank_ref, cnt_ref,
                kext, vext, uext, upper, carry, xprev):
    ts = TS_MIX
    j = pl.program_id(1)
    tile_id = pl.program_id(0) * pl.num_programs(1) + j
    last_tile = idx_ref.shape[1] // ts - 1

    @pl.when(j == 0)
    def _():
        kext[0:WINDOW, :] = jnp.zeros((WINDOW, D_MODEL), BF16)
        vext[0:WINDOW, :] = jnp.zeros((WINDOW, D_MODEL), BF16)
        uext[0:SUBLANES, :] = jnp.zeros((SUBLANES, D_MODEL), F32)

    @pl.when(tile_id == 0)
    def _():
        carry[...] = jnp.zeros_like(carry)
        xprev[...] = jnp.zeros_like(xprev)
        r = lax.broadcasted_iota(I32, (ts, ts), 0)
        c = lax.broadcasted_iota(I32, (ts, ts), 1)
        upper[...] = (r < c).astype(BF16)

    route = lambda x1_tile, tile, live: _route_tile(
        x1_tile, tile, live, whi_ref, wlo_ref, br_ref, upper, carry, idx_ref, w_ref, rank_ref, cnt_ref)
    route(xprev[...], jnp.maximum(tile_id - 1, 0), tile_id >= 1)

    x = x_ref[0]
    xb = x.astype(BF16)

    def proj(off, width):
        return _dot(xb, win_ref[:, off:off + width])

    u = proj(OFF_C, D_MODEL) * proj(OFF_H, D_MODEL)
    uext[SUBLANES:SUBLANES + ts, :] = u
    y = (convw_ref[2:3, :] * u
         + convw_ref[1:2, :] * uext[SUBLANES - 1:SUBLANES - 1 + ts, :]
         + convw_ref[0:1, :] * uext[SUBLANES - 2:SUBLANES - 2 + ts, :])
    uext[0:SUBLANES, :] = u[ts - SUBLANES:ts, :]
    yc_in = proj(OFF_B, D_MODEL) * y
    y_conv = _dot(yc_in.astype(BF16), wbrc_ref[...])
    acc = jax.nn.sigmoid(proj(OFF_GC, D_MODEL)) * y_conv

    q = (proj(OFF_Q, D_MODEL) * (1.0 / math.sqrt(HEAD_DIM))).astype(BF16)
    kb = proj(OFF_K, KV_DIM).astype(BF16)
    vb = proj(OFF_V, KV_DIM).astype(BF16)
    kext[WINDOW:WINDOW + ts, :] = _dot(kb, rep_ref[...]).astype(BF16)
    vext[WINDOW:WINDOW + ts, :] = _dot(vb, rep_ref[...]).astype(BF16)

    grp = GQA_GROUP * HEAD_DIM
    row = lax.broadcasted_iota(I32, (WINDOW, 2 * WINDOW), 0)
    col = lax.broadcasted_iota(I32, (WINDOW, 2 * WINDOW), 1)
    band = (col > row) & (col <= row + WINDOW)
    q_lane_grp = lax.broadcasted_iota(I32, (WINDOW, grp), 1) // HEAD_DIM
    o_blocks = []
    for i in range(ts // WINDOW):
        if i == 0:
            mask = band & ((col >= WINDOW) | (j > 0))
        else:
            mask = band
        o_heads = []
        for h in range(N_KV_HEADS):
            qh = q[i * WINDOW:(i + 1) * WINDOW, h * grp:(h + 1) * grp]
            kh = kext[i * WINDOW:i * WINDOW + 2 * WINDOW, h * grp:(h + 1) * grp]
            vh = vext[i * WINDOW:i * WINDOW + 2 * WINDOW, h * grp:(h + 1) * grp]
            qs = jnp.concatenate(
                [jnp.where(q_lane_grp == g, qh, jnp.zeros_like(qh)) for g in range(GQA_GROUP)], axis=0)
            s_all = _dot_nt(qs, kh)
            ps = []
            for g in range(GQA_GROUP):
                s = jnp.where(mask, s_all[g * WINDOW:(g + 1) * WINDOW], -jnp.inf)
                sink = sinks_ref[h * GQA_GROUP + g]
                m = jnp.maximum(jnp.max(s, axis=-1, keepdims=True), sink)
                e = jnp.exp(s - m)
                den = jnp.sum(e, axis=-1, keepdims=True) + jnp.exp(sink - m)
                ps.append((e / den).astype(BF16))
            pv = _dot(jnp.concatenate(ps, axis=0), vh)
            oh = jnp.zeros((WINDOW, grp), F32)
            for g in range(GQA_GROUP):
                oh = jnp.where(q_lane_grp == g, pv[g * WINDOW:(g + 1) * WINDOW], oh)
            o_heads.append(oh)
        o_blocks.append(jnp.concatenate(o_heads, axis=-1))
    o = jnp.concatenate(o_blocks, axis=0)
    kext[0:WINDOW, :] = kext[ts:ts + WINDOW, :]
    vext[0:WINDOW, :] = vext[ts:ts + WINDOW, :]

    y_attn = _dot(o.astype(BF16), wbra_ref[...])
    acc = acc + jax.nn.sigmoid(proj(OFF_GA, D_MODEL)) * y_attn

    z = DN_ALPHA * x + _dot(acc.astype(BF16), wout_ref[...])
    x1 = _layer_norm(z, g1_ref[...], b1_ref[...])
    x1_ref[0] = x1
    for s in range(ROW_TILES):
        x1t_ref[pl.ds(s, ts, stride=ROW_TILES), :] = x1[:, s * LANES:(s + 1) * LANES]

    xprev[...] = x1

    @pl.when(tile_id == last_tile)
    def _():
        route(x1, tile_id, True)


def _route_tile(x1, tile, live, whi_ref, wlo_ref, br_ref, upper, carry, idx_ref, w_ref, rank_ref,
                cnt_ref):
    ts = x1.shape[0]
    cols = pl.ds(pl.multiple_of(tile * ts, ts), ts)
    xh = x1.astype(BF16)
    xl = (x1 - xh.astype(F32)).astype(BF16)
    whi = whi_ref[...]
    logits = _dot_nt(whi, xh) + _dot_nt(whi, xl) + _dot_nt(wlo_ref[...], xh) + br_ref[...]

    eid = lax.broadcasted_iota(I32, (N_EXPERTS, ts), 0)
    rest = logits
    sels, vals = [], []
    for k in range(TOP_K):
        m = jnp.max(rest, axis=0, keepdims=True)
        idx = jnp.min(jnp.where(rest == m, eid, N_EXPERTS), axis=0, keepdims=True)
        sel = eid == idx
        rest = jnp.where(sel, -jnp.inf, rest)
        sels.append(sel)
        vals.append(m)
        idx_ref[k:k + 1, cols] = idx
    exps = [jnp.exp(v - vals[0]) for v in vals]
    den = exps[0] + exps[1] + exps[2] + exps[3]
    for k in range(TOP_K):
        w_ref[k:k + 1, cols] = exps[k] / den

    member = jnp.zeros((N_EXPERTS, ts), F32)
    for sel in sels:
        member = member + sel.astype(F32)
    cum = _dot(member.astype(BF16), upper[...]) + carry[:, 0:1]
    for k in range(TOP_K):
        rk = jnp.sum(jnp.where(sels[k], cum, 0.0), axis=0, keepdims=True)
        rank_ref[k:k + 1, cols] = rk.astype(I32)
    carry[...] = carry[...] + jnp.where(live, jnp.sum(member, axis=1, keepdims=True), 0.0)
    cnt_ref[...] = carry[...].astype(I32)


def _mix(x, w_in_p, rep, conv_w, wbrc, wbra, sinks, wout, g1, b1, whi, wlo, br):
    bsz, seq, d = x.shape
    n = bsz * seq
    nj = seq // TS_MIX
    const = lambda shape: pl.BlockSpec(shape, lambda b, j, s: (0,) * len(shape),
                                       pipeline_mode=pl.Buffered(1))
    tok = pl.BlockSpec((TOP_K, n), lambda b, j, s: (0, 0))
    grid_spec = pltpu.PrefetchScalarGridSpec(
        num_scalar_prefetch=1,
        grid=(bsz, nj),
        in_specs=[
            pl.BlockSpec((1, TS_MIX, d), lambda b, j, s: (b, j, 0)),
            const((d, IN_TOTAL)),
            const((KV_DIM, d)),
            const((3, d)),
            const((d, d)),
            const((d, d)),
            const((d, d)),
            const((1, d)),
            const((1, d)),
            const((N_EXPERTS, d)),
            const((N_EXPERTS, d)),
            const((N_EXPERTS, 1)),
        ],
        out_specs=[
            pl.BlockSpec((1, TS_MIX, d), lambda b, j, s: (b, j, 0)),
            pl.BlockSpec((TS_MIX * ROW_TILES, LANES), lambda b, j, s: (b * nj + j, 0)),
            tok, tok, tok,
            pl.BlockSpec((N_EXPERTS, LANES), lambda b, j, s: (0, 0)),
        ],
        scratch_shapes=[
            pltpu.VMEM((WINDOW + TS_MIX, d), BF16),
            pltpu.VMEM((WINDOW + TS_MIX, d), BF16),
            pltpu.VMEM((SUBLANES + TS_MIX, d), F32),
            pltpu.VMEM((TS_MIX, TS_MIX), BF16),
            pltpu.VMEM((N_EXPERTS, LANES), F32),
            pltpu.VMEM((TS_MIX, d), F32),
        ],
    )
    return pl.pallas_call(
        _mix_kernel,
        grid_spec=grid_spec,
        out_shape=[jax.ShapeDtypeStruct((bsz, seq, d), F32),
                   jax.ShapeDtypeStruct((n * ROW_TILES, LANES), F32),
                   jax.ShapeDtypeStruct((TOP_K, n), I32),
                   jax.ShapeDtypeStruct((TOP_K, n), F32),
                   jax.ShapeDtypeStruct((TOP_K, n), I32),
                   jax.ShapeDtypeStruct((N_EXPERTS, LANES), I32)],
        compiler_params=pltpu.CompilerParams(
            dimension_semantics=("arbitrary", "arbitrary"),
            vmem_limit_bytes=VMEM_LIMIT_BYTES),
        name="mix",
    )(sinks, x, w_in_p, rep, conv_w, wbrc, wbra, wout, g1, b1, whi, wlo, br)


def _dest_kernel(start_ref, idx_ref, rank_ref, dest_ref):
    idx = idx_ref[...]
    acc = rank_ref[...]
    for e in range(N_EXPERTS):
        acc = acc + jnp.where(idx == e, start_ref[e], 0)
    dest_ref[...] = acc


def _dest(start_padded, idx, rank):
    k, n = idx.shape
    tok = pl.BlockSpec((k, TN_DEST), lambda i, s: (0, i))
    return pl.pallas_call(
        _dest_kernel,
        grid_spec=pltpu.PrefetchScalarGridSpec(
            num_scalar_prefetch=1, grid=(n // TN_DEST,), in_specs=[tok, tok], out_specs=tok),
        out_shape=jax.ShapeDtypeStruct((k, n), I32),
        name="dest",
    )(start_padded, idx, rank)


def _row_copy(src, src_row, dst, dst_row, rows, sem):
    s0 = pl.multiple_of(src_row * ROW_TILES, ROW_TILES)
    d0 = pl.multiple_of(dst_row * ROW_TILES, ROW_TILES)
    return pltpu.make_async_copy(src.at[pl.ds(s0, rows * ROW_TILES)],
                                 dst.at[pl.ds(d0, rows * ROW_TILES)], sem)


def _pad_pieces():
    return [1 << b for b in reversed(range(CHUNK.bit_length() - 1))]


def _index_copy(idx_hbm, tile, idx_smem, slot, isems, tile_tokens):
    n_idx = TOP_K * tile_tokens
    last = idx_hbm.shape[0] // n_idx - 1
    src = pl.multiple_of(jnp.minimum(tile, last) * n_idx, n_idx)
    return pltpu.make_async_copy(idx_hbm.at[pl.ds(src, n_idx)],
                                 idx_smem.at[pl.ds(slot * n_idx, n_idx)], isems.at[slot])


def _disp_kernel(padpos_ref, padlen_ref, nu_ref, didx_hbm, x1t_hbm, xs_ref, zbuf, xbuf, idx_smem, sems,
                 zsem, isems, xsems):
    half = CHUNK // 2
    n_half = xs_ref.shape[0] // (half * ROW_TILES)
    i = pl.program_id(0)
    last = pl.num_programs(0) - 1
    slot = i % 2

    def tile_copy(tile):
        ring = tile % 3
        return _row_copy(x1t_hbm, jnp.minimum(tile, last) * TD_DISP, xbuf, ring * TD_DISP, TD_DISP,
                         xsems.at[ring])

    @pl.when(i == 0)
    def _():
        tile_copy(0).start()
        _index_copy(didx_hbm, 0, idx_smem, 0, isems, TD_DISP).start()

    tile_copy(i).wait()
    tile_copy(i + 1).start()
    _index_copy(didx_hbm, i, idx_smem, slot, isems, TD_DISP).wait()
    _index_copy(didx_hbm, i + 1, idx_smem, 1 - slot, isems, TD_DISP).start()
    base = slot * (TOP_K * TD_DISP)
    src0 = (i % 3) * TD_DISP

    def issue(tb, carry):
        for u in range(ISSUE_UNROLL):
            t = tb * ISSUE_UNROLL + u
            for k in range(TOP_K):
                _row_copy(xbuf, src0 + t, xs_ref, idx_smem[base + k * TD_DISP + t], 1,
                          sems.at[slot]).start(priority=k % 2)
        return carry

    lax.fori_loop(0, TD_DISP // ISSUE_UNROLL, issue, 0)

    @pl.when(i == last)
    def _():
        tile_copy(i + 1).wait()
        _index_copy(didx_hbm, i + 1, idx_smem, 1 - slot, isems, TD_DISP).wait()

    @pl.when(i == 0)
    def _():
        zbuf[...] = jnp.zeros_like(zbuf)
        for wait in (False, True):
            for e in range(N_EXPERTS):
                npad = padlen_ref[e]
                for piece in _pad_pieces():
                    pos = padpos_ref[e] + (npad & ~(2 * piece - 1))
                    cp = _row_copy(zbuf, 0, xs_ref, pos, piece, zsem)

                    @pl.when((npad & piece) != 0)
                    def _():
                        cp.wait() if wait else cp.start()

            def tail(hc, carry):
                cp = _row_copy(zbuf, 0, xs_ref, hc * half, half, zsem)
                cp.wait() if wait else cp.start()
                return carry

            lax.fori_loop(2 * nu_ref[0], n_half, tail, 0)

    def drain(par):
        for k in range(TOP_K):
            _row_copy(xbuf, 0, xs_ref, 0, TD_DISP, sems.at[par]).wait()

    @pl.when(i >= 1)
    def _():
        drain(1 - slot)

    @pl.when(i == last)
    def _():
        drain(slot)


def _tiled_index(dest, tile_tokens):
    k, n = dest.shape
    return dest.reshape(k, n // tile_tokens, tile_tokens).transpose(1, 0, 2).reshape(k * n)


def _disp(dest, padpos, padlen, n_used, x1t, p_rows):
    n = dest.shape[1]
    didx = _tiled_index(dest, TD_DISP)
    anyspec = pl.BlockSpec(memory_space=pl.ANY)
    return pl.pallas_call(
        _disp_kernel,
        grid_spec=pltpu.PrefetchScalarGridSpec(
            num_scalar_prefetch=3, grid=(n // TD_DISP,),
            in_specs=[anyspec, anyspec],
            out_specs=anyspec,
            scratch_shapes=[pltpu.VMEM((CHUNK // 2 * ROW_TILES, LANES), F32),
                            pltpu.VMEM((3 * TD_DISP * ROW_TILES, LANES), F32),
                            pltpu.SMEM((2 * TOP_K * TD_DISP,), I32),
                            pltpu.SemaphoreType.DMA((2,)), pltpu.SemaphoreType.DMA,
                            pltpu.SemaphoreType.DMA((2,)), pltpu.SemaphoreType.DMA((3,))]),
        out_shape=jax.ShapeDtypeStruct((p_rows * ROW_TILES, LANES), F32),
        compiler_params=pltpu.CompilerParams(dimension_semantics=("arbitrary",),
                                             vmem_limit_bytes=VMEM_LIMIT_BYTES,
                                             has_side_effects=True),
        name="disp",
    )(padpos, padlen, n_used, didx, x1t)


def _moe_kernel(ce_ref, nu_ref, slot_ref, nxt_ref, fill_ref, xs_ref, wgu_hbm, bgu_ref, wd_hbm, bd_ref, ys_ref,
                wgu_f32, wd_f32, wgu_bf, wd_bf, wsems):
    c = pl.program_id(0)
    used = c < nu_ref[0]

    def weight_copies(e, slot):
        return (pltpu.make_async_copy(wgu_hbm.at[e], wgu_f32.at[slot], wsems.at[0, slot]),
                pltpu.make_async_copy(wd_hbm.at[e], wd_f32.at[slot], wsems.at[1, slot]))

    @pl.when(used & ((c == 0) | (ce_ref[c] != ce_ref[jnp.maximum(c - 1, 0)])))
    def _():
        slot = slot_ref[c]

        @pl.when(c == 0)
        def _():
            for cp in weight_copies(ce_ref[0], slot):
                cp.start()

        for cp in weight_copies(ce_ref[c], slot):
            cp.wait()

        @pl.when(nxt_ref[c] >= 0)
        def _():
            for cp in weight_copies(nxt_ref[c], 1 - slot):
                cp.start()

        wgu_bf[...] = wgu_f32[slot].astype(BF16)
        wd_bf[...] = wd_f32[slot].astype(BF16)

    def expert(rows):
        x = jnp.concatenate(
            [xs_ref[pl.ds(s, rows, stride=ROW_TILES), :] for s in range(ROW_TILES)], axis=-1)
        gu = _dot(x.astype(BF16), wgu_bf[...]) + bgu_ref[0]
        gate = jnp.minimum(gu[:, :D_EXPERT], SWIGLU_LIMIT)
        up = jnp.clip(gu[:, D_EXPERT:], -SWIGLU_LIMIT, SWIGLU_LIMIT)
        h = (up + 1.0) * gate * jax.nn.sigmoid(SWIGLU_ALPHA * gate)
        y = _dot(h.astype(BF16), wd_bf[...]) + bd_ref[0]
        for s in range(ROW_TILES):
            ys_ref[pl.ds(s, rows, stride=ROW_TILES), :] = y[:, s * LANES:(s + 1) * LANES]
        if rows < CHUNK:
            ys_ref[rows * ROW_TILES:, :] = jnp.zeros(((CHUNK - rows) * ROW_TILES, LANES), F32)

    filled = jnp.where(used, fill_ref[jnp.minimum(c, fill_ref.shape[0] - 1)], 0)
    lower = 0
    for rows in MOE_ROW_VARIANTS:
        @pl.when((filled > lower) & (filled <= rows))
        def _():
            expert(rows)

        lower = rows

    @pl.when(filled == 0)
    def _():
        ys_ref[...] = jnp.zeros_like(ys_ref)


def _moe(chunk_e, n_used, run_slot, next_e, chunk_fill, xs, wgu, bgu, wd, bd):
    n_chunks = xs.shape[0] // (CHUNK * ROW_TILES)
    d = D_MODEL

    def cc(c, ce, nu, *_):
        return jnp.minimum(c, nu[0] - 1)

    rows = pl.BlockSpec((CHUNK * ROW_TILES, LANES), lambda c, *s: (cc(c, *s), 0))
    bspec = lambda shape: pl.BlockSpec(shape, lambda c, *s: (s[0][cc(c, *s)], 0, 0))
    anyspec = pl.BlockSpec(memory_space=pl.ANY)
    return pl.pallas_call(
        _moe_kernel,
        grid_spec=pltpu.PrefetchScalarGridSpec(
            num_scalar_prefetch=5, grid=(n_chunks,),
            in_specs=[rows, anyspec, bspec((1, 1, 2 * D_EXPERT)), anyspec, bspec((1, 1, d))],
            out_specs=pl.BlockSpec((CHUNK * ROW_TILES, LANES), lambda c, *s: (c, 0)),
            scratch_shapes=[pltpu.VMEM((2, d, 2 * D_EXPERT), F32), pltpu.VMEM((2, D_EXPERT, d), F32),
                            pltpu.VMEM((d, 2 * D_EXPERT), BF16), pltpu.VMEM((D_EXPERT, d), BF16),
                            pltpu.SemaphoreType.DMA((2, 2))]),
        out_shape=jax.ShapeDtypeStruct(xs.shape, F32),
        compiler_params=pltpu.CompilerParams(dimension_semantics=("arbitrary",),
                                             vmem_limit_bytes=VMEM_LIMIT_BYTES),
        name="moe",
    )(chunk_e, n_used, run_slot, next_e, chunk_fill, xs, wgu, bgu, wd, bd)


def _fin_kernel(didx_ref, ys_ref, x1_ref, w_ref, p_ref, wpg_ref, wpp_ref, g2_ref, b2_ref,
                out_ref, rows_a, rows_b, sems):
    ts = TS_FIN
    i = pl.program_id(0)
    last = pl.num_programs(0) - 1
    bufs = (rows_a, rows_b)

    def gather_row(par, tile_base, k, t):
        _row_copy(ys_ref, didx_ref[tile_base + k * ts + t], bufs[par].at[k], t, 1,
                  sems.at[par]).start(priority=k % 2)

    def wait_rows(par):
        for k in range(TOP_K):
            _row_copy(ys_ref, 0, bufs[par].at[k], 0, ts, sems.at[par]).wait()

    @pl.when(i == 0)
    def _():
        def issue(tb, carry):
            for u in range(ISSUE_UNROLL):
                for k in range(TOP_K):
                    gather_row(0, 0, k, tb * ISSUE_UNROLL + u)
            return carry

        lax.fori_loop(0, ts // ISSUE_UNROLL, issue, 0)

    next_base = jnp.minimum(i + 1, last) * (TOP_K * ts)

    for par in range(2):
        @pl.when(i % 2 == par)
        def _():
            wait_rows(par)
            for t in range(ts):
                for k in range(TOP_K):
                    gather_row(1 - par, next_base, k, t)

            x1 = x1_ref[...]
            ple = (jax.nn.sigmoid(_dot(x1.astype(BF16), wpg_ref[...]))
                   * _dot(p_ref[...].astype(BF16), wpp_ref[...]))
            z = DN_ALPHA * x1 + ple
            w = w_ref[...]
            for k in range(TOP_K):
                yk = jnp.concatenate(
                    [bufs[par][k, pl.ds(s, ts, stride=ROW_TILES), :] for s in range(ROW_TILES)],
                    axis=-1)
                z = z + w[:, k:k + 1] * yk
            out_ref[...] = _layer_norm(z, g2_ref[...], b2_ref[...])

            @pl.when(i == last)
            def _():
                wait_rows(1 - par)


def _fin(dest, ys, x1, w_rows, p2, wpg, wpp, g2, b2):
    n, d = x1.shape
    nt = n // TS_FIN
    didx = _tiled_index(dest, TS_FIN)
    const = lambda shape: pl.BlockSpec(shape, lambda i, s: (0,) * len(shape))
    tile = lambda width: pl.BlockSpec((TS_FIN, width), lambda i, s: (i, 0))
    return pl.pallas_call(
        _fin_kernel,
        grid_spec=pltpu.PrefetchScalarGridSpec(
            num_scalar_prefetch=1, grid=(nt,),
            in_specs=[pl.BlockSpec(memory_space=pl.ANY), tile(d), tile(TOP_K), tile(D_PLE),
                      const((d, d)), const((D_PLE, d)), const((1, d)), const((1, d))],
            out_specs=tile(d),
            scratch_shapes=[pltpu.VMEM((TOP_K, TS_FIN * ROW_TILES, LANES), F32),
                            pltpu.VMEM((TOP_K, TS_FIN * ROW_TILES, LANES), F32),
                            pltpu.SemaphoreType.DMA((2,))]),
        out_shape=jax.ShapeDtypeStruct((n, d), F32),
        compiler_params=pltpu.CompilerParams(dimension_semantics=("arbitrary",),
                                             vmem_limit_bytes=VMEM_LIMIT_BYTES),
        name="fin",
    )(didx, ys, x1, w_rows, p2, wpg, wpp, g2, b2)


def _permute_w_in(w_in):
    d = D_MODEL
    b_, c_, h_, q_ = (w_in[:, i * d:(i + 1) * d] for i in range(4))
    k_ = w_in[:, 4 * d:4 * d + KV_DIM]
    v_ = w_in[:, 4 * d + KV_DIM:4 * d + 2 * KV_DIM]
    gc_ = w_in[:, 4 * d + 2 * KV_DIM:5 * d + 2 * KV_DIM]
    ga_ = w_in[:, 5 * d + 2 * KV_DIM:]
    return jnp.concatenate([b_, c_, h_, q_, gc_, ga_, k_, v_], axis=1).astype(BF16)


def _replication_matrix():
    src = jnp.arange(KV_DIM)[:, None]
    dst = jnp.arange(D_MODEL)[None, :]
    same_head = (dst // (GQA_GROUP * HEAD_DIM)) == (src // HEAD_DIM)
    same_dim = (dst % HEAD_DIM) == (src % HEAD_DIM)
    return (same_head & same_dim).astype(BF16)


def kernel(x, p, w_in, conv_w, w_br_conv, w_br_attn, attn_sinks, w_out, ln1_g, ln1_b, w_router,
           b_router, w_gu, b_gu, w_down, b_down, w_ple_proj, w_ple_gate, ln2_g, ln2_b):
    bsz, seq, d = x.shape
    n = bsz * seq
    for i in range(DEPTH):
        wr_t = w_router[i].T
        wr_hi = wr_t.astype(BF16)
        wr_lo = (wr_t - wr_hi.astype(F32)).astype(BF16)
        x1, x1t, idx, w_top, rank, cnt = _mix(
            x, _permute_w_in(w_in[i]), _replication_matrix(), conv_w[i],
            w_br_conv[i].astype(BF16), w_br_attn[i].astype(BF16), attn_sinks[i],
            w_out[i].astype(BF16), ln1_g[i][None, :], ln1_b[i][None, :],
            wr_hi, wr_lo, b_router[i][:, None])
        x1 = x1.reshape(n, d)

        counts = cnt[:, 0]
        padded = ((counts + CHUNK - 1) // CHUNK) * CHUNK
        end_padded = jnp.cumsum(padded)
        start_padded = end_padded - padded
        p_rows = n * TOP_K + N_EXPERTS * CHUNK
        n_chunks = p_rows // CHUNK
        chunk_start = jnp.arange(n_chunks, dtype=I32) * CHUNK
        chunk_e = jnp.minimum(
            jnp.sum((end_padded[None, :] <= chunk_start[:, None]).astype(I32), axis=1),
            N_EXPERTS - 1)
        n_used = (end_padded[-1:] // CHUNK).astype(I32)

        dest = _dest(start_padded.astype(I32), idx, rank)
        xs = _disp(dest, (start_padded + counts).astype(I32), (padded - counts).astype(I32),
                   n_used, x1t, p_rows)
        eid = jnp.arange(N_EXPERTS, dtype=I32)
        active = padded > 0
        run_idx = jnp.cumsum(active.astype(I32)) - 1
        later = active[None, :] & (eid[None, :] > eid[:, None])
        nxt = jnp.min(jnp.where(later, eid[None, :], N_EXPERTS), axis=1)
        nxt = jnp.where(nxt == N_EXPERTS, -1, nxt)
        onehot = (chunk_e[:, None] == eid[None, :]).astype(I32)
        run_slot = jnp.sum(onehot * (run_idx & 1)[None, :], axis=1).astype(I32)
        next_e = jnp.sum(onehot * nxt[None, :], axis=1).astype(I32)
        chunk_fill = jnp.clip(
            jnp.sum(onehot * (counts + start_padded)[None, :], axis=1) - chunk_start, 0, CHUNK).astype(I32)
        ys = _moe(chunk_e, n_used, run_slot, next_e, chunk_fill, xs, w_gu[i], b_gu[i][:, None, :],
                  w_down[i], b_down[i][:, None, :])
        out = _fin(dest, ys, x1, w_top.T, p[i].reshape(n, D_PLE),
                   w_ple_gate[i].astype(BF16), w_ple_proj[i].astype(BF16),
                   ln2_g[i][None, :], ln2_b[i][None, :])
        x = out.reshape(bsz, seq, d)
    return x
```

```python
import math

import jax
import jax.numpy as jnp
from jax import lax
from jax.experimental import pallas as pl
from jax.experimental.pallas import tpu as pltpu

F32 = jnp.float32
BF16 = jnp.bfloat16
I32 = jnp.int32

SUBLANES = 8
LANES = 128
VMEM_LIMIT_BYTES = 56 * 1024 * 1024

D_MODEL = 1024
HEAD_DIM = 64
N_Q_HEADS = 16
N_KV_HEADS = 4
GQA_GROUP = N_Q_HEADS // N_KV_HEADS
KV_DIM = N_KV_HEADS * HEAD_DIM
WINDOW = 128
N_EXPERTS = 32
TOP_K = 4
D_EXPERT = D_MODEL
SWIGLU_LIMIT = 7.0
SWIGLU_ALPHA = 1.702
D_PLE = 256
DEPTH = 1
DN_ALPHA = (2.0 * DEPTH) ** 0.25
LN_EPS = 1e-5
ROW_TILES = D_MODEL // LANES

OFF_B, OFF_C, OFF_H, OFF_Q, OFF_GC, OFF_GA = (i * D_MODEL for i in range(6))
OFF_K = 6 * D_MODEL
OFF_V = OFF_K + KV_DIM
IN_TOTAL = OFF_V + KV_DIM

TS_MIX = 512
TRASH_ROWS = TOP_K * TS_MIX
CHUNK = 512
MOE_ROW_VARIANTS = (128, 256, CHUNK)
TS_FIN = 256
ISSUE_UNROLL = 4


def _layer_norm(z, g, b):
    mu = jnp.mean(z, axis=-1, keepdims=True)
    zc = z - mu
    var = jnp.mean(zc * zc, axis=-1, keepdims=True)
    return zc * lax.rsqrt(var + LN_EPS) * g + b


def _dot(a, b):
    return jnp.dot(a, b, preferred_element_type=F32)


def _dot_nt(a, b):
    return lax.dot_general(a, b, (((1,), (1,)), ((), ())), preferred_element_type=F32)


def _row_copy(src, src_row, dst, dst_row, rows, sem):
    s0 = pl.multiple_of(src_row * ROW_TILES, ROW_TILES)
    d0 = pl.multiple_of(dst_row * ROW_TILES, ROW_TILES)
    return pltpu.make_async_copy(src.at[pl.ds(s0, rows * ROW_TILES)],
                                 dst.at[pl.ds(d0, rows * ROW_TILES)], sem)


def _mix_kernel(sinks_ref, x_ref, win_ref, rep_ref, convw_ref, wbrc_ref, wbra_ref, wout_ref,
                g1_ref, b1_ref, whi_ref, wlo_ref, br_ref,
                x1_ref, idx_ref, w_ref, dest_ref, cnt_ref, alloc_ref, xs_hbm,
                kext, vext, uext, upper, lower, carry, cur_id, nfree, ring, dest_v, dest_s,
                pad_v, pad_s, zbuf, dsems, rsems, psem, zsem):
    ts = TS_MIX
    j = pl.program_id(1)
    tile_id = pl.program_id(0) * pl.num_programs(1) + j
    last_tile = idx_ref.shape[1] // ts - 1
    trash_base = xs_hbm.shape[0] // ROW_TILES - TRASH_ROWS
    cur = tile_id % 2
    oth = 1 - cur

    def ring_tile(tile):
        return ((tile + 3) % 3) * ts

    def load_ring(tile):
        base = pl.multiple_of(ring_tile(tile) * ROW_TILES, ROW_TILES)
        return jnp.concatenate(
            [ring[pl.ds(base + s, ts, stride=ROW_TILES), :] for s in range(ROW_TILES)], axis=-1)

    def dest_copy(slot):
        return pltpu.make_async_copy(dest_v.at[slot], dest_s.at[slot], dsems.at[slot])

    def row_copy(tile, slot, k, t):
        return _row_copy(ring, ring_tile(tile) + t, xs_hbm, dest_s[slot, k, t], 1, rsems.at[slot])

    def wait_rows(slot):
        for k in range(TOP_K):
            _row_copy(ring, 0, xs_hbm, 0, ts, rsems.at[slot]).wait()

    route = lambda x1_tile, tile, live, slot: _route_tile(
        x1_tile, tile, live, slot, trash_base, whi_ref, wlo_ref, br_ref, upper, lower, carry, cur_id,
        nfree, idx_ref, w_ref, dest_ref, cnt_ref, alloc_ref, dest_v)

    @pl.when(j == 0)
    def _():
        kext[0:WINDOW, :] = jnp.zeros((WINDOW, D_MODEL), BF16)
        vext[0:WINDOW, :] = jnp.zeros((WINDOW, D_MODEL), BF16)
        uext[0:SUBLANES, :] = jnp.zeros((SUBLANES, D_MODEL), F32)

    @pl.when(tile_id == 0)
    def _():
        carry[...] = jnp.zeros_like(carry)
        cur_id[...] = jnp.zeros_like(cur_id)
        nfree[...] = jnp.zeros_like(nfree)
        alloc_ref[...] = jnp.full(alloc_ref.shape, -1, I32)
        ring[...] = jnp.zeros_like(ring)
        zbuf[...] = jnp.zeros_like(zbuf)
        r = lax.broadcasted_iota(I32, (ts, ts), 0)
        c = lax.broadcasted_iota(I32, (ts, ts), 1)
        upper[...] = (r < c).astype(BF16)
        r = lax.broadcasted_iota(I32, (N_EXPERTS, N_EXPERTS), 0)
        c = lax.broadcasted_iota(I32, (N_EXPERTS, N_EXPERTS), 1)
        lower[...] = (c < r).astype(BF16)
        for k in range(TOP_K):
            dest_v[0, k:k + 1, :] = _trash_rows(trash_base, k, ts)
        dest_copy(0).start()

    @pl.when(tile_id >= 1)
    def _():
        wait_rows(oth)

    route(load_ring(tile_id - 1), jnp.maximum(tile_id - 1, 0), tile_id >= 1, oth)
    dest_copy(oth).start()
    dest_copy(cur).wait()
    for t in range(ts):
        for k in range(TOP_K):
            row_copy(tile_id - 2, cur, k, t).start(priority=k % 2)

    x = x_ref[0]
    xb = x.astype(BF16)

    def proj(off, width):
        return _dot(xb, win_ref[:, off:off + width])

    u = proj(OFF_C, D_MODEL) * proj(OFF_H, D_MODEL)
    uext[SUBLANES:SUBLANES + ts, :] = u
    y = (convw_ref[2:3, :] * u
         + convw_ref[1:2, :] * uext[SUBLANES - 1:SUBLANES - 1 + ts, :]
         + convw_ref[0:1, :] * uext[SUBLANES - 2:SUBLANES - 2 + ts, :])
    uext[0:SUBLANES, :] = u[ts - SUBLANES:ts, :]
    yc_in = proj(OFF_B, D_MODEL) * y
    y_conv = _dot(yc_in.astype(BF16), wbrc_ref[...])
    acc = jax.nn.sigmoid(proj(OFF_GC, D_MODEL)) * y_conv

    q = (proj(OFF_Q, D_MODEL) * (1.0 / math.sqrt(HEAD_DIM))).astype(BF16)
    kb = proj(OFF_K, KV_DIM).astype(BF16)
    vb = proj(OFF_V, KV_DIM).astype(BF16)
    kext[WINDOW:WINDOW + ts, :] = _dot(kb, rep_ref[...]).astype(BF16)
    vext[WINDOW:WINDOW + ts, :] = _dot(vb, rep_ref[...]).astype(BF16)

    grp = GQA_GROUP * HEAD_DIM
    row = lax.broadcasted_iota(I32, (WINDOW, 2 * WINDOW), 0)
    col = lax.broadcasted_iota(I32, (WINDOW, 2 * WINDOW), 1)
    band = (col > row) & (col <= row + WINDOW)
    q_lane_grp = lax.broadcasted_iota(I32, (WINDOW, grp), 1) // HEAD_DIM
    o_blocks = []
    for i in range(ts // WINDOW):
        if i == 0:
            mask = band & ((col >= WINDOW) | (j > 0))
        else:
            mask = band
        o_heads = []
        for h in range(N_KV_HEADS):
            qh = q[i * WINDOW:(i + 1) * WINDOW, h * grp:(h + 1) * grp]
            kh = kext[i * WINDOW:i * WINDOW + 2 * WINDOW, h * grp:(h + 1) * grp]
            vh = vext[i * WINDOW:i * WINDOW + 2 * WINDOW, h * grp:(h + 1) * grp]
            qs = jnp.concatenate(
                [jnp.where(q_lane_grp == g, qh, jnp.zeros_like(qh)) for g in range(GQA_GROUP)], axis=0)
            s_all = _dot_nt(qs, kh)
            ps = []
            for g in range(GQA_GROUP):
                s = jnp.where(mask, s_all[g * WINDOW:(g + 1) * WINDOW], -jnp.inf)
                sink = sinks_ref[h * GQA_GROUP + g]
                m = jnp.maximum(jnp.max(s, axis=-1, keepdims=True), sink)
                e = jnp.exp(s - m)
                den = jnp.sum(e, axis=-1, keepdims=True) + jnp.exp(sink - m)
                ps.append((e / den).astype(BF16))
            pv = _dot(jnp.concatenate(ps, axis=0), vh)
            oh = jnp.zeros((WINDOW, grp), F32)
            for g in range(GQA_GROUP):
                oh = jnp.where(q_lane_grp == g, pv[g * WINDOW:(g + 1) * WINDOW], oh)
            o_heads.append(oh)
        o_blocks.append(jnp.concatenate(o_heads, axis=-1))
    o = jnp.concatenate(o_blocks, axis=0)
    kext[0:WINDOW, :] = kext[ts:ts + WINDOW, :]
    vext[0:WINDOW, :] = vext[ts:ts + WINDOW, :]

    y_attn = _dot(o.astype(BF16), wbra_ref[...])
    acc = acc + jax.nn.sigmoid(proj(OFF_GA, D_MODEL)) * y_attn

    z = DN_ALPHA * x + _dot(acc.astype(BF16), wout_ref[...])
    x1 = _layer_norm(z, g1_ref[...], b1_ref[...])
    x1_ref[0] = x1
    ring_base = pl.multiple_of(ring_tile(tile_id) * ROW_TILES, ROW_TILES)
    for s in range(ROW_TILES):
        ring[pl.ds(ring_base + s, ts, stride=ROW_TILES), :] = x1[:, s * LANES:(s + 1) * LANES]

    @pl.when(tile_id == last_tile)
    def _():
        def move_tile(tile, slot):
            dest_copy(slot).wait()

            def issue(tb, carry_):
                for u in range(ISSUE_UNROLL):
                    for k in range(TOP_K):
                        row_copy(tile, slot, k, tb * ISSUE_UNROLL + u).start(priority=k % 2)
                return carry_

            lax.fori_loop(0, ts // ISSUE_UNROLL, issue, 0)

        wait_rows(cur)
        move_tile(tile_id - 1, oth)
        wait_rows(oth)
        route(load_ring(tile_id), tile_id, True, cur)
        dest_copy(cur).start()
        move_tile(tile_id, cur)
        wait_rows(cur)
        _zero_unassigned_rows(carry, cur_id, nfree, xs_hbm, trash_base, pad_v, pad_s, zbuf, psem, zsem)


def _trash_rows(trash_base, k, ts):
    return trash_base + k * ts + lax.broadcasted_iota(I32, (1, ts), 1)


def _route_tile(x1, tile, live, slot, trash_base, whi_ref, wlo_ref, br_ref, upper, lower, carry, cur_id,
                nfree, idx_ref, w_ref, dest_ref, cnt_ref, alloc_ref, dest_v):
    ts = x1.shape[0]
    shift = CHUNK.bit_length() - 1
    cols = pl.ds(pl.multiple_of(tile * ts, ts), ts)
    xh = x1.astype(BF16)
    xl = (x1 - xh.astype(F32)).astype(BF16)
    whi = whi_ref[...]
    logits = _dot_nt(whi, xh) + _dot_nt(whi, xl) + _dot_nt(wlo_ref[...], xh) + br_ref[...]

    eid = lax.broadcasted_iota(I32, (N_EXPERTS, ts), 0)
    rest = logits
    sels, vals = [], []
    for k in range(TOP_K):
        m = jnp.max(rest, axis=0, keepdims=True)
        idx = jnp.min(jnp.where(rest == m, eid, N_EXPERTS), axis=0, keepdims=True)
        sel = eid == idx
        rest = jnp.where(sel, -jnp.inf, rest)
        sels.append(sel)
        vals.append(m)
        idx_ref[k:k + 1, cols] = idx
    exps = [jnp.exp(v - vals[0]) for v in vals]
    den = exps[0] + exps[1] + exps[2] + exps[3]
    for k in range(TOP_K):
        w_ref[k:k + 1, cols] = exps[k] / den

    member = jnp.zeros((N_EXPERTS, ts), F32)
    for sel in sels:
        member = member + sel.astype(F32)
    before = carry[:, 0:1]
    rank = (_dot(member.astype(BF16), upper[...]) + before).astype(I32)
    tile_cnt = jnp.sum(member, axis=1, keepdims=True)

    before_i = before.astype(I32)
    after_i = before_i + tile_cnt.astype(I32)
    blk0 = before_i >> shift
    any_rows = after_i > before_i
    new0 = any_rows & ((before_i & (CHUNK - 1)) == 0)
    blk1 = (after_i - 1) >> shift
    new1 = any_rows & (blk1 > blk0)
    is_new = new0 | new1
    new_f = jnp.broadcast_to(is_new.astype(F32), (N_EXPERTS, LANES))
    new_id = (nfree[...] + _dot(lower[...], new_f.astype(BF16)))[:, 0:1].astype(I32)
    id0 = jnp.where(new0, new_id, cur_id[:, 0:1])
    row = (jnp.where((rank >> shift) > blk0, new_id, id0) * CHUNK + (rank & (CHUNK - 1))).astype(F32)
    for k in range(TOP_K):
        dk = jnp.sum(jnp.where(sels[k], row, 0.0), axis=0, keepdims=True).astype(I32)
        dk = jnp.where(live, dk, _trash_rows(trash_base, k, ts))
        dest_ref[k:k + 1, cols] = dk
        dest_v[slot, k:k + 1, :] = dk

    took = live & is_new
    lane = lax.broadcasted_iota(I32, alloc_ref.shape, 1)
    alloc_ref[...] = jnp.where(took & (lane == jnp.where(new0, blk0, blk1)), new_id, alloc_ref[...])
    cur_id[...] = jnp.where(took, new_id, cur_id[...])
    nfree[...] = nfree[...] + jnp.where(live, jnp.sum(is_new.astype(F32), axis=0, keepdims=True), 0.0)
    carry[...] = carry[...] + jnp.where(live, tile_cnt, 0.0)
    cnt_ref[...] = carry[...].astype(I32)


def _zero_unassigned_rows(carry, cur_id, nfree, xs_hbm, trash_base, pad_v, pad_s, zbuf, psem, zsem):
    shift = CHUNK.bit_length() - 1
    count = carry[...].astype(I32)
    fill = count - (((count - 1) >> shift) << shift)
    pad = jnp.where(count > 0, CHUNK - fill, 0)
    pos = cur_id[...] * CHUNK + fill
    lane = lax.broadcasted_iota(I32, pad_v.shape, 1)
    pad_v[...] = jnp.where(lane == 0, pos, jnp.where(lane == 1, pad, nfree[...].astype(I32)))
    info = pltpu.make_async_copy(pad_v, pad_s, psem)
    info.start()
    info.wait()
    pieces = [1 << b for b in reversed(range(shift))]
    half = pieces[0]
    for wait in (False, True):
        for e in range(N_EXPERTS):
            npad = pad_s[e, 1]
            for piece in pieces:
                cp = _row_copy(zbuf, 0, xs_hbm, pad_s[e, 0] + (npad & ~(2 * piece - 1)), piece, zsem)

                @pl.when((npad & piece) != 0)
                def _():
                    cp.wait() if wait else cp.start()

        def tail(hc, carry_):
            cp = _row_copy(zbuf, 0, xs_hbm, hc * half, half, zsem)
            cp.wait() if wait else cp.start()
            return carry_

        lax.fori_loop(2 * pad_s[0, 2], trash_base // half, tail, 0)


def _mix(x, w_in_p, rep, conv_w, wbrc, wbra, sinks, wout, g1, b1, whi, wlo, br, p_rows):
    bsz, seq, d = x.shape
    n = bsz * seq
    nj = seq // TS_MIX
    const = lambda shape: pl.BlockSpec(shape, lambda b, j, s: (0,) * len(shape),
                                       pipeline_mode=pl.Buffered(1))
    tok = pl.BlockSpec((TOP_K, n), lambda b, j, s: (0, 0))
    table = pl.BlockSpec((N_EXPERTS, LANES), lambda b, j, s: (0, 0))
    grid_spec = pltpu.PrefetchScalarGridSpec(
        num_scalar_prefetch=1,
        grid=(bsz, nj),
        in_specs=[
            pl.BlockSpec((1, TS_MIX, d), lambda b, j, s: (b, j, 0)),
            const((d, IN_TOTAL)),
            const((KV_DIM, d)),
            const((3, d)),
            const((d, d)),
            const((d, d)),
            const((d, d)),
            const((1, d)),
            const((1, d)),
            const((N_EXPERTS, d)),
            const((N_EXPERTS, d)),
            const((N_EXPERTS, 1)),
        ],
        out_specs=[
            pl.BlockSpec((1, TS_MIX, d), lambda b, j, s: (b, j, 0)),
            tok, tok, tok, table, table,
            pl.BlockSpec(memory_space=pl.ANY),
        ],
        scratch_shapes=[
            pltpu.VMEM((WINDOW + TS_MIX, d), BF16),
            pltpu.VMEM((WINDOW + TS_MIX, d), BF16),
            pltpu.VMEM((SUBLANES + TS_MIX, d), F32),
            pltpu.VMEM((TS_MIX, TS_MIX), BF16),
            pltpu.VMEM((N_EXPERTS, N_EXPERTS), BF16),
            pltpu.VMEM((N_EXPERTS, LANES), F32),
            pltpu.VMEM((N_EXPERTS, LANES), I32),
            pltpu.VMEM((N_EXPERTS, LANES), F32),
            pltpu.VMEM((3 * TS_MIX * ROW_TILES, LANES), F32),
            pltpu.VMEM((2, TOP_K, TS_MIX), I32),
            pltpu.SMEM((2, TOP_K, TS_MIX), I32),
            pltpu.VMEM((N_EXPERTS, LANES), I32),
            pltpu.SMEM((N_EXPERTS, LANES), I32),
            pltpu.VMEM((CHUNK // 2 * ROW_TILES, LANES), F32),
            pltpu.SemaphoreType.DMA((2,)),
            pltpu.SemaphoreType.DMA((2,)),
            pltpu.SemaphoreType.DMA,
            pltpu.SemaphoreType.DMA,
        ],
    )
    return pl.pallas_call(
        _mix_kernel,
        grid_spec=grid_spec,
        out_shape=[jax.ShapeDtypeStruct((bsz, seq, d), F32),
                   jax.ShapeDtypeStruct((TOP_K, n), I32),
                   jax.ShapeDtypeStruct((TOP_K, n), F32),
                   jax.ShapeDtypeStruct((TOP_K, n), I32),
                   jax.ShapeDtypeStruct((N_EXPERTS, LANES), I32),
                   jax.ShapeDtypeStruct((N_EXPERTS, LANES), I32),
                   jax.ShapeDtypeStruct(((p_rows + TRASH_ROWS) * ROW_TILES, LANES), F32)],
        compiler_params=pltpu.CompilerParams(
            dimension_semantics=("arbitrary", "arbitrary"),
            vmem_limit_bytes=VMEM_LIMIT_BYTES,
            has_side_effects=True),
        name="mix",
    )(sinks, x, w_in_p, rep, conv_w, wbrc, wbra, wout, g1, b1, whi, wlo, br)


def _tiled_index(dest, tile_tokens):
    k, n = dest.shape
    return dest.reshape(k, n // tile_tokens, tile_tokens).transpose(1, 0, 2).reshape(k * n)


def _moe_kernel(ce_ref, nu_ref, slot_ref, nxt_ref, fill_ref, blk_ref, xs_ref, wgu_hbm, bgu_ref, wd_hbm,
                bd_ref, ys_ref, wgu_f32, wd_f32, wgu_bf, wd_bf, wsems):
    c = pl.program_id(0)
    used = c < nu_ref[0]

    def weight_copies(e, slot):
        return (pltpu.make_async_copy(wgu_hbm.at[e], wgu_f32.at[slot], wsems.at[0, slot]),
                pltpu.make_async_copy(wd_hbm.at[e], wd_f32.at[slot], wsems.at[1, slot]))

    @pl.when(used & ((c == 0) | (ce_ref[c] != ce_ref[jnp.maximum(c - 1, 0)])))
    def _():
        slot = slot_ref[c]

        @pl.when(c == 0)
        def _():
            for cp in weight_copies(ce_ref[0], slot):
                cp.start()

        for cp in weight_copies(ce_ref[c], slot):
            cp.wait()

        @pl.when(nxt_ref[c] >= 0)
        def _():
            for cp in weight_copies(nxt_ref[c], 1 - slot):
                cp.start()

        wgu_bf[...] = wgu_f32[slot].astype(BF16)
        wd_bf[...] = wd_f32[slot].astype(BF16)

    def expert(rows):
        x = jnp.concatenate(
            [xs_ref[pl.ds(s, rows, stride=ROW_TILES), :] for s in range(ROW_TILES)], axis=-1)
        gu = _dot(x.astype(BF16), wgu_bf[...]) + bgu_ref[0]
        gate = jnp.minimum(gu[:, :D_EXPERT], SWIGLU_LIMIT)
        up = jnp.clip(gu[:, D_EXPERT:], -SWIGLU_LIMIT, SWIGLU_LIMIT)
        h = (up + 1.0) * gate * jax.nn.sigmoid(SWIGLU_ALPHA * gate)
        y = _dot(h.astype(BF16), wd_bf[...]) + bd_ref[0]
        for s in range(ROW_TILES):
            ys_ref[pl.ds(s, rows, stride=ROW_TILES), :] = y[:, s * LANES:(s + 1) * LANES]
        if rows < CHUNK:
            ys_ref[rows * ROW_TILES:, :] = jnp.zeros(((CHUNK - rows) * ROW_TILES, LANES), F32)

    filled = jnp.where(used, fill_ref[jnp.minimum(c, fill_ref.shape[0] - 1)], 0)
    lower = 0
    for rows in MOE_ROW_VARIANTS:
        @pl.when((filled > lower) & (filled <= rows))
        def _():
            expert(rows)

        lower = rows

    @pl.when(filled == 0)
    def _():
        ys_ref[...] = jnp.zeros_like(ys_ref)


def _moe(chunk_e, n_used, run_slot, next_e, chunk_fill, chunk_blk, xs, wgu, bgu, wd, bd, p_rows):
    n_chunks = p_rows // CHUNK
    d = D_MODEL

    def cc(c, ce, nu, *_):
        return jnp.minimum(c, nu[0] - 1)

    rows_in = pl.BlockSpec((CHUNK * ROW_TILES, LANES), lambda c, *s: (s[5][cc(c, *s)], 0))
    rows_out = pl.BlockSpec((CHUNK * ROW_TILES, LANES), lambda c, *s: (s[5][c], 0))
    bspec = lambda shape: pl.BlockSpec(shape, lambda c, *s: (s[0][cc(c, *s)], 0, 0))
    anyspec = pl.BlockSpec(memory_space=pl.ANY)
    return pl.pallas_call(
        _moe_kernel,
        grid_spec=pltpu.PrefetchScalarGridSpec(
            num_scalar_prefetch=6, grid=(n_chunks,),
            in_specs=[rows_in, anyspec, bspec((1, 1, 2 * D_EXPERT)), anyspec, bspec((1, 1, d))],
            out_specs=rows_out,
            scratch_shapes=[pltpu.VMEM((2, d, 2 * D_EXPERT), F32), pltpu.VMEM((2, D_EXPERT, d), F32),
                            pltpu.VMEM((d, 2 * D_EXPERT), BF16), pltpu.VMEM((D_EXPERT, d), BF16),
                            pltpu.SemaphoreType.DMA((2, 2))]),
        out_shape=jax.ShapeDtypeStruct((p_rows * ROW_TILES, LANES), F32),
        compiler_params=pltpu.CompilerParams(dimension_semantics=("arbitrary",),
                                             vmem_limit_bytes=VMEM_LIMIT_BYTES),
        name="moe",
    )(chunk_e, n_used, run_slot, next_e, chunk_fill, chunk_blk, xs, wgu, bgu, wd, bd)


def _fin_kernel(didx_ref, ys_ref, x1_ref, w_ref, p_ref, wpg_ref, wpp_ref, g2_ref, b2_ref,
                out_ref, rows_a, rows_b, sems):
    ts = TS_FIN
    i = pl.program_id(0)
    last = pl.num_programs(0) - 1
    bufs = (rows_a, rows_b)

    def gather_row(par, tile_base, k, t):
        _row_copy(ys_ref, didx_ref[tile_base + k * ts + t], bufs[par].at[k], t, 1,
                  sems.at[par]).start(priority=k % 2)

    def wait_rows(par):
        for k in range(TOP_K):
            _row_copy(ys_ref, 0, bufs[par].at[k], 0, ts, sems.at[par]).wait()

    @pl.when(i == 0)
    def _():
        def issue(tb, carry):
            for u in range(ISSUE_UNROLL):
                for k in range(TOP_K):
                    gather_row(0, 0, k, tb * ISSUE_UNROLL + u)
            return carry

        lax.fori_loop(0, ts // ISSUE_UNROLL, issue, 0)

    next_base = jnp.minimum(i + 1, last) * (TOP_K * ts)

    for par in range(2):
        @pl.when(i % 2 == par)
        def _():
            wait_rows(par)
            for t in range(ts):
                for k in range(TOP_K):
                    gather_row(1 - par, next_base, k, t)

            x1 = x1_ref[...]
            ple = (jax.nn.sigmoid(_dot(x1.astype(BF16), wpg_ref[...]))
                   * _dot(p_ref[...].astype(BF16), wpp_ref[...]))
            z = DN_ALPHA * x1 + ple
            w = w_ref[...]
            for k in range(TOP_K):
                yk = jnp.concatenate(
                    [bufs[par][k, pl.ds(s, ts, stride=ROW_TILES), :] for s in range(ROW_TILES)],
                    axis=-1)
                z = z + w[:, k:k + 1] * yk
            out_ref[...] = _layer_norm(z, g2_ref[...], b2_ref[...])

            @pl.when(i == last)
            def _():
                wait_rows(1 - par)


def _fin(dest, ys, x1, w_rows, p2, wpg, wpp, g2, b2):
    n, d = x1.shape
    nt = n // TS_FIN
    didx = _tiled_index(dest, TS_FIN)
    const = lambda shape: pl.BlockSpec(shape, lambda i, s: (0,) * len(shape))
    tile = lambda width: pl.BlockSpec((TS_FIN, width), lambda i, s: (i, 0))
    return pl.pallas_call(
        _fin_kernel,
        grid_spec=pltpu.PrefetchScalarGridSpec(
            num_scalar_prefetch=1, grid=(nt,),
            in_specs=[pl.BlockSpec(memory_space=pl.ANY), tile(d), tile(TOP_K), tile(D_PLE),
                      const((d, d)), const((D_PLE, d)), const((1, d)), const((1, d))],
            out_specs=tile(d),
            scratch_shapes=[pltpu.VMEM((TOP_K, TS_FIN * ROW_TILES, LANES), F32),
                            pltpu.VMEM((TOP_K, TS_FIN * ROW_TILES, LANES), F32),
                            pltpu.SemaphoreType.DMA((2,))]),
        out_shape=jax.ShapeDtypeStruct((n, d), F32),
        compiler_params=pltpu.CompilerParams(dimension_semantics=("arbitrary",),
                                             vmem_limit_bytes=VMEM_LIMIT_BYTES),
        name="fin",
    )(didx, ys, x1, w_rows, p2, wpg, wpp, g2, b2)


def _permute_w_in(w_in):
    d = D_MODEL
    b_, c_, h_, q_ = (w_in[:, i * d:(i + 1) * d] for i in range(4))
    k_ = w_in[:, 4 * d:4 * d + KV_DIM]
    v_ = w_in[:, 4 * d + KV_DIM:4 * d + 2 * KV_DIM]
    gc_ = w_in[:, 4 * d + 2 * KV_DIM:5 * d + 2 * KV_DIM]
    ga_ = w_in[:, 5 * d + 2 * KV_DIM:]
    return jnp.concatenate([b_, c_, h_, q_, gc_, ga_, k_, v_], axis=1).astype(BF16)


def _replication_matrix():
    src = jnp.arange(KV_DIM)[:, None]
    dst = jnp.arange(D_MODEL)[None, :]
    same_head = (dst // (GQA_GROUP * HEAD_DIM)) == (src // HEAD_DIM)
    same_dim = (dst % HEAD_DIM) == (src % HEAD_DIM)
    return (same_head & same_dim).astype(BF16)


def kernel(x, p, w_in, conv_w, w_br_conv, w_br_attn, attn_sinks, w_out, ln1_g, ln1_b, w_router,
           b_router, w_gu, b_gu, w_down, b_down, w_ple_proj, w_ple_gate, ln2_g, ln2_b):
    bsz, seq, d = x.shape
    n = bsz * seq
    for i in range(DEPTH):
        wr_t = w_router[i].T
        wr_hi = wr_t.astype(BF16)
        wr_lo = (wr_t - wr_hi.astype(F32)).astype(BF16)
        p_rows = n * TOP_K + N_EXPERTS * CHUNK
        x1, idx, w_top, dest, cnt, alloc, xs = _mix(
            x, _permute_w_in(w_in[i]), _replication_matrix(), conv_w[i],
            w_br_conv[i].astype(BF16), w_br_attn[i].astype(BF16), attn_sinks[i],
            w_out[i].astype(BF16), ln1_g[i][None, :], ln1_b[i][None, :],
            wr_hi, wr_lo, b_router[i][:, None], p_rows)
        x1 = x1.reshape(n, d)

        counts = cnt[:, 0]
        padded = ((counts + CHUNK - 1) // CHUNK) * CHUNK
        end_padded = jnp.cumsum(padded)
        start_padded = end_padded - padded
        n_chunks = p_rows // CHUNK
        chunk_pos = jnp.arange(n_chunks, dtype=I32)
        chunk_start = chunk_pos * CHUNK
        chunk_e = jnp.minimum(
            jnp.sum((end_padded[None, :] <= chunk_start[:, None]).astype(I32), axis=1),
            N_EXPERTS - 1)
        n_used = (end_padded[-1:] // CHUNK).astype(I32)
        eid = jnp.arange(N_EXPERTS, dtype=I32)
        onehot = (chunk_e[:, None] == eid[None, :]).astype(I32)
        chunk_no = chunk_pos - jnp.sum(onehot * (start_padded // CHUNK)[None, :], axis=1)
        in_table = onehot[:, :, None] * (chunk_no[:, None, None] == jnp.arange(LANES, dtype=I32)[None, None, :])
        chunk_blk = jnp.where(chunk_pos < n_used[0], jnp.sum(in_table * alloc[None, :, :], axis=(1, 2)),
                              chunk_pos).astype(I32)
        active = padded > 0
        run_idx = jnp.cumsum(active.astype(I32)) - 1
        later = active[None, :] & (eid[None, :] > eid[:, None])
        nxt = jnp.min(jnp.where(later, eid[None, :], N_EXPERTS), axis=1)
        nxt = jnp.where(nxt == N_EXPERTS, -1, nxt)
        run_slot = jnp.sum(onehot * (run_idx & 1)[None, :], axis=1).astype(I32)
        next_e = jnp.sum(onehot * nxt[None, :], axis=1).astype(I32)
        chunk_fill = jnp.clip(
            jnp.sum(onehot * (counts + start_padded)[None, :], axis=1) - chunk_start, 0, CHUNK).astype(I32)
        ys = _moe(chunk_e, n_used, run_slot, next_e, chunk_fill, chunk_blk, xs, w_gu[i],
                  b_gu[i][:, None, :], w_down[i], b_down[i][:, None, :], p_rows)
        out = _fin(dest, ys, x1, w_top.T, p[i].reshape(n, D_PLE),
                   w_ple_gate[i].astype(BF16), w_ple_proj[i].astype(BF16),
                   ln2_g[i][None, :], ln2_b[i][None, :])
        x = out.reshape(bsz, seq, d)
    return x
```

```python
import math

import jax
import jax.numpy as jnp
from jax import lax
from jax.experimental import pallas as pl
from jax.experimental.pallas import tpu as pltpu

F32 = jnp.float32
BF16 = jnp.bfloat16
I32 = jnp.int32

SUBLANES = 8
LANES = 128
VMEM_LIMIT_BYTES = 56 * 1024 * 1024

D_MODEL = 1024
HEAD_DIM = 64
N_Q_HEADS = 16
N_KV_HEADS = 4
GQA_GROUP = N_Q_HEADS // N_KV_HEADS
KV_DIM = N_KV_HEADS * HEAD_DIM
WINDOW = 128
N_EXPERTS = 32
TOP_K = 4
D_EXPERT = D_MODEL
SWIGLU_LIMIT = 7.0
SWIGLU_ALPHA = 1.702
D_PLE = 256
DEPTH = 1
DN_ALPHA = (2.0 * DEPTH) ** 0.25
LN_EPS = 1e-5
ROW_TILES = D_MODEL // LANES

OFF_B, OFF_C, OFF_H, OFF_Q, OFF_GC, OFF_GA = (i * D_MODEL for i in range(6))
OFF_K = 6 * D_MODEL
OFF_V = OFF_K + KV_DIM
IN_TOTAL = OFF_V + KV_DIM

TS_MIX = 512
TRASH_ROWS = TOP_K * TS_MIX
CHUNK = 512
MOE_ROW_VARIANTS = (128, 256, CHUNK)
TS_FIN = 256
ISSUE_UNROLL = 4


def _layer_norm(z, g, b):
    mu = jnp.mean(z, axis=-1, keepdims=True)
    zc = z - mu
    var = jnp.mean(zc * zc, axis=-1, keepdims=True)
    return zc * lax.rsqrt(var + LN_EPS) * g + b


def _dot(a, b):
    return jnp.dot(a, b, preferred_element_type=F32)


def _dot_nt(a, b):
    return lax.dot_general(a, b, (((1,), (1,)), ((), ())), preferred_element_type=F32)


def _row_copy(src, src_row, dst, dst_row, rows, sem):
    s0 = pl.multiple_of(src_row * ROW_TILES, ROW_TILES)
    d0 = pl.multiple_of(dst_row * ROW_TILES, ROW_TILES)
    return pltpu.make_async_copy(src.at[pl.ds(s0, rows * ROW_TILES)],
                                 dst.at[pl.ds(d0, rows * ROW_TILES)], sem)


def _mix_kernel(sinks_ref, x_ref, win_ref, rep_ref, convw_ref, wbrc_ref, wbra_ref, wout_ref,
                g1_ref, b1_ref, whi_ref, wlo_ref, br_ref,
                x1_ref, idx_ref, w_ref, dest_ref, cnt_ref, alloc_ref, xs_hbm,
                kext, vext, uext, upper, lower, carry, cur_id, nfree, ring, dest_v, dest_s,
                pad_v, pad_s, zbuf, dsems, rsems, psem, zsem):
    ts = TS_MIX
    j = pl.program_id(1)
    tile_id = pl.program_id(0) * pl.num_programs(1) + j
    last_tile = idx_ref.shape[1] // ts - 1
    trash_base = xs_hbm.shape[0] // ROW_TILES - TRASH_ROWS
    cur = tile_id % 2
    oth = 1 - cur

    def ring_tile(tile):
        return ((tile + 3) % 3) * ts

    def load_ring(tile):
        base = pl.multiple_of(ring_tile(tile) * ROW_TILES, ROW_TILES)
        return jnp.concatenate(
            [ring[pl.ds(base + s, ts, stride=ROW_TILES), :] for s in range(ROW_TILES)], axis=-1)

    def dest_copy(slot):
        return pltpu.make_async_copy(dest_v.at[slot], dest_s.at[slot], dsems.at[slot])

    def row_copy(tile, slot, k, t):
        return _row_copy(ring, ring_tile(tile) + t, xs_hbm, dest_s[slot, k, t], 1, rsems.at[slot])

    def wait_rows(slot):
        for k in range(TOP_K):
            _row_copy(ring, 0, xs_hbm, 0, ts, rsems.at[slot]).wait()

    route = lambda x1_tile, tile, live, slot: _route_tile(
        x1_tile, tile, live, slot, trash_base, whi_ref, wlo_ref, br_ref, upper, lower, carry, cur_id,
        nfree, idx_ref, w_ref, dest_ref, cnt_ref, alloc_ref, dest_v)

    @pl.when(j == 0)
    def _():
        kext[0:WINDOW, :] = jnp.zeros((WINDOW, D_MODEL), BF16)
        vext[0:WINDOW, :] = jnp.zeros((WINDOW, D_MODEL), BF16)
        uext[0:SUBLANES, :] = jnp.zeros((SUBLANES, D_MODEL), F32)

    @pl.when(tile_id == 0)
    def _():
        carry[...] = jnp.zeros_like(carry)
        cur_id[...] = jnp.zeros_like(cur_id)
        nfree[...] = jnp.zeros_like(nfree)
        alloc_ref[...] = jnp.full(alloc_ref.shape, -1, I32)
        ring[...] = jnp.zeros_like(ring)
        zbuf[...] = jnp.zeros_like(zbuf)
        r = lax.broadcasted_iota(I32, (ts, ts), 0)
        c = lax.broadcasted_iota(I32, (ts, ts), 1)
        upper[...] = (r < c).astype(BF16)
        r = lax.broadcasted_iota(I32, (N_EXPERTS, N_EXPERTS), 0)
        c = lax.broadcasted_iota(I32, (N_EXPERTS, N_EXPERTS), 1)
        lower[...] = (c < r).astype(BF16)
        for k in range(TOP_K):
            dest_v[0, k:k + 1, :] = _trash_rows(trash_base, k, ts)
        dest_copy(0).start()

    @pl.when(tile_id >= 1)
    def _():
        wait_rows(oth)

    route(load_ring(tile_id - 1), jnp.maximum(tile_id - 1, 0), tile_id >= 1, oth)
    dest_copy(oth).start()
    dest_copy(cur).wait()
    for t in range(ts):
        for k in range(TOP_K):
            row_copy(tile_id - 2, cur, k, t).start(priority=k % 2)

    x = x_ref[0]
    xb = x.astype(BF16)

    def proj(off, width):
        return _dot(xb, win_ref[:, off:off + width])

    u = proj(OFF_C, D_MODEL) * proj(OFF_H, D_MODEL)
    uext[SUBLANES:SUBLANES + ts, :] = u
    y = (convw_ref[2:3, :] * u
         + convw_ref[1:2, :] * uext[SUBLANES - 1:SUBLANES - 1 + ts, :]
         + convw_ref[0:1, :] * uext[SUBLANES - 2:SUBLANES - 2 + ts, :])
    uext[0:SUBLANES, :] = u[ts - SUBLANES:ts, :]
    yc_in = proj(OFF_B, D_MODEL) * y
    y_conv = _dot(yc_in.astype(BF16), wbrc_ref[...])
    acc = jax.nn.sigmoid(proj(OFF_GC, D_MODEL)) * y_conv

    q = (proj(OFF_Q, D_MODEL) * (1.0 / math.sqrt(HEAD_DIM))).astype(BF16)
    kb = proj(OFF_K, KV_DIM).astype(BF16)
    vb = proj(OFF_V, KV_DIM).astype(BF16)
    kext[WINDOW:WINDOW + ts, :] = _dot(kb, rep_ref[...]).astype(BF16)
    vext[WINDOW:WINDOW + ts, :] = _dot(vb, rep_ref[...]).astype(BF16)

    grp = GQA_GROUP * HEAD_DIM
    row = lax.broadcasted_iota(I32, (WINDOW, 2 * WINDOW), 0)
    col = lax.broadcasted_iota(I32, (WINDOW, 2 * WINDOW), 1)
    band = (col > row) & (col <= row + WINDOW)
    q_lane_grp = lax.broadcasted_iota(I32, (WINDOW, grp), 1) // HEAD_DIM
    o_blocks = []
    for i in range(ts // WINDOW):
        if i == 0:
            mask = band & ((col >= WINDOW) | (j > 0))
        else:
            mask = band
        o_heads = []
        for h in range(N_KV_HEADS):
            qh = q[i * WINDOW:(i + 1) * WINDOW, h * grp:(h + 1) * grp]
            kh = kext[i * WINDOW:i * WINDOW + 2 * WINDOW, h * grp:(h + 1) * grp]
            vh = vext[i * WINDOW:i * WINDOW + 2 * WINDOW, h * grp:(h + 1) * grp]
            qs = jnp.concatenate(
                [jnp.where(q_lane_grp == g, qh, jnp.zeros_like(qh)) for g in range(GQA_GROUP)], axis=0)
            s_all = _dot_nt(qs, kh)
            ps = []
            for g in range(GQA_GROUP):
                s = jnp.where(mask, s_all[g * WINDOW:(g + 1) * WINDOW], -jnp.inf)
                sink = sinks_ref[h * GQA_GROUP + g]
                m = jnp.maximum(jnp.max(s, axis=-1, keepdims=True), sink)
                e = jnp.exp(s - m)
                den = jnp.sum(e, axis=-1, keepdims=True) + jnp.exp(sink - m)
                ps.append((e / den).astype(BF16))
            pv = _dot(jnp.concatenate(ps, axis=0), vh)
            oh = jnp.zeros((WINDOW, grp), F32)
            for g in range(GQA_GROUP):
                oh = jnp.where(q_lane_grp == g, pv[g * WINDOW:(g + 1) * WINDOW], oh)
            o_heads.append(oh)
        o_blocks.append(jnp.concatenate(o_heads, axis=-1))
    o = jnp.concatenate(o_blocks, axis=0)
    kext[0:WINDOW, :] = kext[ts:ts + WINDOW, :]
    vext[0:WINDOW, :] = vext[ts:ts + WINDOW, :]

    y_attn = _dot(o.astype(BF16), wbra_ref[...])
    acc = acc + jax.nn.sigmoid(proj(OFF_GA, D_MODEL)) * y_attn

    z = DN_ALPHA * x + _dot(acc.astype(BF16), wout_ref[...])
    x1 = _layer_norm(z, g1_ref[...], b1_ref[...])
    x1_ref[0] = x1
    ring_base = pl.multiple_of(ring_tile(tile_id) * ROW_TILES, ROW_TILES)
    for s in range(ROW_TILES):
        ring[pl.ds(ring_base + s, ts, stride=ROW_TILES), :] = x1[:, s * LANES:(s + 1) * LANES]

    @pl.when(tile_id == last_tile)
    def _():
        def move_tile(tile, slot):
            dest_copy(slot).wait()

            def issue(tb, carry_):
                for u in range(ISSUE_UNROLL):
                    for k in range(TOP_K):
                        row_copy(tile, slot, k, tb * ISSUE_UNROLL + u).start(priority=k % 2)
                return carry_

            lax.fori_loop(0, ts // ISSUE_UNROLL, issue, 0)

        wait_rows(cur)
        move_tile(tile_id - 1, oth)
        wait_rows(oth)
        route(load_ring(tile_id), tile_id, True, cur)
        dest_copy(cur).start()
        move_tile(tile_id, cur)
        wait_rows(cur)
        _zero_unassigned_rows(carry, cur_id, nfree, xs_hbm, trash_base, pad_v, pad_s, zbuf, psem, zsem)


def _trash_rows(trash_base, k, ts):
    return trash_base + k * ts + lax.broadcasted_iota(I32, (1, ts), 1)


def _route_tile(x1, tile, live, slot, trash_base, whi_ref, wlo_ref, br_ref, upper, lower, carry, cur_id,
                nfree, idx_ref, w_ref, dest_ref, cnt_ref, alloc_ref, dest_v):
    ts = x1.shape[0]
    shift = CHUNK.bit_length() - 1
    cols = pl.ds(pl.multiple_of(tile * ts, ts), ts)
    xh = x1.astype(BF16)
    xl = (x1 - xh.astype(F32)).astype(BF16)
    whi = whi_ref[...]
    logits = _dot_nt(whi, xh) + _dot_nt(whi, xl) + _dot_nt(wlo_ref[...], xh) + br_ref[...]

    eid = lax.broadcasted_iota(I32, (N_EXPERTS, ts), 0)
    rest = logits
    sels, vals = [], []
    for k in range(TOP_K):
        m = jnp.max(rest, axis=0, keepdims=True)
        idx = jnp.min(jnp.where(rest == m, eid, N_EXPERTS), axis=0, keepdims=True)
        sel = eid == idx
        rest = jnp.where(sel, -jnp.inf, rest)
        sels.append(sel)
        vals.append(m)
        idx_ref[k:k + 1, cols] = idx
    exps = [jnp.exp(v - vals[0]) for v in vals]
    den = exps[0] + exps[1] + exps[2] + exps[3]
    for k in range(TOP_K):
        w_ref[k:k + 1, cols] = exps[k] / den

    member = jnp.zeros((N_EXPERTS, ts), F32)
    for sel in sels:
        member = member + sel.astype(F32)
    before = carry[:, 0:1]
    rank = (_dot(member.astype(BF16), upper[...]) + before).astype(I32)
    tile_cnt = jnp.sum(member, axis=1, keepdims=True)

    before_i = before.astype(I32)
    after_i = before_i + tile_cnt.astype(I32)
    blk0 = before_i >> shift
    any_rows = after_i > before_i
    new0 = any_rows & ((before_i & (CHUNK - 1)) == 0)
    blk1 = (after_i - 1) >> shift
    new1 = any_rows & (blk1 > blk0)
    is_new = new0 | new1
    new_f = jnp.broadcast_to(is_new.astype(F32), (N_EXPERTS, LANES))
    new_id = (nfree[...] + _dot(lower[...], new_f.astype(BF16)))[:, 0:1].astype(I32)
    id0 = jnp.where(new0, new_id, cur_id[:, 0:1])
    row = (jnp.where((rank >> shift) > blk0, new_id, id0) * CHUNK + (rank & (CHUNK - 1))).astype(F32)
    for k in range(TOP_K):
        dk = jnp.sum(jnp.where(sels[k], row, 0.0), axis=0, keepdims=True).astype(I32)
        dk = jnp.where(live, dk, _trash_rows(trash_base, k, ts))
        dest_ref[k:k + 1, cols] = dk
        dest_v[slot, k:k + 1, :] = dk

    took = live & is_new
    lane = lax.broadcasted_iota(I32, alloc_ref.shape, 1)
    alloc_ref[...] = jnp.where(took & (lane == jnp.where(new0, blk0, blk1)), new_id, alloc_ref[...])
    cur_id[...] = jnp.where(took, new_id, cur_id[...])
    nfree[...] = nfree[...] + jnp.where(live, jnp.sum(is_new.astype(F32), axis=0, keepdims=True), 0.0)
    carry[...] = carry[...] + jnp.where(live, tile_cnt, 0.0)
    cnt_ref[...] = carry[...].astype(I32)


def _zero_unassigned_rows(carry, cur_id, nfree, xs_hbm, trash_base, pad_v, pad_s, zbuf, psem, zsem):
    shift = CHUNK.bit_length() - 1
    count = carry[...].astype(I32)
    fill = count - (((count - 1) >> shift) << shift)
    pad = jnp.where(count > 0, CHUNK - fill, 0)
    pos = cur_id[...] * CHUNK + fill
    lane = lax.broadcasted_iota(I32, pad_v.shape, 1)
    pad_v[...] = jnp.where(lane == 0, pos, jnp.where(lane == 1, pad, nfree[...].astype(I32)))
    info = pltpu.make_async_copy(pad_v, pad_s, psem)
    info.start()
    info.wait()
    pieces = [1 << b for b in reversed(range(shift))]
    half = pieces[0]
    for wait in (False, True):
        for e in range(N_EXPERTS):
            npad = pad_s[e, 1]
            for piece in pieces:
                cp = _row_copy(zbuf, 0, xs_hbm, pad_s[e, 0] + (npad & ~(2 * piece - 1)), piece, zsem)

                @pl.when((npad & piece) != 0)
                def _():
                    cp.wait() if wait else cp.start()

        def tail(hc, carry_):
            cp = _row_copy(zbuf, 0, xs_hbm, hc * half, half, zsem)
            cp.wait() if wait else cp.start()
            return carry_

        lax.fori_loop(2 * pad_s[0, 2], trash_base // half, tail, 0)


def _mix(x, w_in_p, rep, conv_w, wbrc, wbra, sinks, wout, g1, b1, whi, wlo, br, p_rows):
    bsz, seq, d = x.shape
    n = bsz * seq
    nj = seq // TS_MIX
    const = lambda shape: pl.BlockSpec(shape, lambda b, j, s: (0,) * len(shape),
                                       pipeline_mode=pl.Buffered(1))
    tok = pl.BlockSpec((TOP_K, n), lambda b, j, s: (0, 0))
    table = pl.BlockSpec((N_EXPERTS, LANES), lambda b, j, s: (0, 0))
    grid_spec = pltpu.PrefetchScalarGridSpec(
        num_scalar_prefetch=1,
        grid=(bsz, nj),
        in_specs=[
            pl.BlockSpec((1, TS_MIX, d), lambda b, j, s: (b, j, 0)),
            const((d, IN_TOTAL)),
            const((KV_DIM, d)),
            const((3, d)),
            const((d, d)),
            const((d, d)),
            const((d, d)),
            const((1, d)),
            const((1, d)),
            const((N_EXPERTS, d)),
            const((N_EXPERTS, d)),
            const((N_EXPERTS, 1)),
        ],
        out_specs=[
            pl.BlockSpec((1, TS_MIX, d), lambda b, j, s: (b, j, 0)),
            tok, tok, tok, table, table,
            pl.BlockSpec(memory_space=pl.ANY),
        ],
        scratch_shapes=[
            pltpu.VMEM((WINDOW + TS_MIX, d), BF16),
            pltpu.VMEM((WINDOW + TS_MIX, d), BF16),
            pltpu.VMEM((SUBLANES + TS_MIX, d), F32),
            pltpu.VMEM((TS_MIX, TS_MIX), BF16),
            pltpu.VMEM((N_EXPERTS, N_EXPERTS), BF16),
            pltpu.VMEM((N_EXPERTS, LANES), F32),
            pltpu.VMEM((N_EXPERTS, LANES), I32),
            pltpu.VMEM((N_EXPERTS, LANES), F32),
            pltpu.VMEM((3 * TS_MIX * ROW_TILES, LANES), F32),
            pltpu.VMEM((2, TOP_K, TS_MIX), I32),
            pltpu.SMEM((2, TOP_K, TS_MIX), I32),
            pltpu.VMEM((N_EXPERTS, LANES), I32),
            pltpu.SMEM((N_EXPERTS, LANES), I32),
            pltpu.VMEM((CHUNK // 2 * ROW_TILES, LANES), F32),
            pltpu.SemaphoreType.DMA((2,)),
            pltpu.SemaphoreType.DMA((2,)),
            pltpu.SemaphoreType.DMA,
            pltpu.SemaphoreType.DMA,
        ],
    )
    return pl.pallas_call(
        _mix_kernel,
        grid_spec=grid_spec,
        out_shape=[jax.ShapeDtypeStruct((bsz, seq, d), F32),
                   jax.ShapeDtypeStruct((TOP_K, n), I32),
                   jax.ShapeDtypeStruct((TOP_K, n), F32),
                   jax.ShapeDtypeStruct((TOP_K, n), I32),
                   jax.ShapeDtypeStruct((N_EXPERTS, LANES), I32),
                   jax.ShapeDtypeStruct((N_EXPERTS, LANES), I32),
                   jax.ShapeDtypeStruct(((p_rows + TRASH_ROWS) * ROW_TILES, LANES), F32)],
        compiler_params=pltpu.CompilerParams(
            dimension_semantics=("arbitrary", "arbitrary"),
            vmem_limit_bytes=VMEM_LIMIT_BYTES,
            has_side_effects=True),
        name="mix",
    )(sinks, x, w_in_p, rep, conv_w, wbrc, wbra, wout, g1, b1, whi, wlo, br)


def _moe_kernel(ce_ref, nu_ref, slot_ref, nxt_ref, fill_ref, blk_ref, xs_ref, wgu_hbm, bgu_ref, wd_hbm,
                bd_ref, ys_ref, wgu_f32, wd_f32, wgu_bf, wd_bf, wsems):
    c = pl.program_id(0)
    used = c < nu_ref[0]

    def weight_copies(e, slot):
        return (pltpu.make_async_copy(wgu_hbm.at[e], wgu_f32.at[slot], wsems.at[0, slot]),
                pltpu.make_async_copy(wd_hbm.at[e], wd_f32.at[slot], wsems.at[1, slot]))

    @pl.when(used & ((c == 0) | (ce_ref[c] != ce_ref[jnp.maximum(c - 1, 0)])))
    def _():
        slot = slot_ref[c]

        @pl.when(c == 0)
        def _():
            for cp in weight_copies(ce_ref[0], slot):
                cp.start()

        for cp in weight_copies(ce_ref[c], slot):
            cp.wait()

        @pl.when(nxt_ref[c] >= 0)
        def _():
            for cp in weight_copies(nxt_ref[c], 1 - slot):
                cp.start()

        wgu_bf[...] = wgu_f32[slot].astype(BF16)
        wd_bf[...] = wd_f32[slot].astype(BF16)

    def expert(rows):
        x = jnp.concatenate(
            [xs_ref[pl.ds(s, rows, stride=ROW_TILES), :] for s in range(ROW_TILES)], axis=-1)
        gu = _dot(x.astype(BF16), wgu_bf[...]) + bgu_ref[0]
        gate = jnp.minimum(gu[:, :D_EXPERT], SWIGLU_LIMIT)
        up = jnp.clip(gu[:, D_EXPERT:], -SWIGLU_LIMIT, SWIGLU_LIMIT)
        h = (up + 1.0) * gate * jax.nn.sigmoid(SWIGLU_ALPHA * gate)
        y = _dot(h.astype(BF16), wd_bf[...]) + bd_ref[0]
        for s in range(ROW_TILES):
            ys_ref[pl.ds(s, rows, stride=ROW_TILES), :] = y[:, s * LANES:(s + 1) * LANES]
        if rows < CHUNK:
            ys_ref[rows * ROW_TILES:, :] = jnp.zeros(((CHUNK - rows) * ROW_TILES, LANES), F32)

    filled = jnp.where(used, fill_ref[jnp.minimum(c, fill_ref.shape[0] - 1)], 0)
    lower = 0
    for rows in MOE_ROW_VARIANTS:
        @pl.when((filled > lower) & (filled <= rows))
        def _():
            expert(rows)

        lower = rows

    @pl.when(filled == 0)
    def _():
        ys_ref[...] = jnp.zeros_like(ys_ref)


def _moe(chunk_e, n_used, run_slot, next_e, chunk_fill, chunk_blk, xs, wgu, bgu, wd, bd, p_rows):
    n_chunks = p_rows // CHUNK
    d = D_MODEL

    def cc(c, ce, nu, *_):
        return jnp.minimum(c, nu[0] - 1)

    rows_in = pl.BlockSpec((CHUNK * ROW_TILES, LANES), lambda c, *s: (s[5][cc(c, *s)], 0))
    rows_out = pl.BlockSpec((CHUNK * ROW_TILES, LANES), lambda c, *s: (s[5][c], 0))
    bspec = lambda shape: pl.BlockSpec(shape, lambda c, *s: (s[0][cc(c, *s)], 0, 0))
    anyspec = pl.BlockSpec(memory_space=pl.ANY)
    return pl.pallas_call(
        _moe_kernel,
        grid_spec=pltpu.PrefetchScalarGridSpec(
            num_scalar_prefetch=6, grid=(n_chunks,),
            in_specs=[rows_in, anyspec, bspec((1, 1, 2 * D_EXPERT)), anyspec, bspec((1, 1, d))],
            out_specs=rows_out,
            scratch_shapes=[pltpu.VMEM((2, d, 2 * D_EXPERT), F32), pltpu.VMEM((2, D_EXPERT, d), F32),
                            pltpu.VMEM((d, 2 * D_EXPERT), BF16), pltpu.VMEM((D_EXPERT, d), BF16),
                            pltpu.SemaphoreType.DMA((2, 2))]),
        out_shape=jax.ShapeDtypeStruct((p_rows * ROW_TILES, LANES), F32),
        compiler_params=pltpu.CompilerParams(dimension_semantics=("arbitrary",),
                                             vmem_limit_bytes=VMEM_LIMIT_BYTES),
        name="moe",
    )(chunk_e, n_used, run_slot, next_e, chunk_fill, chunk_blk, xs, wgu, bgu, wd, bd)


def _fin_kernel(ys_ref, d0_ref, d1_ref, d2_ref, x1_ref, w_ref, p_ref, wpg_ref, wpp_ref, g2_ref, b2_ref,
                out_ref, rows_a, rows_b, idx_s, sems, isems):
    ts = TS_FIN
    i = pl.program_id(0)
    last = pl.num_programs(0) - 1
    bufs = (rows_a, rows_b)

    def gather_row(par, k, t):
        _row_copy(ys_ref, idx_s[par, k, t], bufs[par].at[k], t, 1, sems.at[par]).start(priority=k % 2)

    def wait_rows(par):
        for k in range(TOP_K):
            _row_copy(ys_ref, 0, bufs[par].at[k], 0, ts, sems.at[par]).wait()

    def index_copy(src_ref, par):
        return pltpu.make_async_copy(src_ref, idx_s.at[par], isems.at[par])

    @pl.when(i == 0)
    def _():
        index_copy(d0_ref, 0).start()
        index_copy(d1_ref, 1).start()
        index_copy(d0_ref, 0).wait()
        index_copy(d1_ref, 1).wait()

        def issue(tb, carry):
            for u in range(ISSUE_UNROLL):
                for k in range(TOP_K):
                    gather_row(0, k, tb * ISSUE_UNROLL + u)
            return carry

        lax.fori_loop(0, ts // ISSUE_UNROLL, issue, 0)

    for par in range(2):
        @pl.when(i % 2 == par)
        def _():
            wait_rows(par)
            for t in range(ts):
                for k in range(TOP_K):
                    gather_row(1 - par, k, t)
            index_copy(d2_ref, par).start()

            x1 = x1_ref[...]
            ple = (jax.nn.sigmoid(_dot(x1.astype(BF16), wpg_ref[...]))
                   * _dot(p_ref[...].astype(BF16), wpp_ref[...]))
            z = DN_ALPHA * x1 + ple
            w = w_ref[...]
            for k in range(TOP_K):
                yk = jnp.concatenate(
                    [bufs[par][k, pl.ds(s, ts, stride=ROW_TILES), :] for s in range(ROW_TILES)],
                    axis=-1)
                z = z + w[:, k:k + 1] * yk
            out_ref[...] = _layer_norm(z, g2_ref[...], b2_ref[...])
            index_copy(d2_ref, par).wait()

            @pl.when(i == last)
            def _():
                wait_rows(1 - par)


def _fin(dest, ys, x1, w_rows, p2, wpg, wpp, g2, b2):
    n, d = x1.shape
    nt = n // TS_FIN
    const = lambda shape: pl.BlockSpec(shape, lambda i: (0,) * len(shape))
    tile = lambda width: pl.BlockSpec((TS_FIN, width), lambda i: (i, 0))
    dtile = lambda tile_of: pl.BlockSpec((TOP_K, TS_FIN), lambda i: (0, tile_of(i)))
    return pl.pallas_call(
        _fin_kernel,
        grid=(nt,),
        in_specs=[pl.BlockSpec(memory_space=pl.ANY),
                  dtile(lambda i: 0), dtile(lambda i: 1), dtile(lambda i: jnp.minimum(i + 2, nt - 1)),
                  tile(d), tile(TOP_K), tile(D_PLE),
                  const((d, d)), const((D_PLE, d)), const((1, d)), const((1, d))],
        out_specs=tile(d),
        scratch_shapes=[pltpu.VMEM((TOP_K, TS_FIN * ROW_TILES, LANES), F32),
                        pltpu.VMEM((TOP_K, TS_FIN * ROW_TILES, LANES), F32),
                        pltpu.SMEM((2, TOP_K, TS_FIN), I32),
                        pltpu.SemaphoreType.DMA((2,)), pltpu.SemaphoreType.DMA((2,))],
        out_shape=jax.ShapeDtypeStruct((n, d), F32),
        compiler_params=pltpu.CompilerParams(dimension_semantics=("arbitrary",),
                                             vmem_limit_bytes=VMEM_LIMIT_BYTES),
        name="fin",
    )(ys, dest, dest, dest, x1, w_rows, p2, wpg, wpp, g2, b2)


def _permute_w_in(w_in):
    d = D_MODEL
    b_, c_, h_, q_ = (w_in[:, i * d:(i + 1) * d] for i in range(4))
    k_ = w_in[:, 4 * d:4 * d + KV_DIM]
    v_ = w_in[:, 4 * d + KV_DIM:4 * d + 2 * KV_DIM]
    gc_ = w_in[:, 4 * d + 2 * KV_DIM:5 * d + 2 * KV_DIM]
    ga_ = w_in[:, 5 * d + 2 * KV_DIM:]
    return jnp.concatenate([b_, c_, h_, q_, gc_, ga_, k_, v_], axis=1).astype(BF16)


def _replication_matrix():
    src = jnp.arange(KV_DIM)[:, None]
    dst = jnp.arange(D_MODEL)[None, :]
    same_head = (dst // (GQA_GROUP * HEAD_DIM)) == (src // HEAD_DIM)
    same_dim = (dst % HEAD_DIM) == (src % HEAD_DIM)
    return (same_head & same_dim).astype(BF16)


def kernel(x, p, w_in, conv_w, w_br_conv, w_br_attn, attn_sinks, w_out, ln1_g, ln1_b, w_router,
           b_router, w_gu, b_gu, w_down, b_down, w_ple_proj, w_ple_gate, ln2_g, ln2_b):
    bsz, seq, d = x.shape
    n = bsz * seq
    for i in range(DEPTH):
        wr_t = w_router[i].T
        wr_hi = wr_t.astype(BF16)
        wr_lo = (wr_t - wr_hi.astype(F32)).astype(BF16)
        p_rows = n * TOP_K + N_EXPERTS * CHUNK
        x1, idx, w_top, dest, cnt, alloc, xs = _mix(
            x, _permute_w_in(w_in[i]), _replication_matrix(), conv_w[i],
            w_br_conv[i].astype(BF16), w_br_attn[i].astype(BF16), attn_sinks[i],
            w_out[i].astype(BF16), ln1_g[i][None, :], ln1_b[i][None, :],
            wr_hi, wr_lo, b_router[i][:, None], p_rows)
        x1 = x1.reshape(n, d)

        counts = cnt[:, 0]
        padded = ((counts + CHUNK - 1) // CHUNK) * CHUNK
        end_padded = jnp.cumsum(padded)
        start_padded = end_padded - padded
        n_chunks = p_rows // CHUNK
        chunk_pos = jnp.arange(n_chunks, dtype=I32)
        chunk_start = chunk_pos * CHUNK
        chunk_e = jnp.minimum(
            jnp.sum((end_padded[None, :] <= chunk_start[:, None]).astype(I32), axis=1),
            N_EXPERTS - 1)
        n_used = (end_padded[-1:] // CHUNK).astype(I32)
        eid = jnp.arange(N_EXPERTS, dtype=I32)
        onehot = (chunk_e[:, None] == eid[None, :]).astype(I32)
        chunk_no = chunk_pos - jnp.sum(onehot * (start_padded // CHUNK)[None, :], axis=1)
        in_table = onehot[:, :, None] * (chunk_no[:, None, None] == jnp.arange(LANES, dtype=I32)[None, None, :])
        chunk_blk = jnp.where(chunk_pos < n_used[0], jnp.sum(in_table * alloc[None, :, :], axis=(1, 2)),
                              chunk_pos).astype(I32)
        active = padded > 0
        run_idx = jnp.cumsum(active.astype(I32)) - 1
        later = active[None, :] & (eid[None, :] > eid[:, None])
        nxt = jnp.min(jnp.where(later, eid[None, :], N_EXPERTS), axis=1)
        nxt = jnp.where(nxt == N_EXPERTS, -1, nxt)
        run_slot = jnp.sum(onehot * (run_idx & 1)[None, :], axis=1).astype(I32)
        next_e = jnp.sum(onehot * nxt[None, :], axis=1).astype(I32)
        chunk_fill = jnp.clip(
            jnp.sum(onehot * (counts + start_padded)[None, :], axis=1) - chunk_start, 0, CHUNK).astype(I32)
        ys = _moe(chunk_e, n_used, run_slot, next_e, chunk_fill, chunk_blk, xs, w_gu[i],
                  b_gu[i][:, None, :], w_down[i], b_down[i][:, None, :], p_rows)
        out = _fin(dest, ys, x1, w_top.T, p[i].reshape(n, D_PLE),
                   w_ple_gate[i].astype(BF16), w_ple_proj[i].astype(BF16),
                   ln2_g[i][None, :], ln2_b[i][None, :])
        x = out.reshape(bsz, seq, d)
    return x
```

```python
import math

import jax
import jax.numpy as jnp
from jax import lax
from jax.experimental import pallas as pl
from jax.experimental.pallas import tpu as pltpu

F32 = jnp.float32
BF16 = jnp.bfloat16
I32 = jnp.int32

SUBLANES = 8
LANES = 128
VMEM_LIMIT_BYTES = 56 * 1024 * 1024

D_MODEL = 1024
HEAD_DIM = 64
N_Q_HEADS = 16
N_KV_HEADS = 4
GQA_GROUP = N_Q_HEADS // N_KV_HEADS
KV_DIM = N_KV_HEADS * HEAD_DIM
WINDOW = 128
N_EXPERTS = 32
TOP_K = 4
D_EXPERT = D_MODEL
SWIGLU_LIMIT = 7.0
SWIGLU_ALPHA = 1.702
D_PLE = 256
DEPTH = 1
DN_ALPHA = (2.0 * DEPTH) ** 0.25
LN_EPS = 1e-5
ROW_TILES = D_MODEL // LANES

OFF_B, OFF_C, OFF_H, OFF_Q, OFF_GC, OFF_GA = (i * D_MODEL for i in range(6))
OFF_K = 6 * D_MODEL
OFF_V = OFF_K + KV_DIM
IN_TOTAL = OFF_V + KV_DIM

TS_MIX = 512
TRASH_ROWS = TOP_K * TS_MIX
CHUNK = 512
MOE_ROW_VARIANTS = (128, 256, CHUNK)
TS_FIN = 256
FIN_DEPTH = 3
ISSUE_UNROLL = 4


def _layer_norm(z, g, b):
    mu = jnp.mean(z, axis=-1, keepdims=True)
    zc = z - mu
    var = jnp.mean(zc * zc, axis=-1, keepdims=True)
    return zc * lax.rsqrt(var + LN_EPS) * g + b


def _dot(a, b):
    return jnp.dot(a, b, preferred_element_type=F32)


def _dot_nt(a, b):
    return lax.dot_general(a, b, (((1,), (1,)), ((), ())), preferred_element_type=F32)


def _row_copy(src, src_row, dst, dst_row, rows, sem):
    s0 = pl.multiple_of(src_row * ROW_TILES, ROW_TILES)
    d0 = pl.multiple_of(dst_row * ROW_TILES, ROW_TILES)
    return pltpu.make_async_copy(src.at[pl.ds(s0, rows * ROW_TILES)],
                                 dst.at[pl.ds(d0, rows * ROW_TILES)], sem)


def _mix_kernel(sinks_ref, x_ref, win_ref, rep_ref, convw_ref, wbrc_ref, wbra_ref, wout_ref,
                g1_ref, b1_ref, whi_ref, wlo_ref, br_ref,
                x1_ref, idx_ref, w_ref, dest_ref, cnt_ref, alloc_ref, xs_hbm,
                kext, vext, uext, upper, lower, carry, cur_id, nfree, ring, dest_v, dest_s,
                pad_v, pad_s, zbuf, dsems, rsems, psem, zsem):
    ts = TS_MIX
    j = pl.program_id(1)
    tile_id = pl.program_id(0) * pl.num_programs(1) + j
    last_tile = idx_ref.shape[1] // ts - 1
    trash_base = xs_hbm.shape[0] // ROW_TILES - TRASH_ROWS
    cur = tile_id % 2
    oth = 1 - cur

    def ring_tile(tile):
        return ((tile + 3) % 3) * ts

    def load_ring(tile):
        base = pl.multiple_of(ring_tile(tile) * ROW_TILES, ROW_TILES)
        return jnp.concatenate(
            [ring[pl.ds(base + s, ts, stride=ROW_TILES), :] for s in range(ROW_TILES)], axis=-1)

    def dest_copy(slot):
        return pltpu.make_async_copy(dest_v.at[slot], dest_s.at[slot], dsems.at[slot])

    def row_copy(tile, slot, k, t):
        return _row_copy(ring, ring_tile(tile) + t, xs_hbm, dest_s[slot, k, t], 1, rsems.at[slot])

    def wait_rows(slot):
        for k in range(TOP_K):
            _row_copy(ring, 0, xs_hbm, 0, ts, rsems.at[slot]).wait()

    route = lambda x1_tile, tile, live, slot: _route_tile(
        x1_tile, tile, live, slot, trash_base, whi_ref, wlo_ref, br_ref, upper, lower, carry, cur_id,
        nfree, idx_ref, w_ref, dest_ref, cnt_ref, alloc_ref, dest_v)

    @pl.when(j == 0)
    def _():
        kext[0:WINDOW, :] = jnp.zeros((WINDOW, D_MODEL), BF16)
        vext[0:WINDOW, :] = jnp.zeros((WINDOW, D_MODEL), BF16)
        uext[0:SUBLANES, :] = jnp.zeros((SUBLANES, D_MODEL), F32)

    @pl.when(tile_id == 0)
    def _():
        carry[...] = jnp.zeros_like(carry)
        cur_id[...] = jnp.zeros_like(cur_id)
        nfree[...] = jnp.zeros_like(nfree)
        alloc_ref[...] = jnp.full(alloc_ref.shape, -1, I32)
        ring[...] = jnp.zeros_like(ring)
        zbuf[...] = jnp.zeros_like(zbuf)
        r = lax.broadcasted_iota(I32, (ts, ts), 0)
        c = lax.broadcasted_iota(I32, (ts, ts), 1)
        upper[...] = (r < c).astype(BF16)
        r = lax.broadcasted_iota(I32, (N_EXPERTS, N_EXPERTS), 0)
        c = lax.broadcasted_iota(I32, (N_EXPERTS, N_EXPERTS), 1)
        lower[...] = (c < r).astype(BF16)
        for k in range(TOP_K):
            dest_v[0, k:k + 1, :] = _trash_rows(trash_base, k, ts)
        dest_copy(0).start()

    @pl.when(tile_id >= 1)
    def _():
        wait_rows(oth)

    route(load_ring(tile_id - 1), jnp.maximum(tile_id - 1, 0), tile_id >= 1, oth)
    dest_copy(oth).start()
    dest_copy(cur).wait()
    for t in range(ts):
        for k in range(TOP_K):
            row_copy(tile_id - 2, cur, k, t).start(priority=k % 2)

    x = x_ref[0]
    xb = x.astype(BF16)

    def proj(off, width):
        return _dot(xb, win_ref[:, off:off + width])

    u = proj(OFF_C, D_MODEL) * proj(OFF_H, D_MODEL)
    uext[SUBLANES:SUBLANES + ts, :] = u
    y = (convw_ref[2:3, :] * u
         + convw_ref[1:2, :] * uext[SUBLANES - 1:SUBLANES - 1 + ts, :]
         + convw_ref[0:1, :] * uext[SUBLANES - 2:SUBLANES - 2 + ts, :])
    uext[0:SUBLANES, :] = u[ts - SUBLANES:ts, :]
    yc_in = proj(OFF_B, D_MODEL) * y
    y_conv = _dot(yc_in.astype(BF16), wbrc_ref[...])
    acc = jax.nn.sigmoid(proj(OFF_GC, D_MODEL)) * y_conv

    q = (proj(OFF_Q, D_MODEL) * (1.0 / math.sqrt(HEAD_DIM))).astype(BF16)
    kb = proj(OFF_K, KV_DIM).astype(BF16)
    vb = proj(OFF_V, KV_DIM).astype(BF16)
    kext[WINDOW:WINDOW + ts, :] = _dot(kb, rep_ref[...]).astype(BF16)
    vext[WINDOW:WINDOW + ts, :] = _dot(vb, rep_ref[...]).astype(BF16)

    grp = GQA_GROUP * HEAD_DIM
    row = lax.broadcasted_iota(I32, (WINDOW, 2 * WINDOW), 0)
    col = lax.broadcasted_iota(I32, (WINDOW, 2 * WINDOW), 1)
    band = (col > row) & (col <= row + WINDOW)
    q_lane_grp = lax.broadcasted_iota(I32, (WINDOW, grp), 1) // HEAD_DIM
    o_blocks = []
    for i in range(ts // WINDOW):
        if i == 0:
            mask = band & ((col >= WINDOW) | (j > 0))
        else:
            mask = band
        o_heads = []
        for h in range(N_KV_HEADS):
            qh = q[i * WINDOW:(i + 1) * WINDOW, h * grp:(h + 1) * grp]
            kh = kext[i * WINDOW:i * WINDOW + 2 * WINDOW, h * grp:(h + 1) * grp]
            vh = vext[i * WINDOW:i * WINDOW + 2 * WINDOW, h * grp:(h + 1) * grp]
            qs = jnp.concatenate(
                [jnp.where(q_lane_grp == g, qh, jnp.zeros_like(qh)) for g in range(GQA_GROUP)], axis=0)
            s_all = _dot_nt(qs, kh)
            ps = []
            for g in range(GQA_GROUP):
                s = jnp.where(mask, s_all[g * WINDOW:(g + 1) * WINDOW], -jnp.inf)
                sink = sinks_ref[h * GQA_GROUP + g]
                m = jnp.maximum(jnp.max(s, axis=-1, keepdims=True), sink)
                e = jnp.exp(s - m)
                den = jnp.sum(e, axis=-1, keepdims=True) + jnp.exp(sink - m)
                ps.append((e / den).astype(BF16))
            pv = _dot(jnp.concatenate(ps, axis=0), vh)
            oh = jnp.zeros((WINDOW, grp), F32)
            for g in range(GQA_GROUP):
                oh = jnp.where(q_lane_grp == g, pv[g * WINDOW:(g + 1) * WINDOW], oh)
            o_heads.append(oh)
        o_blocks.append(jnp.concatenate(o_heads, axis=-1))
    o = jnp.concatenate(o_blocks, axis=0)
    kext[0:WINDOW, :] = kext[ts:ts + WINDOW, :]
    vext[0:WINDOW, :] = vext[ts:ts + WINDOW, :]

    y_attn = _dot(o.astype(BF16), wbra_ref[...])
    acc = acc + jax.nn.sigmoid(proj(OFF_GA, D_MODEL)) * y_attn

    z = DN_ALPHA * x + _dot(acc.astype(BF16), wout_ref[...])
    x1 = _layer_norm(z, g1_ref[...], b1_ref[...])
    x1_ref[0] = x1
    ring_base = pl.multiple_of(ring_tile(tile_id) * ROW_TILES, ROW_TILES)
    for s in range(ROW_TILES):
        ring[pl.ds(ring_base + s, ts, stride=ROW_TILES), :] = x1[:, s * LANES:(s + 1) * LANES]

    @pl.when(tile_id == last_tile)
    def _():
        def move_tile(tile, slot):
            dest_copy(slot).wait()

            def issue(tb, carry_):
                for u in range(ISSUE_UNROLL):
                    for k in range(TOP_K):
                        row_copy(tile, slot, k, tb * ISSUE_UNROLL + u).start(priority=k % 2)
                return carry_

            lax.fori_loop(0, ts // ISSUE_UNROLL, issue, 0)

        wait_rows(cur)
        move_tile(tile_id - 1, oth)
        wait_rows(oth)
        route(load_ring(tile_id), tile_id, True, cur)
        dest_copy(cur).start()
        move_tile(tile_id, cur)
        wait_rows(cur)
        _zero_unassigned_rows(carry, cur_id, nfree, xs_hbm, trash_base, pad_v, pad_s, zbuf, psem, zsem)


def _trash_rows(trash_base, k, ts):
    return trash_base + k * ts + lax.broadcasted_iota(I32, (1, ts), 1)


def _route_tile(x1, tile, live, slot, trash_base, whi_ref, wlo_ref, br_ref, upper, lower, carry, cur_id,
                nfree, idx_ref, w_ref, dest_ref, cnt_ref, alloc_ref, dest_v):
    ts = x1.shape[0]
    shift = CHUNK.bit_length() - 1
    cols = pl.ds(pl.multiple_of(tile * ts, ts), ts)
    xh = x1.astype(BF16)
    xl = (x1 - xh.astype(F32)).astype(BF16)
    whi = whi_ref[...]
    logits = _dot_nt(whi, xh) + _dot_nt(whi, xl) + _dot_nt(wlo_ref[...], xh) + br_ref[...]

    eid = lax.broadcasted_iota(I32, (N_EXPERTS, ts), 0)
    rest = logits
    sels, vals = [], []
    for k in range(TOP_K):
        m = jnp.max(rest, axis=0, keepdims=True)
        idx = jnp.min(jnp.where(rest == m, eid, N_EXPERTS), axis=0, keepdims=True)
        sel = eid == idx
        rest = jnp.where(sel, -jnp.inf, rest)
        sels.append(sel)
        vals.append(m)
        idx_ref[k:k + 1, cols] = idx
    exps = [jnp.exp(v - vals[0]) for v in vals]
    den = exps[0] + exps[1] + exps[2] + exps[3]
    for k in range(TOP_K):
        w_ref[k:k + 1, cols] = exps[k] / den

    member = jnp.zeros((N_EXPERTS, ts), F32)
    for sel in sels:
        member = member + sel.astype(F32)
    before = carry[:, 0:1]
    rank = (_dot(member.astype(BF16), upper[...]) + before).astype(I32)
    tile_cnt = jnp.sum(member, axis=1, keepdims=True)

    before_i = before.astype(I32)
    after_i = before_i + tile_cnt.astype(I32)
    blk0 = before_i >> shift
    any_rows = after_i > before_i
    new0 = any_rows & ((before_i & (CHUNK - 1)) == 0)
    blk1 = (after_i - 1) >> shift
    new1 = any_rows & (blk1 > blk0)
    is_new = new0 | new1
    new_f = jnp.broadcast_to(is_new.astype(F32), (N_EXPERTS, LANES))
    new_id = (nfree[...] + _dot(lower[...], new_f.astype(BF16)))[:, 0:1].astype(I32)
    id0 = jnp.where(new0, new_id, cur_id[:, 0:1])
    row = (jnp.where((rank >> shift) > blk0, new_id, id0) * CHUNK + (rank & (CHUNK - 1))).astype(F32)
    for k in range(TOP_K):
        dk = jnp.sum(jnp.where(sels[k], row, 0.0), axis=0, keepdims=True).astype(I32)
        dk = jnp.where(live, dk, _trash_rows(trash_base, k, ts))
        dest_ref[k:k + 1, cols] = dk
        dest_v[slot, k:k + 1, :] = dk

    took = live & is_new
    lane = lax.broadcasted_iota(I32, alloc_ref.shape, 1)
    alloc_ref[...] = jnp.where(took & (lane == jnp.where(new0, blk0, blk1)), new_id, alloc_ref[...])
    cur_id[...] = jnp.where(took, new_id, cur_id[...])
    nfree[...] = nfree[...] + jnp.where(live, jnp.sum(is_new.astype(F32), axis=0, keepdims=True), 0.0)
    carry[...] = carry[...] + jnp.where(live, tile_cnt, 0.0)
    cnt_ref[...] = carry[...].astype(I32)


def _zero_unassigned_rows(carry, cur_id, nfree, xs_hbm, trash_base, pad_v, pad_s, zbuf, psem, zsem):
    shift = CHUNK.bit_length() - 1
    count = carry[...].astype(I32)
    fill = count - (((count - 1) >> shift) << shift)
    pad = jnp.where(count > 0, CHUNK - fill, 0)
    pos = cur_id[...] * CHUNK + fill
    lane = lax.broadcasted_iota(I32, pad_v.shape, 1)
    pad_v[...] = jnp.where(lane == 0, pos, jnp.where(lane == 1, pad, nfree[...].astype(I32)))
    info = pltpu.make_async_copy(pad_v, pad_s, psem)
    info.start()
    info.wait()
    pieces = [1 << b for b in reversed(range(shift))]
    half = pieces[0]
    for wait in (False, True):
        for e in range(N_EXPERTS):
            npad = pad_s[e, 1]
            for piece in pieces:
                cp = _row_copy(zbuf, 0, xs_hbm, pad_s[e, 0] + (npad & ~(2 * piece - 1)), piece, zsem)

                @pl.when((npad & piece) != 0)
                def _():
                    cp.wait() if wait else cp.start()

        def tail(hc, carry_):
            cp = _row_copy(zbuf, 0, xs_hbm, hc * half, half, zsem)
            cp.wait() if wait else cp.start()
            return carry_

        lax.fori_loop(2 * pad_s[0, 2], trash_base // half, tail, 0)


def _mix(x, w_in_p, rep, conv_w, wbrc, wbra, sinks, wout, g1, b1, whi, wlo, br, p_rows):
    bsz, seq, d = x.shape
    n = bsz * seq
    nj = seq // TS_MIX
    const = lambda shape: pl.BlockSpec(shape, lambda b, j, s: (0,) * len(shape),
                                       pipeline_mode=pl.Buffered(1))
    tok = pl.BlockSpec((TOP_K, n), lambda b, j, s: (0, 0))
    table = pl.BlockSpec((N_EXPERTS, LANES), lambda b, j, s: (0, 0))
    grid_spec = pltpu.PrefetchScalarGridSpec(
        num_scalar_prefetch=1,
        grid=(bsz, nj),
        in_specs=[
            pl.BlockSpec((1, TS_MIX, d), lambda b, j, s: (b, j, 0)),
            const((d, IN_TOTAL)),
            const((KV_DIM, d)),
            const((3, d)),
            const((d, d)),
            const((d, d)),
            const((d, d)),
            const((1, d)),
            const((1, d)),
            const((N_EXPERTS, d)),
            const((N_EXPERTS, d)),
            const((N_EXPERTS, 1)),
        ],
        out_specs=[
            pl.BlockSpec((1, TS_MIX, d), lambda b, j, s: (b, j, 0)),
            tok, tok, tok, table, table,
            pl.BlockSpec(memory_space=pl.ANY),
        ],
        scratch_shapes=[
            pltpu.VMEM((WINDOW + TS_MIX, d), BF16),
            pltpu.VMEM((WINDOW + TS_MIX, d), BF16),
            pltpu.VMEM((SUBLANES + TS_MIX, d), F32),
            pltpu.VMEM((TS_MIX, TS_MIX), BF16),
            pltpu.VMEM((N_EXPERTS, N_EXPERTS), BF16),
            pltpu.VMEM((N_EXPERTS, LANES), F32),
            pltpu.VMEM((N_EXPERTS, LANES), I32),
            pltpu.VMEM((N_EXPERTS, LANES), F32),
            pltpu.VMEM((3 * TS_MIX * ROW_TILES, LANES), F32),
            pltpu.VMEM((2, TOP_K, TS_MIX), I32),
            pltpu.SMEM((2, TOP_K, TS_MIX), I32),
            pltpu.VMEM((N_EXPERTS, LANES), I32),
            pltpu.SMEM((N_EXPERTS, LANES), I32),
            pltpu.VMEM((CHUNK // 2 * ROW_TILES, LANES), F32),
            pltpu.SemaphoreType.DMA((2,)),
            pltpu.SemaphoreType.DMA((2,)),
            pltpu.SemaphoreType.DMA,
            pltpu.SemaphoreType.DMA,
        ],
    )
    return pl.pallas_call(
        _mix_kernel,
        grid_spec=grid_spec,
        out_shape=[jax.ShapeDtypeStruct((bsz, seq, d), F32),
                   jax.ShapeDtypeStruct((TOP_K, n), I32),
                   jax.ShapeDtypeStruct((TOP_K, n), F32),
                   jax.ShapeDtypeStruct((TOP_K, n), I32),
                   jax.ShapeDtypeStruct((N_EXPERTS, LANES), I32),
                   jax.ShapeDtypeStruct((N_EXPERTS, LANES), I32),
                   jax.ShapeDtypeStruct(((p_rows + TRASH_ROWS) * ROW_TILES, LANES), F32)],
        compiler_params=pltpu.CompilerParams(
            dimension_semantics=("arbitrary", "arbitrary"),
            vmem_limit_bytes=VMEM_LIMIT_BYTES,
            has_side_effects=True),
        name="mix",
    )(sinks, x, w_in_p, rep, conv_w, wbrc, wbra, wout, g1, b1, whi, wlo, br)


def _moe_kernel(ce_ref, nu_ref, slot_ref, nxt_ref, fill_ref, blk_ref, xs_ref, wgu_hbm, bgu_ref, wd_hbm,
                bd_ref, ys_ref, wgu_f32, wd_f32, wgu_bf, wd_bf, wsems):
    c = pl.program_id(0)
    used = c < nu_ref[0]

    def weight_copies(e, slot):
        return (pltpu.make_async_copy(wgu_hbm.at[e], wgu_f32.at[slot], wsems.at[0, slot]),
                pltpu.make_async_copy(wd_hbm.at[e], wd_f32.at[slot], wsems.at[1, slot]))

    @pl.when(used & ((c == 0) | (ce_ref[c] != ce_ref[jnp.maximum(c - 1, 0)])))
    def _():
        slot = slot_ref[c]

        @pl.when(c == 0)
        def _():
            for cp in weight_copies(ce_ref[0], slot):
                cp.start()

        for cp in weight_copies(ce_ref[c], slot):
            cp.wait()

        @pl.when(nxt_ref[c] >= 0)
        def _():
            for cp in weight_copies(nxt_ref[c], 1 - slot):
                cp.start()

        wgu_bf[...] = wgu_f32[slot].astype(BF16)
        wd_bf[...] = wd_f32[slot].astype(BF16)

    def expert(rows):
        x = jnp.concatenate(
            [xs_ref[pl.ds(s, rows, stride=ROW_TILES), :] for s in range(ROW_TILES)], axis=-1)
        gu = _dot(x.astype(BF16), wgu_bf[...]) + bgu_ref[0]
        gate = jnp.minimum(gu[:, :D_EXPERT], SWIGLU_LIMIT)
        up = jnp.clip(gu[:, D_EXPERT:], -SWIGLU_LIMIT, SWIGLU_LIMIT)
        h = (up + 1.0) * gate * jax.nn.sigmoid(SWIGLU_ALPHA * gate)
        y = _dot(h.astype(BF16), wd_bf[...]) + bd_ref[0]
        for s in range(ROW_TILES):
            ys_ref[pl.ds(s, rows, stride=ROW_TILES), :] = y[:, s * LANES:(s + 1) * LANES]
        if rows < CHUNK:
            ys_ref[rows * ROW_TILES:, :] = jnp.zeros(((CHUNK - rows) * ROW_TILES, LANES), F32)

    filled = jnp.where(used, fill_ref[jnp.minimum(c, fill_ref.shape[0] - 1)], 0)
    lower = 0
    for rows in MOE_ROW_VARIANTS:
        @pl.when((filled > lower) & (filled <= rows))
        def _():
            expert(rows)

        lower = rows

    @pl.when(filled == 0)
    def _():
        ys_ref[...] = jnp.zeros_like(ys_ref)


def _moe(chunk_e, n_used, run_slot, next_e, chunk_fill, chunk_blk, xs, wgu, bgu, wd, bd, p_rows):
    n_chunks = p_rows // CHUNK
    d = D_MODEL

    def cc(c, ce, nu, *_):
        return jnp.minimum(c, nu[0] - 1)

    rows_in = pl.BlockSpec((CHUNK * ROW_TILES, LANES), lambda c, *s: (s[5][cc(c, *s)], 0))
    rows_out = pl.BlockSpec((CHUNK * ROW_TILES, LANES), lambda c, *s: (s[5][c], 0))
    bspec = lambda shape: pl.BlockSpec(shape, lambda c, *s: (s[0][cc(c, *s)], 0, 0))
    anyspec = pl.BlockSpec(memory_space=pl.ANY)
    return pl.pallas_call(
        _moe_kernel,
        grid_spec=pltpu.PrefetchScalarGridSpec(
            num_scalar_prefetch=6, grid=(n_chunks,),
            in_specs=[rows_in, anyspec, bspec((1, 1, 2 * D_EXPERT)), anyspec, bspec((1, 1, d))],
            out_specs=rows_out,
            scratch_shapes=[pltpu.VMEM((2, d, 2 * D_EXPERT), F32), pltpu.VMEM((2, D_EXPERT, d), F32),
                            pltpu.VMEM((d, 2 * D_EXPERT), BF16), pltpu.VMEM((D_EXPERT, d), BF16),
                            pltpu.SemaphoreType.DMA((2, 2))]),
        out_shape=jax.ShapeDtypeStruct((p_rows * ROW_TILES, LANES), F32),
        compiler_params=pltpu.CompilerParams(dimension_semantics=("arbitrary",),
                                             vmem_limit_bytes=VMEM_LIMIT_BYTES),
        name="moe",
    )(chunk_e, n_used, run_slot, next_e, chunk_fill, chunk_blk, xs, wgu, bgu, wd, bd)


def _fin_kernel(ys_ref, *refs):
    depth = FIN_DEPTH
    dfirst, dnext = refs[:depth], refs[depth]
    x1_ref, w_ref, p_ref, wpg_ref, wpp_ref, g2_ref, b2_ref, out_ref = refs[depth + 1:depth + 9]
    bufs = refs[depth + 9:2 * depth + 9]
    idx_s, sems, isems = refs[2 * depth + 9:]
    ts = TS_FIN
    i = pl.program_id(0)
    last = pl.num_programs(0) - 1

    def gather_row(m, k, t):
        _row_copy(ys_ref, idx_s[m, k, t], bufs[m].at[k], t, 1, sems.at[m]).start(priority=k % 2)

    def wait_rows(m):
        for k in range(TOP_K):
            _row_copy(ys_ref, 0, bufs[m].at[k], 0, ts, sems.at[m]).wait()

    def index_copy(src_ref, m):
        return pltpu.make_async_copy(src_ref, idx_s.at[m], isems.at[m])

    @pl.when(i == 0)
    def _():
        for m in range(depth):
            index_copy(dfirst[m], m).start()
        for m in range(depth):
            index_copy(dfirst[m], m).wait()
        for m in range(depth - 1):
            def issue(tb, carry, m=m):
                for u in range(ISSUE_UNROLL):
                    for k in range(TOP_K):
                        gather_row(m, k, tb * ISSUE_UNROLL + u)
                return carry

            lax.fori_loop(0, ts // ISSUE_UNROLL, issue, 0)

    for m in range(depth):
        @pl.when(i % depth == m)
        def _():
            ahead = (m + depth - 1) % depth
            wait_rows(m)
            for t in range(ts):
                for k in range(TOP_K):
                    gather_row(ahead, k, t)
            index_copy(dnext, m).start()

            x1 = x1_ref[...]
            ple = (jax.nn.sigmoid(_dot(x1.astype(BF16), wpg_ref[...]))
                   * _dot(p_ref[...].astype(BF16), wpp_ref[...]))
            z = DN_ALPHA * x1 + ple
            w = w_ref[...]
            for k in range(TOP_K):
                yk = jnp.concatenate(
                    [bufs[m][k, pl.ds(s, ts, stride=ROW_TILES), :] for s in range(ROW_TILES)], axis=-1)
                z = z + w[:, k:k + 1] * yk
            out_ref[...] = _layer_norm(z, g2_ref[...], b2_ref[...])
            index_copy(dnext, m).wait()

            @pl.when(i == last)
            def _():
                for d in range(1, depth):
                    wait_rows((m + d) % depth)


def _fin(dest, ys, x1, w_rows, p2, wpg, wpp, g2, b2):
    n, d = x1.shape
    nt = n // TS_FIN
    const = lambda shape: pl.BlockSpec(shape, lambda i: (0,) * len(shape))
    tile = lambda width: pl.BlockSpec((TS_FIN, width), lambda i: (i, 0))
    dtile = lambda tile_of: pl.BlockSpec((TOP_K, TS_FIN), lambda i: (0, tile_of(i)))
    first = [dtile(lambda i, m=m: m) for m in range(FIN_DEPTH)]
    rowbuf = pltpu.VMEM((TOP_K, TS_FIN * ROW_TILES, LANES), F32)
    return pl.pallas_call(
        _fin_kernel,
        grid=(nt,),
        in_specs=[pl.BlockSpec(memory_space=pl.ANY)] + first
                 + [dtile(lambda i: jnp.minimum(i + FIN_DEPTH, nt - 1)),
                    tile(d), tile(TOP_K), tile(D_PLE),
                    const((d, d)), const((D_PLE, d)), const((1, d)), const((1, d))],
        out_specs=tile(d),
        scratch_shapes=[rowbuf] * FIN_DEPTH
                       + [pltpu.SMEM((FIN_DEPTH, TOP_K, TS_FIN), I32),
                          pltpu.SemaphoreType.DMA((FIN_DEPTH,)), pltpu.SemaphoreType.DMA((FIN_DEPTH,))],
        out_shape=jax.ShapeDtypeStruct((n, d), F32),
        compiler_params=pltpu.CompilerParams(dimension_semantics=("arbitrary",),
                                             vmem_limit_bytes=VMEM_LIMIT_BYTES),
        name="fin",
    )(ys, *([dest] * (FIN_DEPTH + 1)), x1, w_rows, p2, wpg, wpp, g2, b2)


def _permute_w_in(w_in):
    d = D_MODEL
    b_, c_, h_, q_ = (w_in[:, i * d:(i + 1) * d] for i in range(4))
    k_ = w_in[:, 4 * d:4 * d + KV_DIM]
    v_ = w_in[:, 4 * d + KV_DIM:4 * d + 2 * KV_DIM]
    gc_ = w_in[:, 4 * d + 2 * KV_DIM:5 * d + 2 * KV_DIM]
    ga_ = w_in[:, 5 * d + 2 * KV_DIM:]
    return jnp.concatenate([b_, c_, h_, q_, gc_, ga_, k_, v_], axis=1).astype(BF16)


def _replication_matrix():
    src = jnp.arange(KV_DIM)[:, None]
    dst = jnp.arange(D_MODEL)[None, :]
    same_head = (dst // (GQA_GROUP * HEAD_DIM)) == (src // HEAD_DIM)
    same_dim = (dst % HEAD_DIM) == (src % HEAD_DIM)
    return (same_head & same_dim).astype(BF16)


def kernel(x, p, w_in, conv_w, w_br_conv, w_br_attn, attn_sinks, w_out, ln1_g, ln1_b, w_router,
           b_router, w_gu, b_gu, w_down, b_down, w_ple_proj, w_ple_gate, ln2_g, ln2_b):
    bsz, seq, d = x.shape
    n = bsz * seq
    for i in range(DEPTH):
        wr_t = w_router[i].T
        wr_hi = wr_t.astype(BF16)
        wr_lo = (wr_t - wr_hi.astype(F32)).astype(BF16)
        p_rows = n * TOP_K + N_EXPERTS * CHUNK
        x1, idx, w_top, dest, cnt, alloc, xs = _mix(
            x, _permute_w_in(w_in[i]), _replication_matrix(), conv_w[i],
            w_br_conv[i].astype(BF16), w_br_attn[i].astype(BF16), attn_sinks[i],
            w_out[i].astype(BF16), ln1_g[i][None, :], ln1_b[i][None, :],
            wr_hi, wr_lo, b_router[i][:, None], p_rows)
        x1 = x1.reshape(n, d)

        counts = cnt[:, 0]
        padded = ((counts + CHUNK - 1) // CHUNK) * CHUNK
        end_padded = jnp.cumsum(padded)
        start_padded = end_padded - padded
        n_chunks = p_rows // CHUNK
        chunk_pos = jnp.arange(n_chunks, dtype=I32)
        chunk_start = chunk_pos * CHUNK
        chunk_e = jnp.minimum(
            jnp.sum((end_padded[None, :] <= chunk_start[:, None]).astype(I32), axis=1),
            N_EXPERTS - 1)
        n_used = (end_padded[-1:] // CHUNK).astype(I32)
        eid = jnp.arange(N_EXPERTS, dtype=I32)
        onehot = (chunk_e[:, None] == eid[None, :]).astype(I32)
        chunk_no = chunk_pos - jnp.sum(onehot * (start_padded // CHUNK)[None, :], axis=1)
        in_table = onehot[:, :, None] * (chunk_no[:, None, None] == jnp.arange(LANES, dtype=I32)[None, None, :])
        chunk_blk = jnp.where(chunk_pos < n_used[0], jnp.sum(in_table * alloc[None, :, :], axis=(1, 2)),
                              chunk_pos).astype(I32)
        active = padded > 0
        run_idx = jnp.cumsum(active.astype(I32)) - 1
        later = active[None, :] & (eid[None, :] > eid[:, None])
        nxt = jnp.min(jnp.where(later, eid[None, :], N_EXPERTS), axis=1)
        nxt = jnp.where(nxt == N_EXPERTS, -1, nxt)
        run_slot = jnp.sum(onehot * (run_idx & 1)[None, :], axis=1).astype(I32)
        next_e = jnp.sum(onehot * nxt[None, :], axis=1).astype(I32)
        chunk_fill = jnp.clip(
            jnp.sum(onehot * (counts + start_padded)[None, :], axis=1) - chunk_start, 0, CHUNK).astype(I32)
        ys = _moe(chunk_e, n_used, run_slot, next_e, chunk_fill, chunk_blk, xs, w_gu[i],
                  b_gu[i][:, None, :], w_down[i], b_down[i][:, None, :], p_rows)
        out = _fin(dest, ys, x1, w_top.T, p[i].reshape(n, D_PLE),
                   w_ple_gate[i].astype(BF16), w_ple_proj[i].astype(BF16),
                   ln2_g[i][None, :], ln2_b[i][None, :])
        x = out.reshape(bsz, seq, d)
    return x
```

```python
import math

import jax
import jax.numpy as jnp
from jax import lax
from jax.experimental import pallas as pl
from jax.experimental.pallas import tpu as pltpu

F32 = jnp.float32
BF16 = jnp.bfloat16
I32 = jnp.int32

SUBLANES = 8
LANES = 128
VMEM_LIMIT_BYTES = 56 * 1024 * 1024

D_MODEL = 1024
HEAD_DIM = 64
N_Q_HEADS = 16
N_KV_HEADS = 4
GQA_GROUP = N_Q_HEADS // N_KV_HEADS
KV_DIM = N_KV_HEADS * HEAD_DIM
WINDOW = 128
N_EXPERTS = 32
TOP_K = 4
D_EXPERT = D_MODEL
SWIGLU_LIMIT = 7.0
SWIGLU_ALPHA = 1.702
D_PLE = 256
DEPTH = 1
DN_ALPHA = (2.0 * DEPTH) ** 0.25
LN_EPS = 1e-5
ROW_TILES = D_MODEL // LANES

OFF_B, OFF_C, OFF_H, OFF_Q, OFF_K = (i * D_MODEL for i in range(5))
OFF_V = OFF_K + KV_DIM
OFF_GC = OFF_V + KV_DIM
OFF_GA = OFF_GC + D_MODEL
IN_TOTAL = OFF_GA + D_MODEL

TS_MIX = 512
TRASH_ROWS = TOP_K * TS_MIX
CHUNK = 512
MOE_ROW_VARIANTS = (128, 256, CHUNK)
TS_FIN = 256
FIN_DEPTH = 3
ISSUE_UNROLL = 4


def _layer_norm(z, g, b):
    mu = jnp.mean(z, axis=-1, keepdims=True)
    zc = z - mu
    var = jnp.mean(zc * zc, axis=-1, keepdims=True)
    return zc * lax.rsqrt(var + LN_EPS) * g + b


def _dot(a, b):
    return jnp.dot(a, b, preferred_element_type=F32)


def _dot_nt(a, b):
    return lax.dot_general(a, b, (((1,), (1,)), ((), ())), preferred_element_type=F32)


def _row_copy(src, src_row, dst, dst_row, rows, sem):
    s0 = pl.multiple_of(src_row * ROW_TILES, ROW_TILES)
    d0 = pl.multiple_of(dst_row * ROW_TILES, ROW_TILES)
    return pltpu.make_async_copy(src.at[pl.ds(s0, rows * ROW_TILES)],
                                 dst.at[pl.ds(d0, rows * ROW_TILES)], sem)


def _mix_kernel(sinks_ref, x_ref, win_ref, rep_ref, convw_ref, wbrc_ref, wbra_ref, wout_ref,
                g1_ref, b1_ref, whi_ref, wlo_ref, br_ref,
                x1_ref, idx_ref, w_ref, dest_ref, cnt_ref, alloc_ref, xs_hbm,
                kext, vext, uext, upper, lower, carry, cur_id, nfree, ring, dest_v, dest_s,
                pad_v, pad_s, zbuf, dsems, rsems, psem, zsem):
    ts = TS_MIX
    j = pl.program_id(1)
    tile_id = pl.program_id(0) * pl.num_programs(1) + j
    last_tile = idx_ref.shape[1] // ts - 1
    trash_base = xs_hbm.shape[0] // ROW_TILES - TRASH_ROWS
    cur = tile_id % 2
    oth = 1 - cur

    def ring_tile(tile):
        return ((tile + 3) % 3) * ts

    def load_ring(tile):
        base = pl.multiple_of(ring_tile(tile) * ROW_TILES, ROW_TILES)
        return jnp.concatenate(
            [ring[pl.ds(base + s, ts, stride=ROW_TILES), :] for s in range(ROW_TILES)], axis=-1)

    def dest_copy(slot):
        return pltpu.make_async_copy(dest_v.at[slot], dest_s.at[slot], dsems.at[slot])

    def row_copy(tile, slot, k, t):
        return _row_copy(ring, ring_tile(tile) + t, xs_hbm, dest_s[slot, k, t], 1, rsems.at[slot])

    def wait_rows(slot):
        for k in range(TOP_K):
            _row_copy(ring, 0, xs_hbm, 0, ts, rsems.at[slot]).wait()

    route = lambda x1_tile, tile, live, slot: _route_tile(
        x1_tile, tile, live, slot, trash_base, whi_ref, wlo_ref, br_ref, upper, lower, carry, cur_id,
        nfree, idx_ref, w_ref, dest_ref, cnt_ref, alloc_ref, dest_v)

    @pl.when(j == 0)
    def _():
        kext[0:WINDOW, :] = jnp.zeros((WINDOW, D_MODEL), BF16)
        vext[0:WINDOW, :] = jnp.zeros((WINDOW, D_MODEL), BF16)
        uext[0:SUBLANES, :] = jnp.zeros((SUBLANES, D_MODEL), F32)

    @pl.when(tile_id == 0)
    def _():
        carry[...] = jnp.zeros_like(carry)
        cur_id[...] = jnp.zeros_like(cur_id)
        nfree[...] = jnp.zeros_like(nfree)
        alloc_ref[...] = jnp.full(alloc_ref.shape, -1, I32)
        ring[...] = jnp.zeros_like(ring)
        zbuf[...] = jnp.zeros_like(zbuf)
        r = lax.broadcasted_iota(I32, (ts, ts), 0)
        c = lax.broadcasted_iota(I32, (ts, ts), 1)
        upper[...] = (r < c).astype(BF16)
        r = lax.broadcasted_iota(I32, (N_EXPERTS, N_EXPERTS), 0)
        c = lax.broadcasted_iota(I32, (N_EXPERTS, N_EXPERTS), 1)
        lower[...] = (c < r).astype(BF16)
        for k in range(TOP_K):
            dest_v[0, k:k + 1, :] = _trash_rows(trash_base, k, ts)
        dest_copy(0).start()

    @pl.when(tile_id >= 1)
    def _():
        wait_rows(oth)

    route(load_ring(tile_id - 1), jnp.maximum(tile_id - 1, 0), tile_id >= 1, oth)
    dest_copy(oth).start()
    dest_copy(cur).wait()
    for t in range(ts):
        for k in range(TOP_K):
            row_copy(tile_id - 2, cur, k, t).start(priority=k % 2)

    x = x_ref[0]
    xb = x.astype(BF16)

    def proj(off, width):
        return _dot(xb, win_ref[:, off:off + width])

    u = proj(OFF_C, D_MODEL) * proj(OFF_H, D_MODEL)
    uext[SUBLANES:SUBLANES + ts, :] = u
    y = (convw_ref[2:3, :] * u
         + convw_ref[1:2, :] * uext[SUBLANES - 1:SUBLANES - 1 + ts, :]
         + convw_ref[0:1, :] * uext[SUBLANES - 2:SUBLANES - 2 + ts, :])
    uext[0:SUBLANES, :] = u[ts - SUBLANES:ts, :]
    yc_in = proj(OFF_B, D_MODEL) * y
    y_conv = _dot(yc_in.astype(BF16), wbrc_ref[...])
    acc = jax.nn.sigmoid(proj(OFF_GC, D_MODEL)) * y_conv

    q = (proj(OFF_Q, D_MODEL) * (1.0 / math.sqrt(HEAD_DIM))).astype(BF16)
    kb = proj(OFF_K, KV_DIM).astype(BF16)
    vb = proj(OFF_V, KV_DIM).astype(BF16)
    kext[WINDOW:WINDOW + ts, :] = _dot(kb, rep_ref[...]).astype(BF16)
    vext[WINDOW:WINDOW + ts, :] = _dot(vb, rep_ref[...]).astype(BF16)

    grp = GQA_GROUP * HEAD_DIM
    row = lax.broadcasted_iota(I32, (WINDOW, 2 * WINDOW), 0)
    col = lax.broadcasted_iota(I32, (WINDOW, 2 * WINDOW), 1)
    band = (col > row) & (col <= row + WINDOW)
    q_lane_grp = lax.broadcasted_iota(I32, (WINDOW, grp), 1) // HEAD_DIM
    o_blocks = []
    for i in range(ts // WINDOW):
        if i == 0:
            mask = band & ((col >= WINDOW) | (j > 0))
        else:
            mask = band
        o_heads = []
        for h in range(N_KV_HEADS):
            qh = q[i * WINDOW:(i + 1) * WINDOW, h * grp:(h + 1) * grp]
            kh = kext[i * WINDOW:i * WINDOW + 2 * WINDOW, h * grp:(h + 1) * grp]
            vh = vext[i * WINDOW:i * WINDOW + 2 * WINDOW, h * grp:(h + 1) * grp]
            qs = jnp.concatenate(
                [jnp.where(q_lane_grp == g, qh, jnp.zeros_like(qh)) for g in range(GQA_GROUP)], axis=0)
            s_all = _dot_nt(qs, kh)
            ps = []
            for g in range(GQA_GROUP):
                s = jnp.where(mask, s_all[g * WINDOW:(g + 1) * WINDOW], -jnp.inf)
                sink = sinks_ref[h * GQA_GROUP + g]
                m = jnp.maximum(jnp.max(s, axis=-1, keepdims=True), sink)
                e = jnp.exp(s - m)
                den = jnp.sum(e, axis=-1, keepdims=True) + jnp.exp(sink - m)
                ps.append((e / den).astype(BF16))
            pv = _dot(jnp.concatenate(ps, axis=0), vh)
            oh = jnp.zeros((WINDOW, grp), F32)
            for g in range(GQA_GROUP):
                oh = jnp.where(q_lane_grp == g, pv[g * WINDOW:(g + 1) * WINDOW], oh)
            o_heads.append(oh)
        o_blocks.append(jnp.concatenate(o_heads, axis=-1))
    o = jnp.concatenate(o_blocks, axis=0)
    kext[0:WINDOW, :] = kext[ts:ts + WINDOW, :]
    vext[0:WINDOW, :] = vext[ts:ts + WINDOW, :]

    y_attn = _dot(o.astype(BF16), wbra_ref[...])
    acc = acc + jax.nn.sigmoid(proj(OFF_GA, D_MODEL)) * y_attn

    z = DN_ALPHA * x + _dot(acc.astype(BF16), wout_ref[...])
    x1 = _layer_norm(z, g1_ref[...], b1_ref[...])
    x1_ref[0] = x1
    ring_base = pl.multiple_of(ring_tile(tile_id) * ROW_TILES, ROW_TILES)
    for s in range(ROW_TILES):
        ring[pl.ds(ring_base + s, ts, stride=ROW_TILES), :] = x1[:, s * LANES:(s + 1) * LANES]

    @pl.when(tile_id == last_tile)
    def _():
        def move_tile(tile, slot):
            dest_copy(slot).wait()

            def issue(tb, carry_):
                for u in range(ISSUE_UNROLL):
                    for k in range(TOP_K):
                        row_copy(tile, slot, k, tb * ISSUE_UNROLL + u).start(priority=k % 2)
                return carry_

            lax.fori_loop(0, ts // ISSUE_UNROLL, issue, 0)

        wait_rows(cur)
        move_tile(tile_id - 1, oth)
        wait_rows(oth)
        route(load_ring(tile_id), tile_id, True, cur)
        dest_copy(cur).start()
        move_tile(tile_id, cur)
        wait_rows(cur)
        _zero_unassigned_rows(carry, cur_id, nfree, xs_hbm, trash_base, pad_v, pad_s, zbuf, psem, zsem)


def _trash_rows(trash_base, k, ts):
    return trash_base + k * ts + lax.broadcasted_iota(I32, (1, ts), 1)


def _route_tile(x1, tile, live, slot, trash_base, whi_ref, wlo_ref, br_ref, upper, lower, carry, cur_id,
                nfree, idx_ref, w_ref, dest_ref, cnt_ref, alloc_ref, dest_v):
    ts = x1.shape[0]
    shift = CHUNK.bit_length() - 1
    cols = pl.ds(pl.multiple_of(tile * ts, ts), ts)
    xh = x1.astype(BF16)
    xl = (x1 - xh.astype(F32)).astype(BF16)
    whi = whi_ref[...]
    logits = _dot_nt(whi, xh) + _dot_nt(whi, xl) + _dot_nt(wlo_ref[...], xh) + br_ref[...]

    eid = lax.broadcasted_iota(I32, (N_EXPERTS, ts), 0)
    rest = logits
    sels, vals = [], []
    for k in range(TOP_K):
        m = jnp.max(rest, axis=0, keepdims=True)
        idx = jnp.min(jnp.where(rest == m, eid, N_EXPERTS), axis=0, keepdims=True)
        sel = eid == idx
        rest = jnp.where(sel, -jnp.inf, rest)
        sels.append(sel)
        vals.append(m)
        idx_ref[k:k + 1, cols] = idx
    exps = [jnp.exp(v - vals[0]) for v in vals]
    den = exps[0] + exps[1] + exps[2] + exps[3]
    for k in range(TOP_K):
        w_ref[k:k + 1, cols] = exps[k] / den

    member = jnp.zeros((N_EXPERTS, ts), F32)
    for sel in sels:
        member = member + sel.astype(F32)
    before = carry[:, 0:1]
    rank = (_dot(member.astype(BF16), upper[...]) + before).astype(I32)
    tile_cnt = jnp.sum(member, axis=1, keepdims=True)

    before_i = before.astype(I32)
    after_i = before_i + tile_cnt.astype(I32)
    blk0 = before_i >> shift
    any_rows = after_i > before_i
    new0 = any_rows & ((before_i & (CHUNK - 1)) == 0)
    blk1 = (after_i - 1) >> shift
    new1 = any_rows & (blk1 > blk0)
    is_new = new0 | new1
    new_f = jnp.broadcast_to(is_new.astype(F32), (N_EXPERTS, LANES))
    new_id = (nfree[...] + _dot(lower[...], new_f.astype(BF16)))[:, 0:1].astype(I32)
    id0 = jnp.where(new0, new_id, cur_id[:, 0:1])
    row = (jnp.where((rank >> shift) > blk0, new_id, id0) * CHUNK + (rank & (CHUNK - 1))).astype(F32)
    for k in range(TOP_K):
        dk = jnp.sum(jnp.where(sels[k], row, 0.0), axis=0, keepdims=True).astype(I32)
        dk = jnp.where(live, dk, _trash_rows(trash_base, k, ts))
        dest_ref[k:k + 1, cols] = dk
        dest_v[slot, k:k + 1, :] = dk

    took = live & is_new
    lane = lax.broadcasted_iota(I32, alloc_ref.shape, 1)
    alloc_ref[...] = jnp.where(took & (lane == jnp.where(new0, blk0, blk1)), new_id, alloc_ref[...])
    cur_id[...] = jnp.where(took, new_id, cur_id[...])
    nfree[...] = nfree[...] + jnp.where(live, jnp.sum(is_new.astype(F32), axis=0, keepdims=True), 0.0)
    carry[...] = carry[...] + jnp.where(live, tile_cnt, 0.0)
    cnt_ref[...] = carry[...].astype(I32)


def _zero_unassigned_rows(carry, cur_id, nfree, xs_hbm, trash_base, pad_v, pad_s, zbuf, psem, zsem):
    shift = CHUNK.bit_length() - 1
    count = carry[...].astype(I32)
    fill = count - (((count - 1) >> shift) << shift)
    pad = jnp.where(count > 0, CHUNK - fill, 0)
    pos = cur_id[...] * CHUNK + fill
    lane = lax.broadcasted_iota(I32, pad_v.shape, 1)
    pad_v[...] = jnp.where(lane == 0, pos, jnp.where(lane == 1, pad, nfree[...].astype(I32)))
    info = pltpu.make_async_copy(pad_v, pad_s, psem)
    info.start()
    info.wait()
    pieces = [1 << b for b in reversed(range(shift))]
    half = pieces[0]
    for wait in (False, True):
        for e in range(N_EXPERTS):
            npad = pad_s[e, 1]
            for piece in pieces:
                cp = _row_copy(zbuf, 0, xs_hbm, pad_s[e, 0] + (npad & ~(2 * piece - 1)), piece, zsem)

                @pl.when((npad & piece) != 0)
                def _():
                    cp.wait() if wait else cp.start()

        def tail(hc, carry_):
            cp = _row_copy(zbuf, 0, xs_hbm, hc * half, half, zsem)
            cp.wait() if wait else cp.start()
            return carry_

        lax.fori_loop(2 * pad_s[0, 2], trash_base // half, tail, 0)


def _mix(x, w_in_p, rep, conv_w, wbrc, wbra, sinks, wout, g1, b1, whi, wlo, br, p_rows):
    bsz, seq, d = x.shape
    n = bsz * seq
    nj = seq // TS_MIX
    const = lambda shape: pl.BlockSpec(shape, lambda b, j, s: (0,) * len(shape),
                                       pipeline_mode=pl.Buffered(1))
    tok = pl.BlockSpec((TOP_K, n), lambda b, j, s: (0, 0))
    table = pl.BlockSpec((N_EXPERTS, LANES), lambda b, j, s: (0, 0))
    grid_spec = pltpu.PrefetchScalarGridSpec(
        num_scalar_prefetch=1,
        grid=(bsz, nj),
        in_specs=[
            pl.BlockSpec((1, TS_MIX, d), lambda b, j, s: (b, j, 0)),
            const((d, IN_TOTAL)),
            const((KV_DIM, d)),
            const((3, d)),
            const((d, d)),
            const((d, d)),
            const((d, d)),
            const((1, d)),
            const((1, d)),
            const((N_EXPERTS, d)),
            const((N_EXPERTS, d)),
            const((N_EXPERTS, 1)),
        ],
        out_specs=[
            pl.BlockSpec((1, TS_MIX, d), lambda b, j, s: (b, j, 0)),
            tok, tok, tok, table, table,
            pl.BlockSpec(memory_space=pl.ANY),
        ],
        scratch_shapes=[
            pltpu.VMEM((WINDOW + TS_MIX, d), BF16),
            pltpu.VMEM((WINDOW + TS_MIX, d), BF16),
            pltpu.VMEM((SUBLANES + TS_MIX, d), F32),
            pltpu.VMEM((TS_MIX, TS_MIX), BF16),
            pltpu.VMEM((N_EXPERTS, N_EXPERTS), BF16),
            pltpu.VMEM((N_EXPERTS, LANES), F32),
            pltpu.VMEM((N_EXPERTS, LANES), I32),
            pltpu.VMEM((N_EXPERTS, LANES), F32),
            pltpu.VMEM((3 * TS_MIX * ROW_TILES, LANES), F32),
            pltpu.VMEM((2, TOP_K, TS_MIX), I32),
            pltpu.SMEM((2, TOP_K, TS_MIX), I32),
            pltpu.VMEM((N_EXPERTS, LANES), I32),
            pltpu.SMEM((N_EXPERTS, LANES), I32),
            pltpu.VMEM((CHUNK // 2 * ROW_TILES, LANES), F32),
            pltpu.SemaphoreType.DMA((2,)),
            pltpu.SemaphoreType.DMA((2,)),
            pltpu.SemaphoreType.DMA,
            pltpu.SemaphoreType.DMA,
        ],
    )
    return pl.pallas_call(
        _mix_kernel,
        grid_spec=grid_spec,
        out_shape=[jax.ShapeDtypeStruct((bsz, seq, d), F32),
                   jax.ShapeDtypeStruct((TOP_K, n), I32),
                   jax.ShapeDtypeStruct((TOP_K, n), F32),
                   jax.ShapeDtypeStruct((TOP_K, n), I32),
                   jax.ShapeDtypeStruct((N_EXPERTS, LANES), I32),
                   jax.ShapeDtypeStruct((N_EXPERTS, LANES), I32),
                   jax.ShapeDtypeStruct(((p_rows + TRASH_ROWS) * ROW_TILES, LANES), F32)],
        compiler_params=pltpu.CompilerParams(
            dimension_semantics=("arbitrary", "arbitrary"),
            vmem_limit_bytes=VMEM_LIMIT_BYTES,
            has_side_effects=True),
        name="mix",
    )(sinks, x, w_in_p, rep, conv_w, wbrc, wbra, wout, g1, b1, whi, wlo, br)


def _moe_kernel(ce_ref, nu_ref, slot_ref, nxt_ref, fill_ref, blk_ref, xs_ref, wgu_hbm, bgu_ref, wd_hbm,
                bd_ref, ys_ref, wgu_f32, wd_f32, wgu_bf, wd_bf, wsems):
    c = pl.program_id(0)
    used = c < nu_ref[0]

    def weight_copies(e, slot):
        return (pltpu.make_async_copy(wgu_hbm.at[e], wgu_f32.at[slot], wsems.at[0, slot]),
                pltpu.make_async_copy(wd_hbm.at[e], wd_f32.at[slot], wsems.at[1, slot]))

    @pl.when(used & ((c == 0) | (ce_ref[c] != ce_ref[jnp.maximum(c - 1, 0)])))
    def _():
        slot = slot_ref[c]

        @pl.when(c == 0)
        def _():
            for cp in weight_copies(ce_ref[0], slot):
                cp.start()

        for cp in weight_copies(ce_ref[c], slot):
            cp.wait()

        @pl.when(nxt_ref[c] >= 0)
        def _():
            for cp in weight_copies(nxt_ref[c], 1 - slot):
                cp.start()

        wgu_bf[...] = wgu_f32[slot].astype(BF16)
        wd_bf[...] = wd_f32[slot].astype(BF16)

    def expert(rows):
        x = jnp.concatenate(
            [xs_ref[pl.ds(s, rows, stride=ROW_TILES), :] for s in range(ROW_TILES)], axis=-1)
        gu = _dot(x.astype(BF16), wgu_bf[...]) + bgu_ref[0]
        gate = jnp.minimum(gu[:, :D_EXPERT], SWIGLU_LIMIT)
        up = jnp.clip(gu[:, D_EXPERT:], -SWIGLU_LIMIT, SWIGLU_LIMIT)
        h = (up + 1.0) * gate * jax.nn.sigmoid(SWIGLU_ALPHA * gate)
        y = _dot(h.astype(BF16), wd_bf[...]) + bd_ref[0]
        for s in range(ROW_TILES):
            ys_ref[pl.ds(s, rows, stride=ROW_TILES), :] = y[:, s * LANES:(s + 1) * LANES]
        if rows < CHUNK:
            ys_ref[rows * ROW_TILES:, :] = jnp.zeros(((CHUNK - rows) * ROW_TILES, LANES), F32)

    filled = jnp.where(used, fill_ref[jnp.minimum(c, fill_ref.shape[0] - 1)], 0)
    lower = 0
    for rows in MOE_ROW_VARIANTS:
        @pl.when((filled > lower) & (filled <= rows))
        def _():
            expert(rows)

        lower = rows

    @pl.when(filled == 0)
    def _():
        ys_ref[...] = jnp.zeros_like(ys_ref)


def _moe(chunk_e, n_used, run_slot, next_e, chunk_fill, chunk_blk, xs, wgu, bgu, wd, bd, p_rows):
    n_chunks = p_rows // CHUNK
    d = D_MODEL

    def cc(c, ce, nu, *_):
        return jnp.minimum(c, nu[0] - 1)

    rows_in = pl.BlockSpec((CHUNK * ROW_TILES, LANES), lambda c, *s: (s[5][cc(c, *s)], 0))
    rows_out = pl.BlockSpec((CHUNK * ROW_TILES, LANES), lambda c, *s: (s[5][c], 0))
    bspec = lambda shape: pl.BlockSpec(shape, lambda c, *s: (s[0][cc(c, *s)], 0, 0))
    anyspec = pl.BlockSpec(memory_space=pl.ANY)
    return pl.pallas_call(
        _moe_kernel,
        grid_spec=pltpu.PrefetchScalarGridSpec(
            num_scalar_prefetch=6, grid=(n_chunks,),
            in_specs=[rows_in, anyspec, bspec((1, 1, 2 * D_EXPERT)), anyspec, bspec((1, 1, d))],
            out_specs=rows_out,
            scratch_shapes=[pltpu.VMEM((2, d, 2 * D_EXPERT), F32), pltpu.VMEM((2, D_EXPERT, d), F32),
                            pltpu.VMEM((d, 2 * D_EXPERT), BF16), pltpu.VMEM((D_EXPERT, d), BF16),
                            pltpu.SemaphoreType.DMA((2, 2))]),
        out_shape=jax.ShapeDtypeStruct((p_rows * ROW_TILES, LANES), F32),
        compiler_params=pltpu.CompilerParams(dimension_semantics=("arbitrary",),
                                             vmem_limit_bytes=VMEM_LIMIT_BYTES),
        name="moe",
    )(chunk_e, n_used, run_slot, next_e, chunk_fill, chunk_blk, xs, wgu, bgu, wd, bd)


def _fin_kernel(ys_ref, *refs):
    depth = FIN_DEPTH
    dfirst, dnext = refs[:depth], refs[depth]
    x1_ref, w_ref, p_ref, wpg_ref, wpp_ref, g2_ref, b2_ref, out_ref = refs[depth + 1:depth + 9]
    bufs = refs[depth + 9:2 * depth + 9]
    idx_s, sems, isems = refs[2 * depth + 9:]
    ts = TS_FIN
    i = pl.program_id(0)
    last = pl.num_programs(0) - 1

    def gather_row(m, k, t):
        _row_copy(ys_ref, idx_s[m, k, t], bufs[m].at[k], t, 1, sems.at[m]).start(priority=k % 2)

    def wait_rows(m):
        for k in range(TOP_K):
            _row_copy(ys_ref, 0, bufs[m].at[k], 0, ts, sems.at[m]).wait()

    def index_copy(src_ref, m):
        return pltpu.make_async_copy(src_ref, idx_s.at[m], isems.at[m])

    @pl.when(i == 0)
    def _():
        for m in range(depth):
            index_copy(dfirst[m], m).start()
        for m in range(depth):
            index_copy(dfirst[m], m).wait()
        for m in range(depth - 1):
            def issue(tb, carry, m=m):
                for u in range(ISSUE_UNROLL):
                    for k in range(TOP_K):
                        gather_row(m, k, tb * ISSUE_UNROLL + u)
                return carry

            lax.fori_loop(0, ts // ISSUE_UNROLL, issue, 0)

    for m in range(depth):
        @pl.when(i % depth == m)
        def _():
            ahead = (m + depth - 1) % depth
            wait_rows(m)
            for t in range(ts):
                for k in range(TOP_K):
                    gather_row(ahead, k, t)
            index_copy(dnext, m).start()

            x1 = x1_ref[...]
            ple = (jax.nn.sigmoid(_dot(x1.astype(BF16), wpg_ref[...]))
                   * _dot(p_ref[...].astype(BF16), wpp_ref[...]))
            z = DN_ALPHA * x1 + ple
            w = w_ref[...]
            for k in range(TOP_K):
                yk = jnp.concatenate(
                    [bufs[m][k, pl.ds(s, ts, stride=ROW_TILES), :] for s in range(ROW_TILES)], axis=-1)
                z = z + w[:, k:k + 1] * yk
            out_ref[...] = _layer_norm(z, g2_ref[...], b2_ref[...])
            index_copy(dnext, m).wait()

            @pl.when(i == last)
            def _():
                for d in range(1, depth):
                    wait_rows((m + d) % depth)


def _fin(dest, ys, x1, w_rows, p2, wpg, wpp, g2, b2):
    n, d = x1.shape
    nt = n // TS_FIN
    const = lambda shape: pl.BlockSpec(shape, lambda i: (0,) * len(shape))
    tile = lambda width: pl.BlockSpec((TS_FIN, width), lambda i: (i, 0))
    dtile = lambda tile_of: pl.BlockSpec((TOP_K, TS_FIN), lambda i: (0, tile_of(i)))
    first = [dtile(lambda i, m=m: m) for m in range(FIN_DEPTH)]
    rowbuf = pltpu.VMEM((TOP_K, TS_FIN * ROW_TILES, LANES), F32)
    return pl.pallas_call(
        _fin_kernel,
        grid=(nt,),
        in_specs=[pl.BlockSpec(memory_space=pl.ANY)] + first
                 + [dtile(lambda i: jnp.minimum(i + FIN_DEPTH, nt - 1)),
                    tile(d), tile(TOP_K), tile(D_PLE),
                    const((d, d)), const((D_PLE, d)), const((1, d)), const((1, d))],
        out_specs=tile(d),
        scratch_shapes=[rowbuf] * FIN_DEPTH
                       + [pltpu.SMEM((FIN_DEPTH, TOP_K, TS_FIN), I32),
                          pltpu.SemaphoreType.DMA((FIN_DEPTH,)), pltpu.SemaphoreType.DMA((FIN_DEPTH,))],
        out_shape=jax.ShapeDtypeStruct((n, d), F32),
        compiler_params=pltpu.CompilerParams(dimension_semantics=("arbitrary",),
                                             vmem_limit_bytes=VMEM_LIMIT_BYTES),
        name="fin",
    )(ys, *([dest] * (FIN_DEPTH + 1)), x1, w_rows, p2, wpg, wpp, g2, b2)


def _replication_matrix():
    src = jnp.arange(KV_DIM)[:, None]
    dst = jnp.arange(D_MODEL)[None, :]
    same_head = (dst // (GQA_GROUP * HEAD_DIM)) == (src // HEAD_DIM)
    same_dim = (dst % HEAD_DIM) == (src % HEAD_DIM)
    return (same_head & same_dim).astype(BF16)


def kernel(x, p, w_in, conv_w, w_br_conv, w_br_attn, attn_sinks, w_out, ln1_g, ln1_b, w_router,
           b_router, w_gu, b_gu, w_down, b_down, w_ple_proj, w_ple_gate, ln2_g, ln2_b):
    bsz, seq, d = x.shape
    n = bsz * seq
    for i in range(DEPTH):
        wr_t = w_router[i].T
        wr_hi = wr_t.astype(BF16)
        wr_lo = (wr_t - wr_hi.astype(F32)).astype(BF16)
        p_rows = n * TOP_K + N_EXPERTS * CHUNK
        x1, idx, w_top, dest, cnt, alloc, xs = _mix(
            x, w_in[i].astype(BF16), _replication_matrix(), conv_w[i],
            w_br_conv[i].astype(BF16), w_br_attn[i].astype(BF16), attn_sinks[i],
            w_out[i].astype(BF16), ln1_g[i][None, :], ln1_b[i][None, :],
            wr_hi, wr_lo, b_router[i][:, None], p_rows)
        x1 = x1.reshape(n, d)

        counts = cnt[:, 0]
        padded = ((counts + CHUNK - 1) // CHUNK) * CHUNK
        end_padded = jnp.cumsum(padded)
        start_padded = end_padded - padded
        n_chunks = p_rows // CHUNK
        chunk_pos = jnp.arange(n_chunks, dtype=I32)
        chunk_start = chunk_pos * CHUNK
        chunk_e = jnp.minimum(
            jnp.sum((end_padded[None, :] <= chunk_start[:, None]).astype(I32), axis=1),
            N_EXPERTS - 1)
        n_used = (end_padded[-1:] // CHUNK).astype(I32)
        eid = jnp.arange(N_EXPERTS, dtype=I32)
        onehot = (chunk_e[:, None] == eid[None, :]).astype(I32)
        chunk_no = chunk_pos - jnp.sum(onehot * (start_padded // CHUNK)[None, :], axis=1)
        in_table = onehot[:, :, None] * (chunk_no[:, None, None] == jnp.arange(LANES, dtype=I32)[None, None, :])
        chunk_blk = jnp.where(chunk_pos < n_used[0], jnp.sum(in_table * alloc[None, :, :], axis=(1, 2)),
                              chunk_pos).astype(I32)
        active = padded > 0
        run_idx = jnp.cumsum(active.astype(I32)) - 1
        later = active[None, :] & (eid[None, :] > eid[:, None])
        nxt = jnp.min(jnp.where(later, eid[None, :], N_EXPERTS), axis=1)
        nxt = jnp.where(nxt == N_EXPERTS, -1, nxt)
        run_slot = jnp.sum(onehot * (run_idx & 1)[None, :], axis=1).astype(I32)
        next_e = jnp.sum(onehot * nxt[None, :], axis=1).astype(I32)
        chunk_fill = jnp.clip(
            jnp.sum(onehot * (counts + start_padded)[None, :], axis=1) - chunk_start, 0, CHUNK).astype(I32)
        ys = _moe(chunk_e, n_used, run_slot, next_e, chunk_fill, chunk_blk, xs, w_gu[i],
                  b_gu[i][:, None, :], w_down[i], b_down[i][:, None, :], p_rows)
        out = _fin(dest, ys, x1, w_top.T, p[i].reshape(n, D_PLE),
                   w_ple_gate[i].astype(BF16), w_ple_proj[i].astype(BF16),
                   ln2_g[i][None, :], ln2_b[i][None, :])
        x = out.reshape(bsz, seq, d)
    return x
```

```python
import math

import jax
import jax.numpy as jnp
from jax import lax
from jax.experimental import pallas as pl
from jax.experimental.pallas import tpu as pltpu

F32 = jnp.float32
BF16 = jnp.bfloat16
I32 = jnp.int32

SUBLANES = 8
LANES = 128
VMEM_LIMIT_BYTES = 56 * 1024 * 1024

D_MODEL = 1024
HEAD_DIM = 64
N_Q_HEADS = 16
N_KV_HEADS = 4
GQA_GROUP = N_Q_HEADS // N_KV_HEADS
KV_DIM = N_KV_HEADS * HEAD_DIM
WINDOW = 128
N_EXPERTS = 32
TOP_K = 4
D_EXPERT = D_MODEL
SWIGLU_LIMIT = 7.0
SWIGLU_ALPHA = 1.702
D_PLE = 256
DEPTH = 1
DN_ALPHA = (2.0 * DEPTH) ** 0.25
LN_EPS = 1e-5
ROW_TILES = D_MODEL // LANES

OFF_B, OFF_C, OFF_H, OFF_Q, OFF_K = (i * D_MODEL for i in range(5))
OFF_V = OFF_K + KV_DIM
OFF_GC = OFF_V + KV_DIM
OFF_GA = OFF_GC + D_MODEL
IN_TOTAL = OFF_GA + D_MODEL

TS_MIX = 512
TRASH_ROWS = TOP_K * TS_MIX
CHUNK = 512
MOE_ROW_VARIANTS = (128, 256, 384, CHUNK)
W_IN_SLAB = 512
TS_FIN = 256
FIN_DEPTH = 3
ISSUE_UNROLL = 4


def _layer_norm(z, g, b):
    mu = jnp.mean(z, axis=-1, keepdims=True)
    zc = z - mu
    var = jnp.mean(zc * zc, axis=-1, keepdims=True)
    return zc * lax.rsqrt(var + LN_EPS) * g + b


def _dot(a, b):
    return jnp.dot(a, b, preferred_element_type=F32)


def _dot_nt(a, b):
    return lax.dot_general(a, b, (((1,), (1,)), ((), ())), preferred_element_type=F32)


def _row_copy(src, src_row, dst, dst_row, rows, sem):
    s0 = pl.multiple_of(src_row * ROW_TILES, ROW_TILES)
    d0 = pl.multiple_of(dst_row * ROW_TILES, ROW_TILES)
    return pltpu.make_async_copy(src.at[pl.ds(s0, rows * ROW_TILES)],
                                 dst.at[pl.ds(d0, rows * ROW_TILES)], sem)


def _mix_kernel(sinks_ref, x_ref, win_hbm, rep_ref, convw_ref, wbrc_ref, wbra_ref, wout_ref,
                g1_ref, b1_ref, whi_ref, wlo_ref, br_ref,
                x1_ref, w_ref, dest_ref, cnt_ref, alloc_ref, xs_hbm,
                kext, vext, uext, upper, lower, carry, cur_id, nfree, ring, dest_v, dest_s,
                pad_v, pad_s, zbuf, win_ref, stage, dsems, rsems, psem, zsem, wsems):
    ts = TS_MIX
    j = pl.program_id(1)
    tile_id = pl.program_id(0) * pl.num_programs(1) + j
    last_tile = w_ref.shape[1] // ts - 1
    trash_base = xs_hbm.shape[0] // ROW_TILES - TRASH_ROWS
    cur = tile_id % 2
    oth = 1 - cur

    def ring_tile(tile):
        return ((tile + 3) % 3) * ts

    def load_ring(tile):
        base = pl.multiple_of(ring_tile(tile) * ROW_TILES, ROW_TILES)
        return jnp.concatenate(
            [ring[pl.ds(base + s, ts, stride=ROW_TILES), :] for s in range(ROW_TILES)], axis=-1)

    def dest_copy(slot):
        return pltpu.make_async_copy(dest_v.at[slot], dest_s.at[slot], dsems.at[slot])

    def row_copy(tile, slot, k, t):
        return _row_copy(ring, ring_tile(tile) + t, xs_hbm, dest_s[slot, k, t], 1, rsems.at[slot])

    def wait_rows(slot):
        for k in range(TOP_K):
            _row_copy(ring, 0, xs_hbm, 0, ts, rsems.at[slot]).wait()

    route = lambda x1_tile, tile, live, slot: _route_tile(
        x1_tile, tile, live, slot, trash_base, whi_ref, wlo_ref, br_ref, upper, lower, carry, cur_id,
        nfree, w_ref, dest_ref, cnt_ref, alloc_ref, dest_v)

    @pl.when(j == 0)
    def _():
        kext[0:WINDOW, :] = jnp.zeros((WINDOW, D_MODEL), BF16)
        vext[0:WINDOW, :] = jnp.zeros((WINDOW, D_MODEL), BF16)
        uext[0:SUBLANES, :] = jnp.zeros((SUBLANES, D_MODEL), F32)

    @pl.when(tile_id == 0)
    def _():
        carry[...] = jnp.zeros_like(carry)
        cur_id[...] = jnp.zeros_like(cur_id)
        nfree[...] = jnp.zeros_like(nfree)
        alloc_ref[...] = jnp.full(alloc_ref.shape, -1, I32)
        ring[...] = jnp.zeros_like(ring)
        zbuf[...] = jnp.zeros_like(zbuf)
        r = lax.broadcasted_iota(I32, (ts, ts), 0)
        c = lax.broadcasted_iota(I32, (ts, ts), 1)
        upper[...] = (r < c).astype(BF16)
        r = lax.broadcasted_iota(I32, (N_EXPERTS, N_EXPERTS), 0)
        c = lax.broadcasted_iota(I32, (N_EXPERTS, N_EXPERTS), 1)
        lower[...] = (c < r).astype(BF16)
        for k in range(TOP_K):
            dest_v[0, k:k + 1, :] = _trash_rows(trash_base, k, ts)
        dest_copy(0).start()

        def slab_copy(s):
            return pltpu.make_async_copy(win_hbm.at[:, pl.ds(s * W_IN_SLAB, W_IN_SLAB)],
                                         stage.at[s % 2], wsems.at[s % 2])

        n_slabs = IN_TOTAL // W_IN_SLAB
        slab_copy(0).start()
        for s in range(n_slabs):
            if s + 1 < n_slabs:
                slab_copy(s + 1).start()
            slab_copy(s).wait()
            win_ref[:, s * W_IN_SLAB:(s + 1) * W_IN_SLAB] = stage[s % 2].astype(BF16)

    @pl.when(tile_id >= 1)
    def _():
        wait_rows(oth)

    route(load_ring(tile_id - 1), jnp.maximum(tile_id - 1, 0), tile_id >= 1, oth)
    dest_copy(oth).start()
    dest_copy(cur).wait()
    for t in range(ts):
        for k in range(TOP_K):
            row_copy(tile_id - 2, cur, k, t).start(priority=k % 2)

    x = x_ref[0]
    xb = x.astype(BF16)

    def proj(off, width):
        return _dot(xb, win_ref[:, off:off + width])

    u = proj(OFF_C, D_MODEL) * proj(OFF_H, D_MODEL)
    uext[SUBLANES:SUBLANES + ts, :] = u
    y = (convw_ref[2:3, :] * u
         + convw_ref[1:2, :] * uext[SUBLANES - 1:SUBLANES - 1 + ts, :]
         + convw_ref[0:1, :] * uext[SUBLANES - 2:SUBLANES - 2 + ts, :])
    uext[0:SUBLANES, :] = u[ts - SUBLANES:ts, :]
    yc_in = proj(OFF_B, D_MODEL) * y
    y_conv = _dot(yc_in.astype(BF16), wbrc_ref[...])
    acc = jax.nn.sigmoid(proj(OFF_GC, D_MODEL)) * y_conv

    q = (proj(OFF_Q, D_MODEL) * (1.0 / math.sqrt(HEAD_DIM))).astype(BF16)
    kb = proj(OFF_K, KV_DIM).astype(BF16)
    vb = proj(OFF_V, KV_DIM).astype(BF16)
    kext[WINDOW:WINDOW + ts, :] = _dot(kb, rep_ref[...]).astype(BF16)
    vext[WINDOW:WINDOW + ts, :] = _dot(vb, rep_ref[...]).astype(BF16)

    grp = GQA_GROUP * HEAD_DIM
    row = lax.broadcasted_iota(I32, (WINDOW, 2 * WINDOW), 0)
    col = lax.broadcasted_iota(I32, (WINDOW, 2 * WINDOW), 1)
    band = (col > row) & (col <= row + WINDOW)
    q_lane_grp = lax.broadcasted_iota(I32, (WINDOW, grp), 1) // HEAD_DIM
    o_blocks = []
    for i in range(ts // WINDOW):
        if i == 0:
            mask = band & ((col >= WINDOW) | (j > 0))
        else:
            mask = band
        o_heads = []
        for h in range(N_KV_HEADS):
            qh = q[i * WINDOW:(i + 1) * WINDOW, h * grp:(h + 1) * grp]
            kh = kext[i * WINDOW:i * WINDOW + 2 * WINDOW, h * grp:(h + 1) * grp]
            vh = vext[i * WINDOW:i * WINDOW + 2 * WINDOW, h * grp:(h + 1) * grp]
            qs = jnp.concatenate(
                [jnp.where(q_lane_grp == g, qh, jnp.zeros_like(qh)) for g in range(GQA_GROUP)], axis=0)
            s_all = _dot_nt(qs, kh)
            ps = []
            for g in range(GQA_GROUP):
                s = jnp.where(mask, s_all[g * WINDOW:(g + 1) * WINDOW], -jnp.inf)
                sink = sinks_ref[h * GQA_GROUP + g]
                m = jnp.maximum(jnp.max(s, axis=-1, keepdims=True), sink)
                e = jnp.exp(s - m)
                den = jnp.sum(e, axis=-1, keepdims=True) + jnp.exp(sink - m)
                ps.append((e / den).astype(BF16))
            pv = _dot(jnp.concatenate(ps, axis=0), vh)
            oh = jnp.zeros((WINDOW, grp), F32)
            for g in range(GQA_GROUP):
                oh = jnp.where(q_lane_grp == g, pv[g * WINDOW:(g + 1) * WINDOW], oh)
            o_heads.append(oh)
        o_blocks.append(jnp.concatenate(o_heads, axis=-1))
    o = jnp.concatenate(o_blocks, axis=0)
    kext[0:WINDOW, :] = kext[ts:ts + WINDOW, :]
    vext[0:WINDOW, :] = vext[ts:ts + WINDOW, :]

    y_attn = _dot(o.astype(BF16), wbra_ref[...])
    acc = acc + jax.nn.sigmoid(proj(OFF_GA, D_MODEL)) * y_attn

    z = DN_ALPHA * x + _dot(acc.astype(BF16), wout_ref[...])
    x1 = _layer_norm(z, g1_ref[...], b1_ref[...])
    x1_ref[0] = x1
    ring_base = pl.multiple_of(ring_tile(tile_id) * ROW_TILES, ROW_TILES)
    for s in range(ROW_TILES):
        ring[pl.ds(ring_base + s, ts, stride=ROW_TILES), :] = x1[:, s * LANES:(s + 1) * LANES]

    @pl.when(tile_id == last_tile)
    def _():
        def move_tile(tile, slot):
            dest_copy(slot).wait()

            def issue(tb, carry_):
                for u in range(ISSUE_UNROLL):
                    for k in range(TOP_K):
                        row_copy(tile, slot, k, tb * ISSUE_UNROLL + u).start(priority=k % 2)
                return carry_

            lax.fori_loop(0, ts // ISSUE_UNROLL, issue, 0)

        wait_rows(cur)
        move_tile(tile_id - 1, oth)
        wait_rows(oth)
        route(load_ring(tile_id), tile_id, True, cur)
        dest_copy(cur).start()
        move_tile(tile_id, cur)
        wait_rows(cur)
        _zero_unassigned_rows(carry, cur_id, nfree, xs_hbm, trash_base, pad_v, pad_s, zbuf, psem, zsem)


def _trash_rows(trash_base, k, ts):
    return trash_base + k * ts + lax.broadcasted_iota(I32, (1, ts), 1)


def _route_tile(x1, tile, live, slot, trash_base, whi_ref, wlo_ref, br_ref, upper, lower, carry, cur_id,
                nfree, w_ref, dest_ref, cnt_ref, alloc_ref, dest_v):
    ts = x1.shape[0]
    shift = CHUNK.bit_length() - 1
    cols = pl.ds(pl.multiple_of(tile * ts, ts), ts)
    xh = x1.astype(BF16)
    xl = (x1 - xh.astype(F32)).astype(BF16)
    whi = whi_ref[...]
    logits = _dot_nt(whi, xh) + _dot_nt(whi, xl) + _dot_nt(wlo_ref[...], xh) + br_ref[...]

    eid = lax.broadcasted_iota(I32, (N_EXPERTS, ts), 0)
    rest = logits
    sels, vals = [], []
    for k in range(TOP_K):
        m = jnp.max(rest, axis=0, keepdims=True)
        idx = jnp.min(jnp.where(rest == m, eid, N_EXPERTS), axis=0, keepdims=True)
        sel = eid == idx
        rest = jnp.where(sel, -jnp.inf, rest)
        sels.append(sel)
        vals.append(m)
    exps = [jnp.exp(v - vals[0]) for v in vals]
    den = exps[0] + exps[1] + exps[2] + exps[3]
    for k in range(TOP_K):
        w_ref[k:k + 1, cols] = exps[k] / den

    member = jnp.zeros((N_EXPERTS, ts), F32)
    for sel in sels:
        member = member + sel.astype(F32)
    before = carry[:, 0:1]
    rank = (_dot(member.astype(BF16), upper[...]) + before).astype(I32)
    tile_cnt = jnp.sum(member, axis=1, keepdims=True)

    before_i = before.astype(I32)
    after_i = before_i + tile_cnt.astype(I32)
    blk0 = before_i >> shift
    any_rows = after_i > before_i
    new0 = any_rows & ((before_i & (CHUNK - 1)) == 0)
    blk1 = (after_i - 1) >> shift
    new1 = any_rows & (blk1 > blk0)
    is_new = new0 | new1
    new_f = jnp.broadcast_to(is_new.astype(F32), (N_EXPERTS, LANES))
    new_id = (nfree[...] + _dot(lower[...], new_f.astype(BF16)))[:, 0:1].astype(I32)
    id0 = jnp.where(new0, new_id, cur_id[:, 0:1])
    row = (jnp.where((rank >> shift) > blk0, new_id, id0) * CHUNK + (rank & (CHUNK - 1))).astype(F32)
    for k in range(TOP_K):
        dk = jnp.sum(jnp.where(sels[k], row, 0.0), axis=0, keepdims=True).astype(I32)
        dk = jnp.where(live, dk, _trash_rows(trash_base, k, ts))
        dest_ref[k:k + 1, cols] = dk
        dest_v[slot, k:k + 1, :] = dk

    took = live & is_new
    lane = lax.broadcasted_iota(I32, alloc_ref.shape, 1)
    alloc_ref[...] = jnp.where(took & (lane == jnp.where(new0, blk0, blk1)), new_id, alloc_ref[...])
    cur_id[...] = jnp.where(took, new_id, cur_id[...])
    nfree[...] = nfree[...] + jnp.where(live, jnp.sum(is_new.astype(F32), axis=0, keepdims=True), 0.0)
    carry[...] = carry[...] + jnp.where(live, tile_cnt, 0.0)
    cnt_ref[...] = carry[...].astype(I32)


def _zero_unassigned_rows(carry, cur_id, nfree, xs_hbm, trash_base, pad_v, pad_s, zbuf, psem, zsem):
    shift = CHUNK.bit_length() - 1
    count = carry[...].astype(I32)
    fill = count - (((count - 1) >> shift) << shift)
    pad = jnp.where(count > 0, CHUNK - fill, 0)
    pos = cur_id[...] * CHUNK + fill
    lane = lax.broadcasted_iota(I32, pad_v.shape, 1)
    pad_v[...] = jnp.where(lane == 0, pos, jnp.where(lane == 1, pad, nfree[...].astype(I32)))
    info = pltpu.make_async_copy(pad_v, pad_s, psem)
    info.start()
    info.wait()
    pieces = [1 << b for b in reversed(range(shift))]
    half = pieces[0]
    for wait in (False, True):
        for e in range(N_EXPERTS):
            npad = pad_s[e, 1]
            for piece in pieces:
                cp = _row_copy(zbuf, 0, xs_hbm, pad_s[e, 0] + (npad & ~(2 * piece - 1)), piece, zsem)

                @pl.when((npad & piece) != 0)
                def _():
                    cp.wait() if wait else cp.start()

        def tail(hc, carry_):
            cp = _row_copy(zbuf, 0, xs_hbm, hc * half, half, zsem)
            cp.wait() if wait else cp.start()
            return carry_

        lax.fori_loop(2 * pad_s[0, 2], trash_base // half, tail, 0)


def _mix(x, w_in_p, rep, conv_w, wbrc, wbra, sinks, wout, g1, b1, whi, wlo, br, p_rows):
    bsz, seq, d = x.shape
    n = bsz * seq
    nj = seq // TS_MIX
    const = lambda shape: pl.BlockSpec(shape, lambda b, j, s: (0,) * len(shape),
                                       pipeline_mode=pl.Buffered(1))
    tok = pl.BlockSpec((TOP_K, n), lambda b, j, s: (0, 0))
    table = pl.BlockSpec((N_EXPERTS, LANES), lambda b, j, s: (0, 0))
    grid_spec = pltpu.PrefetchScalarGridSpec(
        num_scalar_prefetch=1,
        grid=(bsz, nj),
        in_specs=[
            pl.BlockSpec((1, TS_MIX, d), lambda b, j, s: (b, j, 0)),
            pl.BlockSpec(memory_space=pl.ANY),
            const((KV_DIM, d)),
            const((3, d)),
            const((d, d)),
            const((d, d)),
            const((d, d)),
            const((1, d)),
            const((1, d)),
            const((N_EXPERTS, d)),
            const((N_EXPERTS, d)),
            const((N_EXPERTS, 1)),
        ],
        out_specs=[
            pl.BlockSpec((1, TS_MIX, d), lambda b, j, s: (b, j, 0)),
            tok, tok, table, table,
            pl.BlockSpec(memory_space=pl.ANY),
        ],
        scratch_shapes=[
            pltpu.VMEM((WINDOW + TS_MIX, d), BF16),
            pltpu.VMEM((WINDOW + TS_MIX, d), BF16),
            pltpu.VMEM((SUBLANES + TS_MIX, d), F32),
            pltpu.VMEM((TS_MIX, TS_MIX), BF16),
            pltpu.VMEM((N_EXPERTS, N_EXPERTS), BF16),
            pltpu.VMEM((N_EXPERTS, LANES), F32),
            pltpu.VMEM((N_EXPERTS, LANES), I32),
            pltpu.VMEM((N_EXPERTS, LANES), F32),
            pltpu.VMEM((3 * TS_MIX * ROW_TILES, LANES), F32),
            pltpu.VMEM((2, TOP_K, TS_MIX), I32),
            pltpu.SMEM((2, TOP_K, TS_MIX), I32),
            pltpu.VMEM((N_EXPERTS, LANES), I32),
            pltpu.SMEM((N_EXPERTS, LANES), I32),
            pltpu.VMEM((CHUNK // 2 * ROW_TILES, LANES), F32),
            pltpu.VMEM((d, IN_TOTAL), BF16),
            pltpu.VMEM((2, d, W_IN_SLAB), F32),
            pltpu.SemaphoreType.DMA((2,)),
            pltpu.SemaphoreType.DMA((2,)),
            pltpu.SemaphoreType.DMA,
            pltpu.SemaphoreType.DMA,
            pltpu.SemaphoreType.DMA((2,)),
        ],
    )
    return pl.pallas_call(
        _mix_kernel,
        grid_spec=grid_spec,
        out_shape=[jax.ShapeDtypeStruct((bsz, seq, d), F32),
                   jax.ShapeDtypeStruct((TOP_K, n), F32),
                   jax.ShapeDtypeStruct((TOP_K, n), I32),
                   jax.ShapeDtypeStruct((N_EXPERTS, LANES), I32),
                   jax.ShapeDtypeStruct((N_EXPERTS, LANES), I32),
                   jax.ShapeDtypeStruct(((p_rows + TRASH_ROWS) * ROW_TILES, LANES), F32)],
        compiler_params=pltpu.CompilerParams(
            dimension_semantics=("arbitrary", "arbitrary"),
            vmem_limit_bytes=VMEM_LIMIT_BYTES,
            has_side_effects=True),
        name="mix",
    )(sinks, x, w_in_p, rep, conv_w, wbrc, wbra, wout, g1, b1, whi, wlo, br)


def _moe_kernel(ce_ref, nu_ref, slot_ref, nxt_ref, fill_ref, blk_ref, xs_ref, wgu_hbm, bgu_ref, wd_hbm,
                bd_ref, ys_ref, wgu_f32, wd_f32, wgu_bf, wd_bf, wsems):
    c = pl.program_id(0)
    used = c < nu_ref[0]

    def weight_copies(e, slot):
        return (pltpu.make_async_copy(wgu_hbm.at[e], wgu_f32.at[slot], wsems.at[0, slot]),
                pltpu.make_async_copy(wd_hbm.at[e], wd_f32.at[slot], wsems.at[1, slot]))

    @pl.when(used & ((c == 0) | (ce_ref[c] != ce_ref[jnp.maximum(c - 1, 0)])))
    def _():
        slot = slot_ref[c]

        @pl.when(c == 0)
        def _():
            for cp in weight_copies(ce_ref[0], slot):
                cp.start()

        for cp in weight_copies(ce_ref[c], slot):
            cp.wait()

        @pl.when(nxt_ref[c] >= 0)
        def _():
            for cp in weight_copies(nxt_ref[c], 1 - slot):
                cp.start()

        wgu_bf[...] = wgu_f32[slot].astype(BF16)
        wd_bf[...] = wd_f32[slot].astype(BF16)

    def expert(rows):
        x = jnp.concatenate(
            [xs_ref[pl.ds(s, rows, stride=ROW_TILES), :] for s in range(ROW_TILES)], axis=-1)
        gu = _dot(x.astype(BF16), wgu_bf[...]) + bgu_ref[0]
        gate = jnp.minimum(gu[:, :D_EXPERT], SWIGLU_LIMIT)
        up = jnp.clip(gu[:, D_EXPERT:], -SWIGLU_LIMIT, SWIGLU_LIMIT)
        h = (up + 1.0) * gate * jax.nn.sigmoid(SWIGLU_ALPHA * gate)
        y = _dot(h.astype(BF16), wd_bf[...]) + bd_ref[0]
        for s in range(ROW_TILES):
            ys_ref[pl.ds(s, rows, stride=ROW_TILES), :] = y[:, s * LANES:(s + 1) * LANES]
        if rows < CHUNK:
            ys_ref[rows * ROW_TILES:, :] = jnp.zeros(((CHUNK - rows) * ROW_TILES, LANES), F32)

    filled = jnp.where(used, fill_ref[jnp.minimum(c, fill_ref.shape[0] - 1)], 0)
    lower = 0
    for rows in MOE_ROW_VARIANTS:
        @pl.when((filled > lower) & (filled <= rows))
        def _():
            expert(rows)

        lower = rows

    @pl.when(filled == 0)
    def _():
        ys_ref[...] = jnp.zeros_like(ys_ref)


def _moe(chunk_e, n_used, run_slot, next_e, chunk_fill, chunk_blk, xs, wgu, bgu, wd, bd, p_rows):
    n_chunks = p_rows // CHUNK
    d = D_MODEL

    def cc(c, ce, nu, *_):
        return jnp.minimum(c, nu[0] - 1)

    rows_in = pl.BlockSpec((CHUNK * ROW_TILES, LANES), lambda c, *s: (s[5][cc(c, *s)], 0))
    rows_out = pl.BlockSpec((CHUNK * ROW_TILES, LANES), lambda c, *s: (s[5][c], 0))
    bspec = lambda shape: pl.BlockSpec(shape, lambda c, *s: (s[0][cc(c, *s)], 0, 0))
    anyspec = pl.BlockSpec(memory_space=pl.ANY)
    return pl.pallas_call(
        _moe_kernel,
        grid_spec=pltpu.PrefetchScalarGridSpec(
            num_scalar_prefetch=6, grid=(n_chunks,),
            in_specs=[rows_in, anyspec, bspec((1, 1, 2 * D_EXPERT)), anyspec, bspec((1, 1, d))],
            out_specs=rows_out,
            scratch_shapes=[pltpu.VMEM((2, d, 2 * D_EXPERT), F32), pltpu.VMEM((2, D_EXPERT, d), F32),
                            pltpu.VMEM((d, 2 * D_EXPERT), BF16), pltpu.VMEM((D_EXPERT, d), BF16),
                            pltpu.SemaphoreType.DMA((2, 2))]),
        out_shape=jax.ShapeDtypeStruct((p_rows * ROW_TILES, LANES), F32),
        compiler_params=pltpu.CompilerParams(dimension_semantics=("arbitrary",),
                                             vmem_limit_bytes=VMEM_LIMIT_BYTES),
        name="moe",
    )(chunk_e, n_used, run_slot, next_e, chunk_fill, chunk_blk, xs, wgu, bgu, wd, bd)


def _fin_kernel(ys_ref, *refs):
    depth = FIN_DEPTH
    dfirst, dnext = refs[:depth], refs[depth]
    x1_ref, w_ref, p_ref, wpg_ref, wpp_ref, g2_ref, b2_ref, out_ref = refs[depth + 1:depth + 9]
    bufs = refs[depth + 9:2 * depth + 9]
    idx_s, sems, isems = refs[2 * depth + 9:]
    ts = TS_FIN
    i = pl.program_id(0)
    last = pl.num_programs(0) - 1

    def gather_row(m, k, t):
        _row_copy(ys_ref, idx_s[m, k, t], bufs[m].at[k], t, 1, sems.at[m]).start(priority=k % 2)

    def wait_rows(m):
        for k in range(TOP_K):
            _row_copy(ys_ref, 0, bufs[m].at[k], 0, ts, sems.at[m]).wait()

    def index_copy(src_ref, m):
        return pltpu.make_async_copy(src_ref, idx_s.at[m], isems.at[m])

    @pl.when(i == 0)
    def _():
        for m in range(depth):
            index_copy(dfirst[m], m).start()
        for m in range(depth):
            index_copy(dfirst[m], m).wait()
        for m in range(depth - 1):
            def issue(tb, carry, m=m):
                for u in range(ISSUE_UNROLL):
                    for k in range(TOP_K):
                        gather_row(m, k, tb * ISSUE_UNROLL + u)
                return carry

            lax.fori_loop(0, ts // ISSUE_UNROLL, issue, 0)

    for m in range(depth):
        @pl.when(i % depth == m)
        def _():
            ahead = (m + depth - 1) % depth
            wait_rows(m)
            for t in range(ts):
                for k in range(TOP_K):
                    gather_row(ahead, k, t)
            index_copy(dnext, m).start()

            x1 = x1_ref[...]
            ple = (jax.nn.sigmoid(_dot(x1.astype(BF16), wpg_ref[...]))
                   * _dot(p_ref[...].astype(BF16), wpp_ref[...]))
            z = DN_ALPHA * x1 + ple
            w = w_ref[...]
            for k in range(TOP_K):
                yk = jnp.concatenate(
                    [bufs[m][k, pl.ds(s, ts, stride=ROW_TILES), :] for s in range(ROW_TILES)], axis=-1)
                z = z + w[:, k:k + 1] * yk
            out_ref[...] = _layer_norm(z, g2_ref[...], b2_ref[...])
            index_copy(dnext, m).wait()

            @pl.when(i == last)
            def _():
                for d in range(1, depth):
                    wait_rows((m + d) % depth)


def _fin(dest, ys, x1, w_rows, p2, wpg, wpp, g2, b2):
    n, d = x1.shape
    nt = n // TS_FIN
    const = lambda shape: pl.BlockSpec(shape, lambda i: (0,) * len(shape))
    tile = lambda width: pl.BlockSpec((TS_FIN, width), lambda i: (i, 0))
    dtile = lambda tile_of: pl.BlockSpec((TOP_K, TS_FIN), lambda i: (0, tile_of(i)))
    first = [dtile(lambda i, m=m: m) for m in range(FIN_DEPTH)]
    rowbuf = pltpu.VMEM((TOP_K, TS_FIN * ROW_TILES, LANES), F32)
    return pl.pallas_call(
        _fin_kernel,
        grid=(nt,),
        in_specs=[pl.BlockSpec(memory_space=pl.ANY)] + first
                 + [dtile(lambda i: jnp.minimum(i + FIN_DEPTH, nt - 1)),
                    tile(d), tile(TOP_K), tile(D_PLE),
                    const((d, d)), const((D_PLE, d)), const((1, d)), const((1, d))],
        out_specs=tile(d),
        scratch_shapes=[rowbuf] * FIN_DEPTH
                       + [pltpu.SMEM((FIN_DEPTH, TOP_K, TS_FIN), I32),
                          pltpu.SemaphoreType.DMA((FIN_DEPTH,)), pltpu.SemaphoreType.DMA((FIN_DEPTH,))],
        out_shape=jax.ShapeDtypeStruct((n, d), F32),
        compiler_params=pltpu.CompilerParams(dimension_semantics=("arbitrary",),
                                             vmem_limit_bytes=VMEM_LIMIT_BYTES),
        name="fin",
    )(ys, *([dest] * (FIN_DEPTH + 1)), x1, w_rows, p2, wpg, wpp, g2, b2)


def _replication_matrix():
    src = jnp.arange(KV_DIM)[:, None]
    dst = jnp.arange(D_MODEL)[None, :]
    same_head = (dst // (GQA_GROUP * HEAD_DIM)) == (src // HEAD_DIM)
    same_dim = (dst % HEAD_DIM) == (src % HEAD_DIM)
    return (same_head & same_dim).astype(BF16)


def kernel(x, p, w_in, conv_w, w_br_conv, w_br_attn, attn_sinks, w_out, ln1_g, ln1_b, w_router,
           b_router, w_gu, b_gu, w_down, b_down, w_ple_proj, w_ple_gate, ln2_g, ln2_b):
    bsz, seq, d = x.shape
    n = bsz * seq
    for i in range(DEPTH):
        wr_t = w_router[i].T
        wr_hi = wr_t.astype(BF16)
        wr_lo = (wr_t - wr_hi.astype(F32)).astype(BF16)
        p_rows = n * TOP_K + N_EXPERTS * CHUNK
        x1, w_top, dest, cnt, alloc, xs = _mix(
            x, w_in[i], _replication_matrix(), conv_w[i],
            w_br_conv[i].astype(BF16), w_br_attn[i].astype(BF16), attn_sinks[i],
            w_out[i].astype(BF16), ln1_g[i][None, :], ln1_b[i][None, :],
            wr_hi, wr_lo, b_router[i][:, None], p_rows)
        x1 = x1.reshape(n, d)

        counts = cnt[:, 0]
        padded = ((counts + CHUNK - 1) // CHUNK) * CHUNK
        end_padded = jnp.cumsum(padded)
        start_padded = end_padded - padded
        n_chunks = p_rows // CHUNK
        chunk_pos = jnp.arange(n_chunks, dtype=I32)
        chunk_start = chunk_pos * CHUNK
        chunk_e = jnp.minimum(
            jnp.sum((end_padded[None, :] <= chunk_start[:, None]).astype(I32), axis=1),
            N_EXPERTS - 1)
        n_used = (end_padded[-1:] // CHUNK).astype(I32)
        eid = jnp.arange(N_EXPERTS, dtype=I32)
        onehot = (chunk_e[:, None] == eid[None, :]).astype(I32)
        chunk_no = chunk_pos - jnp.sum(onehot * (start_padded // CHUNK)[None, :], axis=1)
        in_table = onehot[:, :, None] * (chunk_no[:, None, None] == jnp.arange(LANES, dtype=I32)[None, None, :])
        chunk_blk = jnp.where(chunk_pos < n_used[0], jnp.sum(in_table * alloc[None, :, :], axis=(1, 2)),
                              chunk_pos).astype(I32)
        active = padded > 0
        run_idx = jnp.cumsum(active.astype(I32)) - 1
        later = active[None, :] & (eid[None, :] > eid[:, None])
        nxt = jnp.min(jnp.where(later, eid[None, :], N_EXPERTS), axis=1)
        nxt = jnp.where(nxt == N_EXPERTS, -1, nxt)
        run_slot = jnp.sum(onehot * (run_idx & 1)[None, :], axis=1).astype(I32)
        next_e = jnp.sum(onehot * nxt[None, :], axis=1).astype(I32)
        chunk_fill = jnp.clip(
            jnp.sum(onehot * (counts + start_padded)[None, :], axis=1) - chunk_start, 0, CHUNK).astype(I32)
        ys = _moe(chunk_e, n_used, run_slot, next_e, chunk_fill, chunk_blk, xs, w_gu[i],
                  b_gu[i][:, None, :], w_down[i], b_down[i][:, None, :], p_rows)
        out = _fin(dest, ys, x1, w_top.T, p[i].reshape(n, D_PLE),
                   w_ple_gate[i].astype(BF16), w_ple_proj[i].astype(BF16),
                   ln2_g[i][None, :], ln2_b[i][None, :])
        x = out.reshape(bsz, seq, d)
    return x
```

```python
import math

import jax
import jax.numpy as jnp
from jax import lax
from jax.experimental import pallas as pl
from jax.experimental.pallas import tpu as pltpu

F32 = jnp.float32
BF16 = jnp.bfloat16
I32 = jnp.int32

SUBLANES = 8
LANES = 128
VMEM_LIMIT_BYTES = 56 * 1024 * 1024

D_MODEL = 1024
HEAD_DIM = 64
N_Q_HEADS = 16
N_KV_HEADS = 4
GQA_GROUP = N_Q_HEADS // N_KV_HEADS
KV_DIM = N_KV_HEADS * HEAD_DIM
WINDOW = 128
N_EXPERTS = 32
TOP_K = 4
D_EXPERT = D_MODEL
SWIGLU_LIMIT = 7.0
SWIGLU_ALPHA = 1.702
D_PLE = 256
DEPTH = 1
DN_ALPHA = (2.0 * DEPTH) ** 0.25
LN_EPS = 1e-5
ROW_TILES = D_MODEL // LANES

OFF_B, OFF_C, OFF_H, OFF_Q, OFF_K = (i * D_MODEL for i in range(5))
OFF_V = OFF_K + KV_DIM
OFF_GC = OFF_V + KV_DIM
OFF_GA = OFF_GC + D_MODEL
IN_TOTAL = OFF_GA + D_MODEL

TS_MIX = 512
TRASH_ROWS = TOP_K * TS_MIX
CHUNK = 512
MOE_ROW_VARIANTS = (128, 256, 384, CHUNK)
W_IN_SLAB = 512
TS_FIN = 256
FIN_DEPTH = 4
ISSUE_UNROLL = 4


def _layer_norm(z, g, b):
    mu = jnp.mean(z, axis=-1, keepdims=True)
    zc = z - mu
    var = jnp.mean(zc * zc, axis=-1, keepdims=True)
    return zc * lax.rsqrt(var + LN_EPS) * g + b


def _dot(a, b):
    return jnp.dot(a, b, preferred_element_type=F32)


def _dot_nt(a, b):
    return lax.dot_general(a, b, (((1,), (1,)), ((), ())), preferred_element_type=F32)


def _row_copy(src, src_row, dst, dst_row, rows, sem):
    s0 = pl.multiple_of(src_row * ROW_TILES, ROW_TILES)
    d0 = pl.multiple_of(dst_row * ROW_TILES, ROW_TILES)
    return pltpu.make_async_copy(src.at[pl.ds(s0, rows * ROW_TILES)],
                                 dst.at[pl.ds(d0, rows * ROW_TILES)], sem)


def _mix_kernel(sinks_ref, x_ref, win_hbm, rep_ref, convw_ref, wbrc_ref, wbra_ref, wout_ref,
                g1_ref, b1_ref, whi_ref, wlo_ref, br_ref,
                x1_ref, w_ref, dest_ref, cnt_ref, alloc_ref, xs_hbm,
                kext, vext, uext, upper, lower, carry, cur_id, nfree, ring, dest_v, dest_s,
                pad_v, pad_s, zbuf, win_ref, stage, dsems, rsems, psem, zsem, wsems):
    ts = TS_MIX
    j = pl.program_id(1)
    tile_id = pl.program_id(0) * pl.num_programs(1) + j
    last_tile = w_ref.shape[1] // ts - 1
    trash_base = xs_hbm.shape[0] // ROW_TILES - TRASH_ROWS
    cur = tile_id % 2
    oth = 1 - cur

    def ring_tile(tile):
        return ((tile + 3) % 3) * ts

    def load_ring(tile):
        base = pl.multiple_of(ring_tile(tile) * ROW_TILES, ROW_TILES)
        return jnp.concatenate(
            [ring[pl.ds(base + s, ts, stride=ROW_TILES), :] for s in range(ROW_TILES)], axis=-1)

    def dest_copy(slot):
        return pltpu.make_async_copy(dest_v.at[slot], dest_s.at[slot], dsems.at[slot])

    def row_copy(tile, slot, k, t):
        return _row_copy(ring, ring_tile(tile) + t, xs_hbm, dest_s[slot, k, t], 1, rsems.at[slot])

    def wait_rows(slot):
        for k in range(TOP_K):
            _row_copy(ring, 0, xs_hbm, 0, ts, rsems.at[slot]).wait()

    route = lambda x1_tile, tile, live, slot: _route_tile(
        x1_tile, tile, live, slot, trash_base, whi_ref, wlo_ref, br_ref, upper, lower, carry, cur_id,
        nfree, w_ref, dest_ref, cnt_ref, alloc_ref, dest_v)

    @pl.when(j == 0)
    def _():
        kext[0:WINDOW, :] = jnp.zeros((WINDOW, D_MODEL), BF16)
        vext[0:WINDOW, :] = jnp.zeros((WINDOW, D_MODEL), BF16)
        uext[0:SUBLANES, :] = jnp.zeros((SUBLANES, D_MODEL), F32)

    @pl.when(tile_id == 0)
    def _():
        carry[...] = jnp.zeros_like(carry)
        cur_id[...] = jnp.zeros_like(cur_id)
        nfree[...] = jnp.zeros_like(nfree)
        alloc_ref[...] = jnp.full(alloc_ref.shape, -1, I32)
        ring[...] = jnp.zeros_like(ring)
        zbuf[...] = jnp.zeros_like(zbuf)
        r = lax.broadcasted_iota(I32, (ts, ts), 0)
        c = lax.broadcasted_iota(I32, (ts, ts), 1)
        upper[...] = (r < c).astype(BF16)
        r = lax.broadcasted_iota(I32, (N_EXPERTS, N_EXPERTS), 0)
        c = lax.broadcasted_iota(I32, (N_EXPERTS, N_EXPERTS), 1)
        lower[...] = (c < r).astype(BF16)
        for k in range(TOP_K):
            dest_v[0, k:k + 1, :] = _trash_rows(trash_base, k, ts)
        dest_copy(0).start()

        def slab_copy(s):
            return pltpu.make_async_copy(win_hbm.at[:, pl.ds(s * W_IN_SLAB, W_IN_SLAB)],
                                         stage.at[s % 2], wsems.at[s % 2])

        n_slabs = IN_TOTAL // W_IN_SLAB
        slab_copy(0).start()
        for s in range(n_slabs):
            if s + 1 < n_slabs:
                slab_copy(s + 1).start()
            slab_copy(s).wait()
            win_ref[:, s * W_IN_SLAB:(s + 1) * W_IN_SLAB] = stage[s % 2].astype(BF16)

    @pl.when(tile_id >= 1)
    def _():
        wait_rows(oth)

    route(load_ring(tile_id - 1), jnp.maximum(tile_id - 1, 0), tile_id >= 1, oth)
    dest_copy(oth).start()
    dest_copy(cur).wait()
    for t in range(ts):
        for k in range(TOP_K):
            row_copy(tile_id - 2, cur, k, t).start(priority=k % 2)

    x = x_ref[0]
    xb = x.astype(BF16)

    def proj(off, width):
        return _dot(xb, win_ref[:, off:off + width])

    u = proj(OFF_C, D_MODEL) * proj(OFF_H, D_MODEL)
    uext[SUBLANES:SUBLANES + ts, :] = u
    y = (convw_ref[2:3, :] * u
         + convw_ref[1:2, :] * uext[SUBLANES - 1:SUBLANES - 1 + ts, :]
         + convw_ref[0:1, :] * uext[SUBLANES - 2:SUBLANES - 2 + ts, :])
    uext[0:SUBLANES, :] = u[ts - SUBLANES:ts, :]
    yc_in = proj(OFF_B, D_MODEL) * y
    y_conv = _dot(yc_in.astype(BF16), wbrc_ref[...])
    acc = jax.nn.sigmoid(proj(OFF_GC, D_MODEL)) * y_conv

    q = (proj(OFF_Q, D_MODEL) * (1.0 / math.sqrt(HEAD_DIM))).astype(BF16)
    kb = proj(OFF_K, KV_DIM).astype(BF16)
    vb = proj(OFF_V, KV_DIM).astype(BF16)
    kext[WINDOW:WINDOW + ts, :] = _dot(kb, rep_ref[...]).astype(BF16)
    vext[WINDOW:WINDOW + ts, :] = _dot(vb, rep_ref[...]).astype(BF16)

    grp = GQA_GROUP * HEAD_DIM
    row = lax.broadcasted_iota(I32, (WINDOW, 2 * WINDOW), 0)
    col = lax.broadcasted_iota(I32, (WINDOW, 2 * WINDOW), 1)
    band = (col > row) & (col <= row + WINDOW)
    q_lane_grp = lax.broadcasted_iota(I32, (WINDOW, grp), 1) // HEAD_DIM
    o_blocks = []
    for i in range(ts // WINDOW):
        if i == 0:
            mask = band & ((col >= WINDOW) | (j > 0))
        else:
            mask = band
        o_heads = []
        for h in range(N_KV_HEADS):
            qh = q[i * WINDOW:(i + 1) * WINDOW, h * grp:(h + 1) * grp]
            kh = kext[i * WINDOW:i * WINDOW + 2 * WINDOW, h * grp:(h + 1) * grp]
            vh = vext[i * WINDOW:i * WINDOW + 2 * WINDOW, h * grp:(h + 1) * grp]
            qs = jnp.concatenate(
                [jnp.where(q_lane_grp == g, qh, jnp.zeros_like(qh)) for g in range(GQA_GROUP)], axis=0)
            s_all = _dot_nt(qs, kh)
            ps = []
            for g in range(GQA_GROUP):
                s = jnp.where(mask, s_all[g * WINDOW:(g + 1) * WINDOW], -jnp.inf)
                sink = sinks_ref[h * GQA_GROUP + g]
                m = jnp.maximum(jnp.max(s, axis=-1, keepdims=True), sink)
                e = jnp.exp(s - m)
                den = jnp.sum(e, axis=-1, keepdims=True) + jnp.exp(sink - m)
                ps.append((e / den).astype(BF16))
            pv = _dot(jnp.concatenate(ps, axis=0), vh)
            oh = jnp.zeros((WINDOW, grp), F32)
            for g in range(GQA_GROUP):
                oh = jnp.where(q_lane_grp == g, pv[g * WINDOW:(g + 1) * WINDOW], oh)
            o_heads.append(oh)
        o_blocks.append(jnp.concatenate(o_heads, axis=-1))
    o = jnp.concatenate(o_blocks, axis=0)
    kext[0:WINDOW, :] = kext[ts:ts + WINDOW, :]
    vext[0:WINDOW, :] = vext[ts:ts + WINDOW, :]

    y_attn = _dot(o.astype(BF16), wbra_ref[...])
    acc = acc + jax.nn.sigmoid(proj(OFF_GA, D_MODEL)) * y_attn

    z = DN_ALPHA * x + _dot(acc.astype(BF16), wout_ref[...])
    x1 = _layer_norm(z, g1_ref[...], b1_ref[...])
    x1_ref[0] = x1
    ring_base = pl.multiple_of(ring_tile(tile_id) * ROW_TILES, ROW_TILES)
    for s in range(ROW_TILES):
        ring[pl.ds(ring_base + s, ts, stride=ROW_TILES), :] = x1[:, s * LANES:(s + 1) * LANES]

    @pl.when(tile_id == last_tile)
    def _():
        def move_tile(tile, slot):
            dest_copy(slot).wait()

            def issue(tb, carry_):
                for u in range(ISSUE_UNROLL):
                    for k in range(TOP_K):
                        row_copy(tile, slot, k, tb * ISSUE_UNROLL + u).start(priority=k % 2)
                return carry_

            lax.fori_loop(0, ts // ISSUE_UNROLL, issue, 0)

        wait_rows(cur)
        move_tile(tile_id - 1, oth)
        wait_rows(oth)
        route(load_ring(tile_id), tile_id, True, cur)
        dest_copy(cur).start()
        move_tile(tile_id, cur)
        wait_rows(cur)
        _zero_unassigned_rows(carry, cur_id, nfree, xs_hbm, trash_base, pad_v, pad_s, zbuf, psem, zsem)


def _trash_rows(trash_base, k, ts):
    return trash_base + k * ts + lax.broadcasted_iota(I32, (1, ts), 1)


def _route_tile(x1, tile, live, slot, trash_base, whi_ref, wlo_ref, br_ref, upper, lower, carry, cur_id,
                nfree, w_ref, dest_ref, cnt_ref, alloc_ref, dest_v):
    ts = x1.shape[0]
    shift = CHUNK.bit_length() - 1
    cols = pl.ds(pl.multiple_of(tile * ts, ts), ts)
    xh = x1.astype(BF16)
    xl = (x1 - xh.astype(F32)).astype(BF16)
    whi = whi_ref[...]
    logits = _dot_nt(whi, xh) + _dot_nt(whi, xl) + _dot_nt(wlo_ref[...], xh) + br_ref[...]

    eid = lax.broadcasted_iota(I32, (N_EXPERTS, ts), 0)
    rest = logits
    sels, vals = [], []
    for k in range(TOP_K):
        m = jnp.max(rest, axis=0, keepdims=True)
        idx = jnp.min(jnp.where(rest == m, eid, N_EXPERTS), axis=0, keepdims=True)
        sel = eid == idx
        rest = jnp.where(sel, -jnp.inf, rest)
        sels.append(sel)
        vals.append(m)
    exps = [jnp.exp(v - vals[0]) for v in vals]
    den = exps[0] + exps[1] + exps[2] + exps[3]
    for k in range(TOP_K):
        w_ref[k:k + 1, cols] = exps[k] / den

    member = jnp.zeros((N_EXPERTS, ts), F32)
    for sel in sels:
        member = member + sel.astype(F32)
    before = carry[:, 0:1]
    rank = (_dot(member.astype(BF16), upper[...]) + before).astype(I32)
    tile_cnt = jnp.sum(member, axis=1, keepdims=True)

    before_i = before.astype(I32)
    after_i = before_i + tile_cnt.astype(I32)
    blk0 = before_i >> shift
    any_rows = after_i > before_i
    new0 = any_rows & ((before_i & (CHUNK - 1)) == 0)
    blk1 = (after_i - 1) >> shift
    new1 = any_rows & (blk1 > blk0)
    is_new = new0 | new1
    new_f = jnp.broadcast_to(is_new.astype(F32), (N_EXPERTS, LANES))
    new_id = (nfree[...] + _dot(lower[...], new_f.astype(BF16)))[:, 0:1].astype(I32)
    id0 = jnp.where(new0, new_id, cur_id[:, 0:1])
    row = (jnp.where((rank >> shift) > blk0, new_id, id0) * CHUNK + (rank & (CHUNK - 1))).astype(F32)
    for k in range(TOP_K):
        dk = jnp.sum(jnp.where(sels[k], row, 0.0), axis=0, keepdims=True).astype(I32)
        dk = jnp.where(live, dk, _trash_rows(trash_base, k, ts))
        dest_ref[k:k + 1, cols] = dk
        dest_v[slot, k:k + 1, :] = dk

    took = live & is_new
    lane = lax.broadcasted_iota(I32, alloc_ref.shape, 1)
    alloc_ref[...] = jnp.where(took & (lane == jnp.where(new0, blk0, blk1)), new_id, alloc_ref[...])
    cur_id[...] = jnp.where(took, new_id, cur_id[...])
    nfree[...] = nfree[...] + jnp.where(live, jnp.sum(is_new.astype(F32), axis=0, keepdims=True), 0.0)
    carry[...] = carry[...] + jnp.where(live, tile_cnt, 0.0)
    cnt_ref[...] = carry[...].astype(I32)


def _zero_unassigned_rows(carry, cur_id, nfree, xs_hbm, trash_base, pad_v, pad_s, zbuf, psem, zsem):
    shift = CHUNK.bit_length() - 1
    count = carry[...].astype(I32)
    fill = count - (((count - 1) >> shift) << shift)
    pad = jnp.where(count > 0, CHUNK - fill, 0)
    pos = cur_id[...] * CHUNK + fill
    lane = lax.broadcasted_iota(I32, pad_v.shape, 1)
    pad_v[...] = jnp.where(lane == 0, pos, jnp.where(lane == 1, pad, nfree[...].astype(I32)))
    info = pltpu.make_async_copy(pad_v, pad_s, psem)
    info.start()
    info.wait()
    pieces = [1 << b for b in reversed(range(shift))]
    half = pieces[0]
    for wait in (False, True):
        for e in range(N_EXPERTS):
            npad = pad_s[e, 1]
            for piece in pieces:
                cp = _row_copy(zbuf, 0, xs_hbm, pad_s[e, 0] + (npad & ~(2 * piece - 1)), piece, zsem)

                @pl.when((npad & piece) != 0)
                def _():
                    cp.wait() if wait else cp.start()

        def tail(hc, carry_):
            cp = _row_copy(zbuf, 0, xs_hbm, hc * half, half, zsem)
            cp.wait() if wait else cp.start()
            return carry_

        lax.fori_loop(2 * pad_s[0, 2], trash_base // half, tail, 0)


def _mix(x, w_in_p, rep, conv_w, wbrc, wbra, sinks, wout, g1, b1, whi, wlo, br, p_rows):
    bsz, seq, d = x.shape
    n = bsz * seq
    nj = seq // TS_MIX
    const = lambda shape: pl.BlockSpec(shape, lambda b, j, s: (0,) * len(shape),
                                       pipeline_mode=pl.Buffered(1))
    tok = pl.BlockSpec((TOP_K, n), lambda b, j, s: (0, 0))
    table = pl.BlockSpec((N_EXPERTS, LANES), lambda b, j, s: (0, 0))
    grid_spec = pltpu.PrefetchScalarGridSpec(
        num_scalar_prefetch=1,
        grid=(bsz, nj),
        in_specs=[
            pl.BlockSpec((1, TS_MIX, d), lambda b, j, s: (b, j, 0)),
            pl.BlockSpec(memory_space=pl.ANY),
            const((KV_DIM, d)),
            const((3, d)),
            const((d, d)),
            const((d, d)),
            const((d, d)),
            const((1, d)),
            const((1, d)),
            const((N_EXPERTS, d)),
            const((N_EXPERTS, d)),
            const((N_EXPERTS, 1)),
        ],
        out_specs=[
            pl.BlockSpec((1, TS_MIX, d), lambda b, j, s: (b, j, 0)),
            tok, tok, table, table,
            pl.BlockSpec(memory_space=pl.ANY),
        ],
        scratch_shapes=[
            pltpu.VMEM((WINDOW + TS_MIX, d), BF16),
            pltpu.VMEM((WINDOW + TS_MIX, d), BF16),
            pltpu.VMEM((SUBLANES + TS_MIX, d), F32),
            pltpu.VMEM((TS_MIX, TS_MIX), BF16),
            pltpu.VMEM((N_EXPERTS, N_EXPERTS), BF16),
            pltpu.VMEM((N_EXPERTS, LANES), F32),
            pltpu.VMEM((N_EXPERTS, LANES), I32),
            pltpu.VMEM((N_EXPERTS, LANES), F32),
            pltpu.VMEM((3 * TS_MIX * ROW_TILES, LANES), F32),
            pltpu.VMEM((2, TOP_K, TS_MIX), I32),
            pltpu.SMEM((2, TOP_K, TS_MIX), I32),
            pltpu.VMEM((N_EXPERTS, LANES), I32),
            pltpu.SMEM((N_EXPERTS, LANES), I32),
            pltpu.VMEM((CHUNK // 2 * ROW_TILES, LANES), F32),
            pltpu.VMEM((d, IN_TOTAL), BF16),
            pltpu.VMEM((2, d, W_IN_SLAB), F32),
            pltpu.SemaphoreType.DMA((2,)),
            pltpu.SemaphoreType.DMA((2,)),
            pltpu.SemaphoreType.DMA,
            pltpu.SemaphoreType.DMA,
            pltpu.SemaphoreType.DMA((2,)),
        ],
    )
    return pl.pallas_call(
        _mix_kernel,
        grid_spec=grid_spec,
        out_shape=[jax.ShapeDtypeStruct((bsz, seq, d), F32),
                   jax.ShapeDtypeStruct((TOP_K, n), F32),
                   jax.ShapeDtypeStruct((TOP_K, n), I32),
                   jax.ShapeDtypeStruct((N_EXPERTS, LANES), I32),
                   jax.ShapeDtypeStruct((N_EXPERTS, LANES), I32),
                   jax.ShapeDtypeStruct(((p_rows + TRASH_ROWS) * ROW_TILES, LANES), F32)],
        compiler_params=pltpu.CompilerParams(
            dimension_semantics=("arbitrary", "arbitrary"),
            vmem_limit_bytes=VMEM_LIMIT_BYTES,
            has_side_effects=True),
        name="mix",
    )(sinks, x, w_in_p, rep, conv_w, wbrc, wbra, wout, g1, b1, whi, wlo, br)


def _moe_kernel(ce_ref, nu_ref, slot_ref, nxt_ref, fill_ref, blk_ref, xs_ref, wgu_hbm, bgu_ref, wd_hbm,
                bd_ref, ys_ref, wgu_f32, wd_f32, wgu_bf, wd_bf, wsems):
    c = pl.program_id(0)
    used = c < nu_ref[0]

    def weight_copies(e, slot):
        return (pltpu.make_async_copy(wgu_hbm.at[e], wgu_f32.at[slot], wsems.at[0, slot]),
                pltpu.make_async_copy(wd_hbm.at[e], wd_f32.at[slot], wsems.at[1, slot]))

    @pl.when(used & ((c == 0) | (ce_ref[c] != ce_ref[jnp.maximum(c - 1, 0)])))
    def _():
        slot = slot_ref[c]

        @pl.when(c == 0)
        def _():
            for cp in weight_copies(ce_ref[0], slot):
                cp.start()

        for cp in weight_copies(ce_ref[c], slot):
            cp.wait()

        @pl.when(nxt_ref[c] >= 0)
        def _():
            for cp in weight_copies(nxt_ref[c], 1 - slot):
                cp.start()

        wgu_bf[...] = wgu_f32[slot].astype(BF16)
        wd_bf[...] = wd_f32[slot].astype(BF16)

    def expert(rows):
        x = jnp.concatenate(
            [xs_ref[pl.ds(s, rows, stride=ROW_TILES), :] for s in range(ROW_TILES)], axis=-1)
        gu = _dot(x.astype(BF16), wgu_bf[...]) + bgu_ref[0]
        gate = jnp.minimum(gu[:, :D_EXPERT], SWIGLU_LIMIT)
        up = jnp.clip(gu[:, D_EXPERT:], -SWIGLU_LIMIT, SWIGLU_LIMIT)
        h = (up + 1.0) * gate * jax.nn.sigmoid(SWIGLU_ALPHA * gate)
        y = _dot(h.astype(BF16), wd_bf[...]) + bd_ref[0]
        for s in range(ROW_TILES):
            ys_ref[pl.ds(s, rows, stride=ROW_TILES), :] = y[:, s * LANES:(s + 1) * LANES]
        if rows < CHUNK:
            ys_ref[rows * ROW_TILES:, :] = jnp.zeros(((CHUNK - rows) * ROW_TILES, LANES), F32)

    filled = jnp.where(used, fill_ref[jnp.minimum(c, fill_ref.shape[0] - 1)], 0)
    lower = 0
    for rows in MOE_ROW_VARIANTS:
        @pl.when((filled > lower) & (filled <= rows))
        def _():
            expert(rows)

        lower = rows

    @pl.when(filled == 0)
    def _():
        ys_ref[...] = jnp.zeros_like(ys_ref)


def _moe(chunk_e, n_used, run_slot, next_e, chunk_fill, chunk_blk, xs, wgu, bgu, wd, bd, p_rows):
    n_chunks = p_rows // CHUNK
    d = D_MODEL

    def cc(c, ce, nu, *_):
        return jnp.minimum(c, nu[0] - 1)

    rows_in = pl.BlockSpec((CHUNK * ROW_TILES, LANES), lambda c, *s: (s[5][cc(c, *s)], 0))
    rows_out = pl.BlockSpec((CHUNK * ROW_TILES, LANES), lambda c, *s: (s[5][c], 0))
    bspec = lambda shape: pl.BlockSpec(shape, lambda c, *s: (s[0][cc(c, *s)], 0, 0))
    anyspec = pl.BlockSpec(memory_space=pl.ANY)
    return pl.pallas_call(
        _moe_kernel,
        grid_spec=pltpu.PrefetchScalarGridSpec(
            num_scalar_prefetch=6, grid=(n_chunks,),
            in_specs=[rows_in, anyspec, bspec((1, 1, 2 * D_EXPERT)), anyspec, bspec((1, 1, d))],
            out_specs=rows_out,
            scratch_shapes=[pltpu.VMEM((2, d, 2 * D_EXPERT), F32), pltpu.VMEM((2, D_EXPERT, d), F32),
                            pltpu.VMEM((d, 2 * D_EXPERT), BF16), pltpu.VMEM((D_EXPERT, d), BF16),
                            pltpu.SemaphoreType.DMA((2, 2))]),
        out_shape=jax.ShapeDtypeStruct((p_rows * ROW_TILES, LANES), F32),
        compiler_params=pltpu.CompilerParams(dimension_semantics=("arbitrary",),
                                             vmem_limit_bytes=VMEM_LIMIT_BYTES),
        name="moe",
    )(chunk_e, n_used, run_slot, next_e, chunk_fill, chunk_blk, xs, wgu, bgu, wd, bd)


def _fin_kernel(ys_ref, *refs):
    depth = FIN_DEPTH
    dfirst, dnext = refs[:depth], refs[depth]
    x1_ref, w_ref, p_ref, wpg_ref, wpp_ref, g2_ref, b2_ref, out_ref = refs[depth + 1:depth + 9]
    bufs = refs[depth + 9:2 * depth + 9]
    idx_s, sems, isems = refs[2 * depth + 9:]
    ts = TS_FIN
    i = pl.program_id(0)
    last = pl.num_programs(0) - 1

    def gather_row(m, k, t):
        _row_copy(ys_ref, idx_s[m, k, t], bufs[m].at[k], t, 1, sems.at[m]).start(priority=k % 2)

    def wait_rows(m):
        for k in range(TOP_K):
            _row_copy(ys_ref, 0, bufs[m].at[k], 0, ts, sems.at[m]).wait()

    def index_copy(src_ref, m):
        return pltpu.make_async_copy(src_ref, idx_s.at[m], isems.at[m])

    @pl.when(i == 0)
    def _():
        for m in range(depth):
            index_copy(dfirst[m], m).start()
        for m in range(depth):
            index_copy(dfirst[m], m).wait()
        for m in range(depth - 1):
            def issue(tb, carry, m=m):
                for u in range(ISSUE_UNROLL):
                    for k in range(TOP_K):
                        gather_row(m, k, tb * ISSUE_UNROLL + u)
                return carry

            lax.fori_loop(0, ts // ISSUE_UNROLL, issue, 0)

    for m in range(depth):
        @pl.when(i % depth == m)
        def _():
            ahead = (m + depth - 1) % depth
            wait_rows(m)
            for t in range(ts):
                for k in range(TOP_K):
                    gather_row(ahead, k, t)
            index_copy(dnext, m).start()

            x1 = x1_ref[...]
            ple = (jax.nn.sigmoid(_dot(x1.astype(BF16), wpg_ref[...]))
                   * _dot(p_ref[...].astype(BF16), wpp_ref[...]))
            z = DN_ALPHA * x1 + ple
            w = w_ref[...]
            for k in range(TOP_K):
                yk = jnp.concatenate(
                    [bufs[m][k, pl.ds(s, ts, stride=ROW_TILES), :] for s in range(ROW_TILES)], axis=-1)
                z = z + w[:, k:k + 1] * yk
            out_ref[...] = _layer_norm(z, g2_ref[...], b2_ref[...])
            index_copy(dnext, m).wait()

            @pl.when(i == last)
            def _():
                for d in range(1, depth):
                    wait_rows((m + d) % depth)


def _fin(dest, ys, x1, w_rows, p2, wpg, wpp, g2, b2):
    n, d = x1.shape
    nt = n // TS_FIN
    const = lambda shape: pl.BlockSpec(shape, lambda i: (0,) * len(shape))
    tile = lambda width: pl.BlockSpec((TS_FIN, width), lambda i: (i, 0))
    dtile = lambda tile_of: pl.BlockSpec((TOP_K, TS_FIN), lambda i: (0, tile_of(i)))
    first = [dtile(lambda i, m=m: m) for m in range(FIN_DEPTH)]
    rowbuf = pltpu.VMEM((TOP_K, TS_FIN * ROW_TILES, LANES), F32)
    return pl.pallas_call(
        _fin_kernel,
        grid=(nt,),
        in_specs=[pl.BlockSpec(memory_space=pl.ANY)] + first
                 + [dtile(lambda i: jnp.minimum(i + FIN_DEPTH, nt - 1)),
                    tile(d), tile(TOP_K), tile(D_PLE),
                    const((d, d)), const((D_PLE, d)), const((1, d)), const((1, d))],
        out_specs=tile(d),
        scratch_shapes=[rowbuf] * FIN_DEPTH
                       + [pltpu.SMEM((FIN_DEPTH, TOP_K, TS_FIN), I32),
                          pltpu.SemaphoreType.DMA((FIN_DEPTH,)), pltpu.SemaphoreType.DMA((FIN_DEPTH,))],
        out_shape=jax.ShapeDtypeStruct((n, d), F32),
        compiler_params=pltpu.CompilerParams(dimension_semantics=("arbitrary",),
                                             vmem_limit_bytes=VMEM_LIMIT_BYTES),
        name="fin",
    )(ys, *([dest] * (FIN_DEPTH + 1)), x1, w_rows, p2, wpg, wpp, g2, b2)


def _replication_matrix():
    src = jnp.arange(KV_DIM)[:, None]
    dst = jnp.arange(D_MODEL)[None, :]
    same_head = (dst // (GQA_GROUP * HEAD_DIM)) == (src // HEAD_DIM)
    same_dim = (dst % HEAD_DIM) == (src % HEAD_DIM)
    return (same_head & same_dim).astype(BF16)


def kernel(x, p, w_in, conv_w, w_br_conv, w_br_attn, attn_sinks, w_out, ln1_g, ln1_b, w_router,
           b_router, w_gu, b_gu, w_down, b_down, w_ple_proj, w_ple_gate, ln2_g, ln2_b):
    bsz, seq, d = x.shape
    n = bsz * seq
    for i in range(DEPTH):
        wr_t = w_router[i].T
        wr_hi = wr_t.astype(BF16)
        wr_lo = (wr_t - wr_hi.astype(F32)).astype(BF16)
        p_rows = n * TOP_K + N_EXPERTS * CHUNK
        x1, w_top, dest, cnt, alloc, xs = _mix(
            x, w_in[i], _replication_matrix(), conv_w[i],
            w_br_conv[i].astype(BF16), w_br_attn[i].astype(BF16), attn_sinks[i],
            w_out[i].astype(BF16), ln1_g[i][None, :], ln1_b[i][None, :],
            wr_hi, wr_lo, b_router[i][:, None], p_rows)
        x1 = x1.reshape(n, d)

        counts = cnt[:, 0]
        padded = ((counts + CHUNK - 1) // CHUNK) * CHUNK
        end_padded = jnp.cumsum(padded)
        start_padded = end_padded - padded
        n_chunks = p_rows // CHUNK
        chunk_pos = jnp.arange(n_chunks, dtype=I32)
        chunk_start = chunk_pos * CHUNK
        chunk_e = jnp.minimum(
            jnp.sum((end_padded[None, :] <= chunk_start[:, None]).astype(I32), axis=1),
            N_EXPERTS - 1)
        n_used = (end_padded[-1:] // CHUNK).astype(I32)
        eid = jnp.arange(N_EXPERTS, dtype=I32)
        onehot = (chunk_e[:, None] == eid[None, :]).astype(I32)
        chunk_no = chunk_pos - jnp.sum(onehot * (start_padded // CHUNK)[None, :], axis=1)
        in_table = onehot[:, :, None] * (chunk_no[:, None, None] == jnp.arange(LANES, dtype=I32)[None, None, :])
        chunk_blk = jnp.where(chunk_pos < n_used[0], jnp.sum(in_table * alloc[None, :, :], axis=(1, 2)),
                              chunk_pos).astype(I32)
        active = padded > 0
        run_idx = jnp.cumsum(active.astype(I32)) - 1
        later = active[None, :] & (eid[None, :] > eid[:, None])
        nxt = jnp.min(jnp.where(later, eid[None, :], N_EXPERTS), axis=1)
        nxt = jnp.where(nxt == N_EXPERTS, -1, nxt)
        run_slot = jnp.sum(onehot * (run_idx & 1)[None, :], axis=1).astype(I32)
        next_e = jnp.sum(onehot * nxt[None, :], axis=1).astype(I32)
        chunk_fill = jnp.clip(
            jnp.sum(onehot * (counts + start_padded)[None, :], axis=1) - chunk_start, 0, CHUNK).astype(I32)
        ys = _moe(chunk_e, n_used, run_slot, next_e, chunk_fill, chunk_blk, xs, w_gu[i],
                  b_gu[i][:, None, :], w_down[i], b_down[i][:, None, :], p_rows)
        out = _fin(dest, ys, x1, w_top.T, p[i].reshape(n, D_PLE),
                   w_ple_gate[i].astype(BF16), w_ple_proj[i].astype(BF16),
                   ln2_g[i][None, :], ln2_b[i][None, :])
        x = out.reshape(bsz, seq, d)
    return x
```
